```python
import jax, jax.numpy as jnp
from jax import lax
import numpy as np

D_MODEL = 1024
BATCH = 2
SEQ = 8192
DEPTH = 2

HEAD_DIM = 64
BLOCK_Q = 128
ROPE_THETA = 10000.0
GRID_W = 64
EPS = 1e-6
D_FF = 2816
N_EVEN = (DEPTH + 1) // 2
N_ODD = DEPTH // 2
GMLP_GROUPS = 8
GMLP_GROUP_DIM = 64
GMLP_CHUNK = 128
GMLP_WIDTH = GMLP_GROUPS * GMLP_GROUP_DIM
DIFF_HEADS = 4
DIFF_D = 64
DIFF_V = 2 * DIFF_D
MLA_HEADS = 8
MLA_Q_RANK = 256
MLA_KV_RANK = 128
MLA_NOPE = 64
MLA_ROPE = 32
MLA_V = 64
GQA_Q_HEADS = 8
GQA_KV_HEADS = 2
GQA_GROUP = GQA_Q_HEADS // GQA_KV_HEADS
GQA_DIM = 64

EVEN_IN = 2 * GMLP_WIDTH + 2 * DIFF_HEADS * 2 * DIFF_D + DIFF_HEADS * DIFF_V
EVEN_OUT = GMLP_WIDTH + DIFF_HEADS * DIFF_V
ODD_IN = (MLA_Q_RANK + MLA_KV_RANK + MLA_ROPE
          + GQA_Q_HEADS * GQA_DIM + 2 * GQA_KV_HEADS * GQA_DIM)
ODD_OUT = MLA_HEADS * MLA_V + GQA_Q_HEADS * GQA_DIM

kernel_name = "hybrid_gmlp_diffattn_mla_axialgqa_macaron"


def rmsnorm(x, g):
    xf = x.astype(jnp.float32)
    y = xf * lax.rsqrt(jnp.mean(xf * xf, axis=-1, keepdims=True) + EPS)
    return (y * g.astype(jnp.float32)).astype(x.dtype)


def rope_angles(pos, dim):
    inv = ROPE_THETA ** (-jnp.arange(0, dim, 2, dtype=jnp.float32) / dim)
    ang = pos.astype(jnp.float32)[:, None] * inv[None, :]
    return jnp.cos(ang), jnp.sin(ang)


def apply_rope(x, cos, sin):
    shape = (cos.shape[0],) + (1,) * (x.ndim - 3) + (cos.shape[-1],)
    c = cos.reshape(shape)
    s = sin.reshape(shape)
    xf = x.astype(jnp.float32)
    x1, x2 = jnp.split(xf, 2, axis=-1)
    return jnp.concatenate([x1 * c - x2 * s, x1 * s + x2 * c], axis=-1).astype(x.dtype)


def axial_rope(x, row_cs, col_cs):
    half = x.shape[-1] // 2
    return jnp.concatenate([apply_rope(x[..., :half], *row_cs),
                            apply_rope(x[..., half:], *col_cs)], axis=-1)


def to_blocks(t):
    b, s = t.shape[:2]
    t = t.reshape((b, s // BLOCK_Q, BLOCK_Q) + t.shape[2:])
    return jnp.moveaxis(t, 1, 0)


def from_blocks(t):
    t = jnp.moveaxis(t, 0, 1)
    return t.reshape((t.shape[0], t.shape[1] * t.shape[2]) + t.shape[3:])


def gqa_attention(q, k, v):
    scale = q.shape[-1] ** -0.5

    def blk(qb):
        s = jnp.einsum('bqhgd,bshd->bhgqs', qb, k).astype(jnp.float32) * scale
        p = jax.nn.softmax(s, axis=-1).astype(v.dtype)
        return jnp.einsum('bhgqs,bshe->bqhge', p, v)

    return from_blocks(lax.map(blk, to_blocks(q)))


def swiglu(h, w_gu, w_down):
    g, u = jnp.split(h @ w_gu, 2, axis=-1)
    return (jax.nn.silu(g) * u) @ w_down


def mixer_gmlp(u, v, sgu_norm, w_s, b_s):
    b, s, _ = v.shape
    vc = v.reshape(b, s // GMLP_CHUNK, GMLP_CHUNK, GMLP_GROUPS, GMLP_GROUP_DIM)
    vc = rmsnorm(vc, sgu_norm)
    mixed = jnp.einsum('gij,bcjgd->bcigd', w_s, vc) + b_s.T[:, :, None]
    return u * mixed.reshape(b, s, GMLP_WIDTH)


def mixer_diff(q, k, v, q_norm, k_norm, lam_q1, lam_k1, lam_q2, lam_k2,
               sub_norm, lam_init, cos, sin):
    b, s = q.shape[:2]
    q = apply_rope(rmsnorm(q, q_norm), cos, sin)
    k = apply_rope(rmsnorm(k, k_norm), cos, sin)
    f32 = jnp.float32
    lam = (jnp.exp(jnp.sum(lam_q1.astype(f32) * lam_k1.astype(f32)))
           - jnp.exp(jnp.sum(lam_q2.astype(f32) * lam_k2.astype(f32))) + lam_init)
    scale = DIFF_D ** -0.5

    def blk(qb):
        sc = jnp.einsum('bqhmd,bshmd->bhmqs', qb, k).astype(f32) * scale
        p = jax.nn.softmax(sc, axis=-1)
        w = (p[:, :, 0] - lam * p[:, :, 1]).astype(v.dtype)
        return jnp.einsum('bhqs,bshe->bqhe', w, v)

    o = from_blocks(lax.map(blk, to_blocks(q)))
    o = rmsnorm(o, sub_norm) * (1.0 - lam_init)
    return o.reshape(b, s, DIFF_HEADS * DIFF_V)


def setup_inputs(seed: int = 0) -> dict:
    key = jax.random.key(seed)
    ks = iter(jax.random.split(key, 40))
    f32 = jnp.float32

    def w(shape, fan_in):
        return jax.random.normal(next(ks), shape, f32) * (fan_in ** -0.5)

    def gain(shape):
        return 1.0 + 0.02 * jax.random.normal(next(ks), shape, f32)

    def small(shape, scale):
        return scale * jax.random.normal(next(ks), shape, f32)

    d = D_MODEL
    return {
        "x": jax.random.normal(next(ks), (BATCH, SEQ, d), f32),
        "ffn1_norm": gain((DEPTH, d)),
        "ffn1_w_gu": w((DEPTH, d, 2 * D_FF), d),
        "ffn1_w_down": w((DEPTH, D_FF, d), D_FF),
        "ffn2_norm": gain((DEPTH, d)),
        "ffn2_w_gu": w((DEPTH, d, 2 * D_FF), d),
        "ffn2_w_down": w((DEPTH, D_FF, d), D_FF),
        "ev_norm": gain((N_EVEN, d)),
        "ev_w_in": w((N_EVEN, d, EVEN_IN), d),
        "ev_sgu_norm": gain((N_EVEN, GMLP_GROUPS, GMLP_GROUP_DIM)),
        "ev_w_s": w((N_EVEN, GMLP_GROUPS, GMLP_CHUNK, GMLP_CHUNK), GMLP_CHUNK),
        "ev_b_s": gain((N_EVEN, GMLP_GROUPS, GMLP_CHUNK)),
        "ev_q_norm": gain((N_EVEN, DIFF_D)),
        "ev_k_norm": gain((N_EVEN, DIFF_D)),
        "ev_lam_q1": small((N_EVEN, DIFF_D), 0.1),
        "ev_lam_k1": small((N_EVEN, DIFF_D), 0.1),
        "ev_lam_q2": small((N_EVEN, DIFF_D), 0.1),
        "ev_lam_k2": small((N_EVEN, DIFF_D), 0.1),
        "ev_sub_norm": gain((N_EVEN, DIFF_V)),
        "ev_w_out": w((N_EVEN, EVEN_OUT, d), EVEN_OUT),
        "od_norm": gain((N_ODD, d)),
        "od_w_in": w((N_ODD, d, ODD_IN), d),
        "od_cq_norm": gain((N_ODD, MLA_Q_RANK)),
        "od_ckv_norm": gain((N_ODD, MLA_KV_RANK)),
        "od_w_uq": w((N_ODD, MLA_Q_RANK, MLA_HEADS * (MLA_NOPE + MLA_ROPE)), MLA_Q_RANK),
        "od_w_ukv": w((N_ODD, MLA_KV_RANK, MLA_HEADS * (MLA_NOPE + MLA_V)), MLA_KV_RANK),
        "od_mla_q_norm": gain((N_ODD, MLA_NOPE + MLA_ROPE)),
        "od_mla_k_norm": gain((N_ODD, MLA_NOPE + MLA_ROPE)),
        "od_gqa_q_norm": gain((N_ODD, GQA_DIM)),
        "od_gqa_k_norm": gain((N_ODD, GQA_DIM)),
        "od_w_out": w((N_ODD, ODD_OUT, d), ODD_OUT),
    }


def reference(x, ffn1_norm, ffn1_w_gu, ffn1_w_down, ffn2_norm, ffn2_w_gu, ffn2_w_down,
              ev_norm, ev_w_in, ev_sgu_norm, ev_w_s, ev_b_s, ev_q_norm, ev_k_norm,
              ev_lam_q1, ev_lam_k1, ev_lam_q2, ev_lam_k2, ev_sub_norm, ev_w_out,
              od_norm, od_w_in, od_cq_norm, od_ckv_norm, od_w_uq, od_w_ukv,
              od_mla_q_norm, od_mla_k_norm, od_gqa_q_norm, od_gqa_k_norm, od_w_out):
    b, s, _ = x.shape
    rows = s // GRID_W
    pos = jnp.arange(s, dtype=jnp.int32)
    row_idx = jnp.repeat(jnp.arange(rows, dtype=jnp.int32), GRID_W)
    col_idx = jnp.tile(jnp.arange(GRID_W, dtype=jnp.int32), rows)
    cs_full = rope_angles(pos, DIFF_D)
    cs_mla = rope_angles(pos, MLA_ROPE)
    cs_row = rope_angles(row_idx, GQA_DIM // 2)
    cs_col = rope_angles(col_idx, GQA_DIM // 2)

    for l in range(DEPTH):
        x = x + 0.5 * swiglu(rmsnorm(x, ffn1_norm[l]), ffn1_w_gu[l], ffn1_w_down[l])
        if l % 2 == 0:
            i = l // 2
            proj = rmsnorm(x, ev_norm[i]) @ ev_w_in[i]
            a_uv, bq, bk, bv = jnp.split(
                proj, [2 * GMLP_WIDTH, 2 * GMLP_WIDTH + 2 * DIFF_HEADS * DIFF_D,
                       2 * GMLP_WIDTH + 4 * DIFF_HEADS * DIFF_D], axis=-1)
            u, v = jnp.split(jax.nn.gelu(a_uv), 2, axis=-1)
            out_a = mixer_gmlp(u, v, ev_sgu_norm[i], ev_w_s[i], ev_b_s[i])
            lam_init = 0.8 - 0.6 * float(np.exp(-0.3 * l))
            out_b = mixer_diff(
                bq.reshape(b, s, DIFF_HEADS, 2, DIFF_D),
                bk.reshape(b, s, DIFF_HEADS, 2, DIFF_D),
                bv.reshape(b, s, DIFF_HEADS, DIFF_V),
                ev_q_norm[i], ev_k_norm[i], ev_lam_q1[i], ev_lam_k1[i],
                ev_lam_q2[i], ev_lam_k2[i], ev_sub_norm[i], lam_init, *cs_full)
            mix = jnp.concatenate([out_a, out_b], axis=-1) @ ev_w_out[i]
        else:
            i = l // 2
            proj = rmsnorm(x, od_norm[i]) @ od_w_in[i]
            o1 = MLA_Q_RANK
            o2 = o1 + MLA_KV_RANK
            o3 = o2 + MLA_ROPE
            o4 = o3 + GQA_Q_HEADS * GQA_DIM
            o5 = o4 + GQA_KV_HEADS * GQA_DIM
            c_q, c_kv, k_pe, gq, gk, gv = jnp.split(proj, [o1, o2, o3, o4, o5], axis=-1)
            q = (rmsnorm(c_q, od_cq_norm[i]) @ od_w_uq[i]).reshape(
                b, s, MLA_HEADS, MLA_NOPE + MLA_ROPE)
            kv = (rmsnorm(c_kv, od_ckv_norm[i]) @ od_w_ukv[i]).reshape(
                b, s, MLA_HEADS, MLA_NOPE + MLA_V)
            k_nope, v_c = jnp.split(kv, [MLA_NOPE], axis=-1)
            k = jnp.concatenate(
                [k_nope, jnp.broadcast_to(k_pe[:, :, None, :], (b, s, MLA_HEADS, MLA_ROPE))],
                axis=-1)
            q = rmsnorm(q, od_mla_q_norm[i])
            k = rmsnorm(k, od_mla_k_norm[i])
            q = jnp.concatenate([q[..., :MLA_NOPE], apply_rope(q[..., MLA_NOPE:], *cs_mla)], axis=-1)
            k = jnp.concatenate([k[..., :MLA_NOPE], apply_rope(k[..., MLA_NOPE:], *cs_mla)], axis=-1)
            out_c = gqa_attention(q[:, :, :, None, :], k, v_c).reshape(b, s, MLA_HEADS * MLA_V)
            qd = rmsnorm(gq.reshape(b, s, GQA_KV_HEADS, GQA_GROUP, GQA_DIM), od_gqa_q_norm[i])
            kd = rmsnorm(gk.reshape(b, s, GQA_KV_HEADS, GQA_DIM), od_gqa_k_norm[i])
            vd = gv.reshape(b, s, GQA_KV_HEADS, GQA_DIM)
            qd = axial_rope(qd, cs_row, cs_col)
            kd = axial_rope(kd, cs_row, cs_col)
            out_d = gqa_attention(qd, kd, vd).reshape(b, s, GQA_Q_HEADS * GQA_DIM)
            mix = jnp.concatenate([out_c, out_d], axis=-1) @ od_w_out[i]
        x = x + mix
        x = x + 0.5 * swiglu(rmsnorm(x, ffn2_norm[l]), ffn2_w_gu[l], ffn2_w_down[l])
    return x
```

```python
import functools
import math

import numpy as np
import jax
import jax.numpy as jnp
from jax import lax
from jax.experimental import pallas as pl
from jax.experimental.pallas import tpu as pltpu

D_MODEL = 1024
D_FF = 2816
ROPE_THETA = 10000.0
GRID_W = 64
EPS = 1e-6
GMLP_GROUPS = 8
GMLP_GROUP_DIM = 64
GMLP_CHUNK = 128
GMLP_WIDTH = GMLP_GROUPS * GMLP_GROUP_DIM
DIFF_HEADS = 4
DIFF_D = 64
DIFF_V = 128
MLA_HEADS = 8
MLA_Q_RANK = 256
MLA_KV_RANK = 128
MLA_NOPE = 64
MLA_ROPE = 32
MLA_V = 64
MLA_QK = MLA_NOPE + MLA_ROPE
GQA_Q_HEADS = 8
GQA_KV_HEADS = 2
GQA_GROUP = GQA_Q_HEADS // GQA_KV_HEADS
GQA_DIM = 64

LANES = 128
VMEM_LIMIT_BYTES = 56 * 1024 * 1024

FFN_TM = 256
PROJ_TM = 256
ATT_TQ = 256
ATT_TK = 256

BF16 = jnp.bfloat16
F32 = jnp.float32


def _params(semantics):
    return pltpu.CompilerParams(dimension_semantics=semantics,
                                vmem_limit_bytes=VMEM_LIMIT_BYTES)


def _const_spec(shape):
    nd = len(shape)
    return pl.BlockSpec(shape, lambda *_: (0,) * nd, pipeline_mode=pl.Buffered(1))


def _rms_rows(x, gain):
    ms = jnp.mean(x * x, axis=-1, keepdims=True)
    return x * lax.rsqrt(ms + EPS) * gain


def _dot(a, b):
    return jnp.dot(a, b, preferred_element_type=F32)


def _lane_is_low(shape):
    return lax.broadcasted_iota(jnp.int32, shape, len(shape) - 1) < (LANES // 2)


def _rms_half_blocks(x, gain):
    low = _lane_is_low(x.shape)
    x2 = x * x
    x2_lo = jnp.where(low, x2, 0.0)
    x2_hi = x2 - x2_lo
    ms_lo = jnp.sum(x2_lo, axis=-1, keepdims=True) * (2.0 / LANES)
    ms_hi = jnp.sum(x2_hi, axis=-1, keepdims=True) * (2.0 / LANES)
    r = jnp.where(low, lax.rsqrt(ms_lo + EPS), lax.rsqrt(ms_hi + EPS))
    return x * r * gain


def _rope_block(x, cos, sin_up, sin_dn, shift):
    up = pltpu.roll(x, LANES - shift, axis=1)
    dn = pltpu.roll(x, shift, axis=1)
    return x * cos + up * sin_up + dn * sin_dn


def _ffn_body(x, g_ref, wgu_ref, wd_ref, o_ref):
    xn = _rms_rows(x, g_ref[...]).astype(BF16)
    h = _dot(xn, wgu_ref[...])
    gate = h[:, :D_FF]
    up = h[:, D_FF:]
    act = (gate / (1.0 + jnp.exp(-gate)) * up).astype(BF16)
    o_ref[...] = x + 0.5 * _dot(act, wd_ref[...])


def _ffn_kernel(x_ref, g_ref, wgu_ref, wd_ref, o_ref):
    _ffn_body(x_ref[...], g_ref, wgu_ref, wd_ref, o_ref)


def _mix_ffn_kernel(x_ref, a_ref, b_ref, wa_ref, wb_ref, g_ref, wgu_ref, wd_ref, o_ref):
    x = x_ref[...] + _dot(a_ref[...], wa_ref[...]) + _dot(b_ref[...], wb_ref[...])
    _ffn_body(x, g_ref, wgu_ref, wd_ref, o_ref)


def _ffn_call(x, gain, w_gu, w_down, mix=None):
    t = x.shape[0]
    tm = FFN_TM
    row = lambda i: (i, 0)
    x_spec = pl.BlockSpec((tm, D_MODEL), row)
    w_specs = [_const_spec((1, D_MODEL)), _const_spec((D_MODEL, 2 * D_FF)),
               _const_spec((D_FF, D_MODEL))]
    w_args = [gain.reshape(1, D_MODEL), w_gu.astype(BF16), w_down.astype(BF16)]
    if mix is None:
        kern, in_specs, args = _ffn_kernel, [x_spec] + w_specs, [x] + w_args
    else:
        a, b, wa, wb = mix
        kern = _mix_ffn_kernel
        in_specs = [x_spec, pl.BlockSpec((tm, a.shape[1]), row), pl.BlockSpec((tm, b.shape[1]), row),
                    _const_spec(wa.shape), _const_spec(wb.shape)] + w_specs
        args = [x, a, b, wa, wb] + w_args
    return pl.pallas_call(
        kern, grid=(t // tm,), in_specs=in_specs,
        out_specs=pl.BlockSpec((tm, D_MODEL), row),
        out_shape=jax.ShapeDtypeStruct((t, D_MODEL), F32),
        compiler_params=_params(("parallel",)),
        name="ffn" if mix is None else "mix_ffn",
    )(*args)


def _gelu_tanh(x):
    c = math.sqrt(2.0 / math.pi)
    return 0.5 * x * (1.0 + jnp.tanh(c * (x + 0.044715 * (x * x * x))))


def _even_proj_kernel(x_ref, g_ref, win_ref, sgu_ref, wpair_ref, bias_ref, qg_ref, kg_ref,
                      cos_ref, sup_ref, sdn_ref, oa_ref, q_ref, k_ref, v_ref):
    xn = _rms_rows(x_ref[...], g_ref[...]).astype(BF16)
    proj = _dot(xn, win_ref[...])
    w = GMLP_WIDTH
    nblk = w // LANES
    tm = proj.shape[0]
    for j in range(nblk):
        u = _gelu_tanh(proj[:, j * LANES:(j + 1) * LANES])
        v = _gelu_tanh(proj[:, w + j * LANES:w + (j + 1) * LANES])
        vn = _rms_half_blocks(v, sgu_ref[:, j * LANES:(j + 1) * LANES])
        low = _lane_is_low(vn.shape)
        vn_lo = jnp.where(low, vn, 0.0).astype(BF16)
        vn_hi = jnp.where(low, 0.0, vn).astype(BF16)
        wp = wpair_ref[j]
        bias = bias_ref[:, j * LANES:(j + 1) * LANES]
        for c in range(tm // GMLP_CHUNK):
            rows = slice(c * GMLP_CHUNK, (c + 1) * GMLP_CHUNK)
            stacked = jnp.concatenate([vn_lo[rows], vn_hi[rows]], axis=0)
            mixed = _dot(wp, stacked) + bias
            oa_ref[rows, j * LANES:(j + 1) * LANES] = (u[rows] * mixed).astype(BF16)
    cos, sup, sdn = cos_ref[...], sup_ref[...], sdn_ref[...]
    for j in range(nblk):
        qb = proj[:, 2 * w + j * LANES:2 * w + (j + 1) * LANES]
        kb = proj[:, 3 * w + j * LANES:3 * w + (j + 1) * LANES]
        qb = _rope_block(_rms_half_blocks(qb, qg_ref[...]), cos, sup, sdn, DIFF_D // 2)
        kb = _rope_block(_rms_half_blocks(kb, kg_ref[...]), cos, sup, sdn, DIFF_D // 2)
        q_ref[:, j * LANES:(j + 1) * LANES] = (qb * (DIFF_D ** -0.5)).astype(BF16)
        k_ref[:, j * LANES:(j + 1) * LANES] = kb.astype(BF16)
    v_ref[...] = proj[:, 4 * w:].astype(BF16)


def _even_proj_call(x, seq, gain, w_in, sgu_norm, w_s, b_s, q_norm, k_norm, tables):
    t = x.shape[0]
    tm = PROJ_TM
    w = GMLP_WIDTH
    row = lambda i: (i, 0)
    pos = lambda i: (i % (seq // tm), 0)
    wpair = jnp.concatenate([w_s[0::2], w_s[1::2]], axis=2).astype(BF16)
    bias = jnp.repeat(b_s.T, GMLP_GROUP_DIM, axis=1)
    two = lambda g: jnp.tile(g, 2).reshape(1, LANES)
    args = [x, gain.reshape(1, D_MODEL), w_in.astype(BF16), sgu_norm.reshape(1, w), wpair, bias,
            two(q_norm), two(k_norm), *tables]
    in_specs = [pl.BlockSpec((tm, D_MODEL), row), _const_spec((1, D_MODEL)),
                _const_spec(w_in.shape), _const_spec((1, w)), _const_spec(wpair.shape),
                _const_spec(bias.shape), _const_spec((1, LANES)), _const_spec((1, LANES)),
                pl.BlockSpec((tm, LANES), pos), pl.BlockSpec((tm, LANES), pos),
                pl.BlockSpec((tm, LANES), pos)]
    out = jax.ShapeDtypeStruct((t, w), BF16)
    return pl.pallas_call(
        _even_proj_kernel, grid=(t // tm,), in_specs=in_specs,
        out_specs=[pl.BlockSpec((tm, w), row)] * 4, out_shape=[out] * 4,
        compiler_params=_params(("parallel",)), name="even_proj",
    )(*args)


def _rms_padded_block(x, gain, width):
    ms = jnp.sum(x * x, axis=-1, keepdims=True) * (1.0 / width)
    return x * lax.rsqrt(ms + EPS) * gain


def _odd_proj_kernel(x_ref, g_ref, win_ref, cqg_ref, ckvg_ref, wuq_ref, wuk_ref, wuv_ref,
                     mqg_ref, mkg_ref, gqg_ref, gkg_ref,
                     mcos_ref, msup_ref, msdn_ref, acos_ref, asup_ref, asdn_ref,
                     qc_ref, kc_ref, vc_ref, qd_ref, kd_ref, vd_ref):
    xn = _rms_rows(x_ref[...], g_ref[...]).astype(BF16)
    proj = _dot(xn, win_ref[...])
    o1 = MLA_Q_RANK
    o2 = o1 + MLA_KV_RANK
    o3 = o2 + LANES
    o4 = o3 + GQA_Q_HEADS * GQA_DIM
    o5 = o4 + LANES
    mcos, msup, msdn = mcos_ref[...], msup_ref[...], msdn_ref[...]
    cq = _rms_rows(proj[:, :o1], cqg_ref[...]).astype(BF16)
    q_all = _dot(cq, wuq_ref[...])
    ckv = _rms_rows(proj[:, o1:o2], ckvg_ref[...]).astype(BF16)
    kn_all = _dot(ckv, wuk_ref[...])
    vc_ref[...] = _dot(ckv, wuv_ref[...]).astype(BF16)
    kpe = proj[:, o2:o3]
    for h in range(MLA_HEADS):
        blk = slice(h * LANES, (h + 1) * LANES)
        qb = _rms_padded_block(q_all[:, blk], mqg_ref[...], MLA_QK)
        qb = _rope_block(qb, mcos, msup, msdn, MLA_ROPE // 2)
        qc_ref[:, blk] = (qb * (MLA_QK ** -0.5)).astype(BF16)
        kb = _rms_padded_block(kn_all[:, blk] + kpe, mkg_ref[...], MLA_QK)
        kc_ref[:, blk] = _rope_block(kb, mcos, msup, msdn, MLA_ROPE // 2).astype(BF16)
    acos, asup, asdn = acos_ref[...], asup_ref[...], asdn_ref[...]
    for j in range(GQA_Q_HEADS * GQA_DIM // LANES):
        blk = slice(j * LANES, (j + 1) * LANES)
        qb = _rms_half_blocks(proj[:, o3 + j * LANES:o3 + (j + 1) * LANES], gqg_ref[...])
        qb = _rope_block(qb, acos, asup, asdn, GQA_DIM // 4)
        qd_ref[:, blk] = (qb * (GQA_DIM ** -0.5)).astype(BF16)
    kb = _rms_half_blocks(proj[:, o4:o5], gkg_ref[...])
    kd_ref[...] = _rope_block(kb, acos, asup, asdn, GQA_DIM // 4).astype(BF16)
    vd_ref[...] = proj[:, o5:].astype(BF16)


def _pad_heads(w, heads, width):
    r = w.shape[0]
    w = w.reshape(r, heads, width)
    return jnp.pad(w, ((0, 0), (0, 0), (0, LANES - width))).reshape(r, heads * LANES)


def _odd_proj_call(x, seq, gain, w_in, cq_norm, ckv_norm, w_uq, w_ukv, mq_norm, mk_norm,
                   gq_norm, gk_norm, mla_tables, axial_tables):
    t = x.shape[0]
    tm = PROJ_TM
    row = lambda i: (i, 0)
    pos = lambda i: (i % (seq // tm), 0)
    o1 = MLA_Q_RANK
    o2 = o1 + MLA_KV_RANK
    o3 = o2 + MLA_ROPE
    o4 = o3 + GQA_Q_HEADS * GQA_DIM
    o5 = o4 + GQA_KV_HEADS * GQA_DIM
    kpe_cols = jnp.pad(w_in[:, o2:o3], ((0, 0), (MLA_NOPE, LANES - MLA_QK)))
    gq_cols = w_in[:, o3:o4].reshape(D_MODEL, GQA_KV_HEADS, GQA_GROUP, GQA_DIM)
    gq_cols = gq_cols.transpose(0, 2, 1, 3).reshape(D_MODEL, GQA_Q_HEADS * GQA_DIM)
    win = jnp.concatenate([w_in[:, :o2], kpe_cols, gq_cols, w_in[:, o4:]], axis=1).astype(BF16)
    wuq = _pad_heads(w_uq, MLA_HEADS, MLA_QK).astype(BF16)
    w_ukv = w_ukv.reshape(MLA_KV_RANK, MLA_HEADS, MLA_NOPE + MLA_V)
    wuk = _pad_heads(w_ukv[:, :, :MLA_NOPE].reshape(MLA_KV_RANK, -1), MLA_HEADS, MLA_NOPE).astype(BF16)
    wuv = w_ukv[:, :, MLA_NOPE:].reshape(MLA_KV_RANK, MLA_HEADS * MLA_V).astype(BF16)
    pad_gain = lambda g: jnp.pad(g, (0, LANES - MLA_QK)).reshape(1, LANES)
    two = lambda g: jnp.tile(g, 2).reshape(1, LANES)
    args = [x, gain.reshape(1, D_MODEL), win, cq_norm.reshape(1, -1), ckv_norm.reshape(1, -1),
            wuq, wuk, wuv, pad_gain(mq_norm), pad_gain(mk_norm), two(gq_norm), two(gk_norm),
            *mla_tables, *axial_tables]
    in_specs = ([pl.BlockSpec((tm, D_MODEL), row)]
                + [_const_spec(a.shape) for a in args[1:12]]
                + [pl.BlockSpec((tm, LANES), pos)] * 6)
    widths = [MLA_HEADS * LANES, MLA_HEADS * LANES, MLA_HEADS * MLA_V,
              GQA_Q_HEADS * GQA_DIM, LANES, LANES]
    return pl.pallas_call(
        _odd_proj_kernel, grid=(t // tm,), in_specs=in_specs,
        out_specs=[pl.BlockSpec((tm, n), row) for n in widths],
        out_shape=[jax.ShapeDtypeStruct((t, n), BF16) for n in widths],
        compiler_params=_params(("parallel",)), name="odd_proj",
    )(*args)


def _attention_kernel(*refs, mode, lam_init):
    if mode == "diff":
        q_ref, k_ref, v_ref, lam_ref, sub_ref, o_ref, m_ref, l_ref, acc_ref = refs
    else:
        q_ref, k_ref, v_ref, o_ref, m_ref, l_ref, acc_ref = refs
    tq = q_ref.shape[0]
    seq = k_ref.shape[0]
    tk = ATT_TK
    q = q_ref[...].astype(F32)
    if mode == "mla":
        qt = [q[:, :LANES].T.astype(BF16), q[:, LANES:].T.astype(BF16)]
    else:
        low = _lane_is_low(q.shape)
        qs = jnp.concatenate([jnp.where(low, q, 0.0), jnp.where(low, 0.0, q)], axis=0)
        qt = qs.T.astype(BF16)
    m_ref[...] = jnp.full(m_ref.shape, -jnp.inf, F32)
    l_ref[...] = jnp.zeros(l_ref.shape, F32)
    acc_ref[...] = jnp.zeros(acc_ref.shape, F32)

    def step(t, carry):
        rows = pl.ds(pl.multiple_of(t * tk, tk), tk)
        kt = k_ref[rows, :]
        vt = v_ref[rows, :]
        if mode == "mla":
            s = jnp.concatenate([_dot(kt[:, :LANES], qt[0]), _dot(kt[:, LANES:], qt[1])], axis=1)
        else:
            s = _dot(kt, qt)
        m_old = m_ref[...]
        m_new = jnp.maximum(m_old, jnp.max(s, axis=0, keepdims=True))
        alpha = jnp.exp(m_old - m_new)
        p = jnp.exp(s - m_new)
        l_ref[...] = alpha * l_ref[...] + jnp.sum(p, axis=0, keepdims=True)
        pv = lax.dot_general(vt, p.astype(BF16), (((0,), (0,)), ((), ())),
                             preferred_element_type=F32)
        acc_ref[...] = alpha * acc_ref[...] + pv
        m_ref[...] = m_new
        return carry

    lax.fori_loop(0, seq // tk, step, 0)

    o = acc_ref[...] / l_ref[...]
    if mode == "diff":
        lv = lam_ref[...]
        lam = (jnp.exp(jnp.sum(lv[0:1] * lv[1:2], axis=-1, keepdims=True))
               - jnp.exp(jnp.sum(lv[2:3] * lv[3:4], axis=-1, keepdims=True)) + lam_init)
        ot = (o[:, :tq] - lam * o[:, tq:]).T
        o_ref[...] = (_rms_rows(ot, sub_ref[...]) * (1.0 - lam_init)).astype(o_ref.dtype)
    else:
        half = LANES // 2
        ot = jnp.concatenate([o[:half, :tq], o[half:, tq:]], axis=0).T
        o_ref[...] = ot.astype(o_ref.dtype)


def _attention_call(q, k, v, mode, extra=(), lam_init=0.0):
    b, seq, _ = q.shape
    qw = 2 * LANES if mode == "mla" else LANES
    nblk = q.shape[2] // qw
    tq = ATT_TQ
    shared = mode == "gqa"
    kv_idx = (lambda bi, j, i: (bi, 0, 0)) if shared else (lambda bi, j, i: (bi, 0, j))
    in_specs = [pl.BlockSpec((None, tq, qw), lambda bi, j, i: (bi, i, j)),
                pl.BlockSpec((None, seq, qw), kv_idx),
                pl.BlockSpec((None, seq, LANES), kv_idx)]
    in_specs += [_const_spec(e.shape) for e in extra]
    return pl.pallas_call(
        functools.partial(_attention_kernel, mode=mode, lam_init=lam_init),
        grid=(b, nblk, seq // tq), in_specs=in_specs,
        out_specs=pl.BlockSpec((None, tq, LANES), lambda bi, j, i: (bi, i, j)),
        out_shape=jax.ShapeDtypeStruct((b, seq, nblk * LANES), BF16),
        scratch_shapes=[pltpu.VMEM((1, 2 * tq), F32), pltpu.VMEM((1, 2 * tq), F32),
                        pltpu.VMEM((LANES, 2 * tq), F32)],
        compiler_params=_params(("parallel", "parallel", "arbitrary")),
        name="attn_" + mode,
    )(q, k, v, *extra)


def _angles(pos, dim):
    inv = ROPE_THETA ** (-jnp.arange(0, dim, 2, dtype=F32) / dim)
    ang = pos.astype(F32)[:, None] * inv[None, :]
    return jnp.cos(ang), jnp.sin(ang)


def _rope_tables(seq):
    pos = jnp.arange(seq, dtype=jnp.int32)
    zeros = lambda n: jnp.zeros((seq, n), F32)
    ones = lambda n: jnp.ones((seq, n), F32)
    c, s = _angles(pos, DIFF_D)
    z = zeros(DIFF_D // 2)
    full = (jnp.tile(jnp.concatenate([c, c], 1), (1, 2)),
            jnp.tile(jnp.concatenate([-s, z], 1), (1, 2)),
            jnp.tile(jnp.concatenate([z, s], 1), (1, 2)))
    c, s = _angles(pos, MLA_ROPE)
    z = zeros(MLA_ROPE // 2)
    tail = zeros(LANES - MLA_QK)
    mla = (jnp.concatenate([ones(MLA_NOPE), c, c, tail], 1),
           jnp.concatenate([zeros(MLA_NOPE), -s, z, tail], 1),
           jnp.concatenate([zeros(MLA_NOPE), z, s, tail], 1))
    cr, sr = _angles(pos // GRID_W, GQA_DIM // 2)
    cc, sc = _angles(pos % GRID_W, GQA_DIM // 2)
    z = zeros(GQA_DIM // 4)
    axial = (jnp.tile(jnp.concatenate([cr, cr, cc, cc], 1), (1, 2)),
             jnp.tile(jnp.concatenate([-sr, z, -sc, z], 1), (1, 2)),
             jnp.tile(jnp.concatenate([z, sr, z, sc], 1), (1, 2)))
    return full, mla, axial


def kernel(x, ffn1_norm, ffn1_w_gu, ffn1_w_down, ffn2_norm, ffn2_w_gu, ffn2_w_down, ev_norm, ev_w_in, ev_sgu_norm, ev_w_s, ev_b_s, ev_q_norm, ev_k_norm, ev_lam_q1, ev_lam_k1, ev_lam_q2, ev_lam_k2, ev_sub_norm, ev_w_out, od_norm, od_w_in, od_cq_norm, od_ckv_norm, od_w_uq, od_w_ukv, od_mla_q_norm, od_mla_k_norm, od_gqa_q_norm, od_gqa_k_norm, od_w_out):
    b, seq, d = x.shape
    t = b * seq
    full_tab, mla_tab, axial_tab = _rope_tables(seq)
    x2 = x.reshape(t, d)

    x2 = _ffn_call(x2, ffn1_norm[0], ffn1_w_gu[0], ffn1_w_down[0])
    out_a, q, k, v = _even_proj_call(x2, seq, ev_norm[0], ev_w_in[0], ev_sgu_norm[0], ev_w_s[0],
                                     ev_b_s[0], ev_q_norm[0], ev_k_norm[0], full_tab)
    lam_init = 0.8 - 0.6 * float(np.exp(-0.3 * 0))
    lam_vecs = jnp.stack([ev_lam_q1[0], ev_lam_k1[0], ev_lam_q2[0], ev_lam_k2[0]])
    shp = (b, seq, GMLP_WIDTH)
    out_b = _attention_call(q.reshape(shp), k.reshape(shp), v.reshape(shp), "diff",
                            extra=(lam_vecs, ev_sub_norm[0].reshape(1, DIFF_V)), lam_init=lam_init)
    w_out = ev_w_out[0].astype(BF16)
    x2 = _ffn_call(x2, ffn2_norm[0], ffn2_w_gu[0], ffn2_w_down[0],
                   mix=(out_a, out_b.reshape(t, -1), w_out[:GMLP_WIDTH], w_out[GMLP_WIDTH:]))

    x2 = _ffn_call(x2, ffn1_norm[1], ffn1_w_gu[1], ffn1_w_down[1])
    qc, kc, vc, qd, kd, vd = _odd_proj_call(
        x2, seq, od_norm[0], od_w_in[0], od_cq_norm[0], od_ckv_norm[0], od_w_uq[0], od_w_ukv[0],
        od_mla_q_norm[0], od_mla_k_norm[0], od_gqa_q_norm[0], od_gqa_k_norm[0], mla_tab, axial_tab)
    r3 = lambda a: a.reshape(b, seq, a.shape[1])
    out_c = _attention_call(r3(qc), r3(kc), r3(vc), "mla")
    out_d = _attention_call(r3(qd), r3(kd), r3(vd), "gqa")
    w_out = od_w_out[0].astype(BF16)
    n_c = MLA_HEADS * MLA_V
    w_d = w_out[n_c:].reshape(GQA_KV_HEADS, GQA_GROUP, GQA_DIM, d).transpose(1, 0, 2, 3)
    x2 = _ffn_call(x2, ffn2_norm[1], ffn2_w_gu[1], ffn2_w_down[1],
                   mix=(out_c.reshape(t, -1), out_d.reshape(t, -1), w_out[:n_c],
                        w_d.reshape(GQA_Q_HEADS * GQA_DIM, d)))
    return x2.reshape(b, seq, d)
```

```python
import functools
import math

import numpy as np
import jax
import jax.numpy as jnp
from jax import lax
from jax.experimental import pallas as pl
from jax.experimental.pallas import tpu as pltpu

D_MODEL = 1024
D_FF = 2816
ROPE_THETA = 10000.0
GRID_W = 64
EPS = 1e-6
GMLP_GROUPS = 8
GMLP_GROUP_DIM = 64
GMLP_CHUNK = 128
GMLP_WIDTH = GMLP_GROUPS * GMLP_GROUP_DIM
DIFF_HEADS = 4
DIFF_D = 64
DIFF_V = 128
MLA_HEADS = 8
MLA_Q_RANK = 256
MLA_KV_RANK = 128
MLA_NOPE = 64
MLA_ROPE = 32
MLA_V = 64
MLA_QK = MLA_NOPE + MLA_ROPE
GQA_Q_HEADS = 8
GQA_KV_HEADS = 2
GQA_GROUP = GQA_Q_HEADS // GQA_KV_HEADS
GQA_DIM = 64

LANES = 128
VMEM_LIMIT_BYTES = 56 * 1024 * 1024

FFN_TM = 256
PROJ_TM = 256
ATT_TQ = 512
ATT_TK = 256
ATT_CW = 256
ATT_PAIRS = 4
LOG2E = math.log2(math.e)

BF16 = jnp.bfloat16
F32 = jnp.float32


def _params(semantics):
    return pltpu.CompilerParams(dimension_semantics=semantics,
                                vmem_limit_bytes=VMEM_LIMIT_BYTES)


def _const_spec(shape):
    nd = len(shape)
    return pl.BlockSpec(shape, lambda *_: (0,) * nd, pipeline_mode=pl.Buffered(1))


def _rms_rows(x, gain):
    ms = jnp.mean(x * x, axis=-1, keepdims=True)
    return x * lax.rsqrt(ms + EPS) * gain


def _dot(a, b):
    return jnp.dot(a, b, preferred_element_type=F32)


def _lane_is_low(shape):
    return lax.broadcasted_iota(jnp.int32, shape, len(shape) - 1) < (LANES // 2)


def _rms_half_blocks(x, gain):
    low = _lane_is_low(x.shape)
    x2 = x * x
    x2_lo = jnp.where(low, x2, 0.0)
    x2_hi = x2 - x2_lo
    ms_lo = jnp.sum(x2_lo, axis=-1, keepdims=True) * (2.0 / LANES)
    ms_hi = jnp.sum(x2_hi, axis=-1, keepdims=True) * (2.0 / LANES)
    r = jnp.where(low, lax.rsqrt(ms_lo + EPS), lax.rsqrt(ms_hi + EPS))
    return x * r * gain


def _rope_block(x, cos, sin_up, sin_dn, shift):
    up = pltpu.roll(x, LANES - shift, axis=1)
    dn = pltpu.roll(x, shift, axis=1)
    return x * cos + up * sin_up + dn * sin_dn


def _ffn_body(x, g_ref, wgu_ref, wd_ref, o_ref):
    xn = _rms_rows(x, g_ref[...]).astype(BF16)
    h = _dot(xn, wgu_ref[...])
    gate = h[:, :D_FF]
    up = h[:, D_FF:]
    act = (gate / (1.0 + jnp.exp(-gate)) * up).astype(BF16)
    o_ref[...] = x + 0.5 * _dot(act, wd_ref[...])


def _ffn_kernel(x_ref, g_ref, wgu_ref, wd_ref, o_ref):
    _ffn_body(x_ref[...], g_ref, wgu_ref, wd_ref, o_ref)


def _mix_ffn_kernel(x_ref, a_ref, b_ref, wa_ref, wb_ref, g_ref, wgu_ref, wd_ref, o_ref):
    x = x_ref[...] + _dot(a_ref[...], wa_ref[...]) + _dot(b_ref[...], wb_ref[...])
    _ffn_body(x, g_ref, wgu_ref, wd_ref, o_ref)


def _ffn_call(x, gain, w_gu, w_down, mix=None):
    t = x.shape[0]
    tm = FFN_TM
    row = lambda i: (i, 0)
    x_spec = pl.BlockSpec((tm, D_MODEL), row)
    w_specs = [_const_spec((1, D_MODEL)), _const_spec((D_MODEL, 2 * D_FF)),
               _const_spec((D_FF, D_MODEL))]
    w_args = [gain.reshape(1, D_MODEL), w_gu.astype(BF16), w_down.astype(BF16)]
    if mix is None:
        kern, in_specs, args = _ffn_kernel, [x_spec] + w_specs, [x] + w_args
    else:
        a, b, wa, wb = mix
        kern = _mix_ffn_kernel
        in_specs = [x_spec, pl.BlockSpec((tm, a.shape[1]), row), pl.BlockSpec((tm, b.shape[1]), row),
                    _const_spec(wa.shape), _const_spec(wb.shape)] + w_specs
        args = [x, a, b, wa, wb] + w_args
    return pl.pallas_call(
        kern, grid=(t // tm,), in_specs=in_specs,
        out_specs=pl.BlockSpec((tm, D_MODEL), row),
        out_shape=jax.ShapeDtypeStruct((t, D_MODEL), F32),
        compiler_params=_params(("parallel",)),
        name="ffn" if mix is None else "mix_ffn",
    )(*args)


def _gelu_tanh(x):
    c = math.sqrt(2.0 / math.pi)
    return 0.5 * x * (1.0 + jnp.tanh(c * (x + 0.044715 * (x * x * x))))


def _even_proj_kernel(x_ref, g_ref, win_ref, sgu_ref, wpair_ref, bias_ref, qg_ref, kg_ref,
                      cos_ref, sup_ref, sdn_ref, oa_ref, q_ref, k_ref, v_ref):
    xn = _rms_rows(x_ref[...], g_ref[...]).astype(BF16)
    proj = _dot(xn, win_ref[...])
    w = GMLP_WIDTH
    nblk = w // LANES
    tm = proj.shape[0]
    for j in range(nblk):
        u = _gelu_tanh(proj[:, j * LANES:(j + 1) * LANES])
        v = _gelu_tanh(proj[:, w + j * LANES:w + (j + 1) * LANES])
        vn = _rms_half_blocks(v, sgu_ref[:, j * LANES:(j + 1) * LANES])
        low = _lane_is_low(vn.shape)
        vn_lo = jnp.where(low, vn, 0.0).astype(BF16)
        vn_hi = jnp.where(low, 0.0, vn).astype(BF16)
        wp = wpair_ref[j]
        bias = bias_ref[:, j * LANES:(j + 1) * LANES]
        for c in range(tm // GMLP_CHUNK):
            rows = slice(c * GMLP_CHUNK, (c + 1) * GMLP_CHUNK)
            stacked = jnp.concatenate([vn_lo[rows], vn_hi[rows]], axis=0)
            mixed = _dot(wp, stacked) + bias
            oa_ref[rows, j * LANES:(j + 1) * LANES] = (u[rows] * mixed).astype(BF16)
    cos, sup, sdn = cos_ref[...], sup_ref[...], sdn_ref[...]
    for j in range(nblk):
        qb = proj[:, 2 * w + j * LANES:2 * w + (j + 1) * LANES]
        kb = proj[:, 3 * w + j * LANES:3 * w + (j + 1) * LANES]
        qb = _rope_block(_rms_half_blocks(qb, qg_ref[...]), cos, sup, sdn, DIFF_D // 2)
        kb = _rope_block(_rms_half_blocks(kb, kg_ref[...]), cos, sup, sdn, DIFF_D // 2)
        q_ref[:, j * LANES:(j + 1) * LANES] = (qb * (LOG2E * DIFF_D ** -0.5)).astype(BF16)
        k_ref[:, j * LANES:(j + 1) * LANES] = kb.astype(BF16)
    v_ref[...] = proj[:, 4 * w:].astype(BF16)


def _even_proj_call(x, seq, gain, w_in, sgu_norm, w_s, b_s, q_norm, k_norm, tables):
    t = x.shape[0]
    tm = PROJ_TM
    w = GMLP_WIDTH
    row = lambda i: (i, 0)
    pos = lambda i: (i % (seq // tm), 0)
    wpair = jnp.concatenate([w_s[0::2], w_s[1::2]], axis=2).astype(BF16)
    bias = jnp.repeat(b_s.T, GMLP_GROUP_DIM, axis=1)
    two = lambda g: jnp.tile(g, 2).reshape(1, LANES)
    args = [x, gain.reshape(1, D_MODEL), w_in.astype(BF16), sgu_norm.reshape(1, w), wpair, bias,
            two(q_norm), two(k_norm), *tables]
    in_specs = [pl.BlockSpec((tm, D_MODEL), row), _const_spec((1, D_MODEL)),
                _const_spec(w_in.shape), _const_spec((1, w)), _const_spec(wpair.shape),
                _const_spec(bias.shape), _const_spec((1, LANES)), _const_spec((1, LANES)),
                pl.BlockSpec((tm, LANES), pos), pl.BlockSpec((tm, LANES), pos),
                pl.BlockSpec((tm, LANES), pos)]
    out = jax.ShapeDtypeStruct((t, w), BF16)
    return pl.pallas_call(
        _even_proj_kernel, grid=(t // tm,), in_specs=in_specs,
        out_specs=[pl.BlockSpec((tm, w), row)] * 4, out_shape=[out] * 4,
        compiler_params=_params(("parallel",)), name="even_proj",
    )(*args)


def _rms_padded_block(x, gain, width):
    ms = jnp.sum(x * x, axis=-1, keepdims=True) * (1.0 / width)
    return x * lax.rsqrt(ms + EPS) * gain


def _odd_proj_kernel(x_ref, g_ref, win_ref, cqg_ref, ckvg_ref, wuq_ref, wuk_ref, wuv_ref,
                     mqg_ref, mkg_ref, gqg_ref, gkg_ref,
                     mcos_ref, msup_ref, msdn_ref, acos_ref, asup_ref, asdn_ref,
                     qc_ref, kc_ref, vc_ref, qd_ref, kd_ref, vd_ref):
    xn = _rms_rows(x_ref[...], g_ref[...]).astype(BF16)
    proj = _dot(xn, win_ref[...])
    o1 = MLA_Q_RANK
    o2 = o1 + MLA_KV_RANK
    o3 = o2 + LANES
    o4 = o3 + GQA_Q_HEADS * GQA_DIM
    o5 = o4 + LANES
    mcos, msup, msdn = mcos_ref[...], msup_ref[...], msdn_ref[...]
    cq = _rms_rows(proj[:, :o1], cqg_ref[...]).astype(BF16)
    q_all = _dot(cq, wuq_ref[...])
    ckv = _rms_rows(proj[:, o1:o2], ckvg_ref[...]).astype(BF16)
    kn_all = _dot(ckv, wuk_ref[...])
    vc_ref[...] = _dot(ckv, wuv_ref[...]).astype(BF16)
    kpe = proj[:, o2:o3]
    for h in range(MLA_HEADS):
        blk = slice(h * LANES, (h + 1) * LANES)
        qb = _rms_padded_block(q_all[:, blk], mqg_ref[...], MLA_QK)
        qb = _rope_block(qb, mcos, msup, msdn, MLA_ROPE // 2)
        qc_ref[:, blk] = (qb * (LOG2E * MLA_QK ** -0.5)).astype(BF16)
        kb = _rms_padded_block(kn_all[:, blk] + kpe, mkg_ref[...], MLA_QK)
        kc_ref[:, blk] = _rope_block(kb, mcos, msup, msdn, MLA_ROPE // 2).astype(BF16)
    acos, asup, asdn = acos_ref[...], asup_ref[...], asdn_ref[...]
    for j in range(GQA_Q_HEADS * GQA_DIM // LANES):
        blk = slice(j * LANES, (j + 1) * LANES)
        qb = _rms_half_blocks(proj[:, o3 + j * LANES:o3 + (j + 1) * LANES], gqg_ref[...])
        qb = _rope_block(qb, acos, asup, asdn, GQA_DIM // 4)
        qd_ref[:, blk] = (qb * (LOG2E * GQA_DIM ** -0.5)).astype(BF16)
    kb = _rms_half_blocks(proj[:, o4:o5], gkg_ref[...])
    kd_ref[...] = _rope_block(kb, acos, asup, asdn, GQA_DIM // 4).astype(BF16)
    vd_ref[...] = proj[:, o5:].astype(BF16)


def _pad_heads(w, heads, width):
    r = w.shape[0]
    w = w.reshape(r, heads, width)
    return jnp.pad(w, ((0, 0), (0, 0), (0, LANES - width))).reshape(r, heads * LANES)


def _odd_proj_call(x, seq, gain, w_in, cq_norm, ckv_norm, w_uq, w_ukv, mq_norm, mk_norm,
                   gq_norm, gk_norm, mla_tables, axial_tables):
    t = x.shape[0]
    tm = PROJ_TM
    row = lambda i: (i, 0)
    pos = lambda i: (i % (seq // tm), 0)
    o1 = MLA_Q_RANK
    o2 = o1 + MLA_KV_RANK
    o3 = o2 + MLA_ROPE
    o4 = o3 + GQA_Q_HEADS * GQA_DIM
    o5 = o4 + GQA_KV_HEADS * GQA_DIM
    kpe_cols = jnp.pad(w_in[:, o2:o3], ((0, 0), (MLA_NOPE, LANES - MLA_QK)))
    gq_cols = w_in[:, o3:o4].reshape(D_MODEL, GQA_KV_HEADS, GQA_GROUP, GQA_DIM)
    gq_cols = gq_cols.transpose(0, 2, 1, 3).reshape(D_MODEL, GQA_Q_HEADS * GQA_DIM)
    win = jnp.concatenate([w_in[:, :o2], kpe_cols, gq_cols, w_in[:, o4:]], axis=1).astype(BF16)
    wuq = _pad_heads(w_uq, MLA_HEADS, MLA_QK).astype(BF16)
    w_ukv = w_ukv.reshape(MLA_KV_RANK, MLA_HEADS, MLA_NOPE + MLA_V)
    wuk = _pad_heads(w_ukv[:, :, :MLA_NOPE].reshape(MLA_KV_RANK, -1), MLA_HEADS, MLA_NOPE).astype(BF16)
    wuv = w_ukv[:, :, MLA_NOPE:].reshape(MLA_KV_RANK, MLA_HEADS * MLA_V).astype(BF16)
    pad_gain = lambda g: jnp.pad(g, (0, LANES - MLA_QK)).reshape(1, LANES)
    two = lambda g: jnp.tile(g, 2).reshape(1, LANES)
    args = [x, gain.reshape(1, D_MODEL), win, cq_norm.reshape(1, -1), ckv_norm.reshape(1, -1),
            wuq, wuk, wuv, pad_gain(mq_norm), pad_gain(mk_norm), two(gq_norm), two(gk_norm),
            *mla_tables, *axial_tables]
    in_specs = ([pl.BlockSpec((tm, D_MODEL), row)]
                + [_const_spec(a.shape) for a in args[1:12]]
                + [pl.BlockSpec((tm, LANES), pos)] * 6)
    widths = [MLA_HEADS * LANES, MLA_HEADS * LANES, MLA_HEADS * MLA_V,
              GQA_Q_HEADS * GQA_DIM, LANES, LANES]
    return pl.pallas_call(
        _odd_proj_kernel, grid=(t // tm,), in_specs=in_specs,
        out_specs=[pl.BlockSpec((tm, n), row) for n in widths],
        out_shape=[jax.ShapeDtypeStruct((t, n), BF16) for n in widths],
        compiler_params=_params(("parallel",)), name="odd_proj",
    )(*args)


def _attention_kernel(*refs, mode, lam_init):
    if mode == "diff":
        q_ref, k_ref, v_ref, lam_ref, sub_ref, o_ref = refs[:6]
    else:
        q_ref, k_ref, v_ref, o_ref = refs[:4]
    qt_ref, s_buf, p_buf, a_buf, m_ref, l_ref, acc_ref = refs[-7:]
    tq = q_ref.shape[0]
    seq = k_ref.shape[0]
    tk = ATT_TK
    cw = ATT_CW
    n = seq // tk
    q = q_ref[...].astype(F32)
    if mode == "mla":
        qt_ref[:, :tq] = q[:, :LANES].T.astype(BF16)
        qt_ref[:, tq:] = q[:, LANES:].T.astype(BF16)
    else:
        low = _lane_is_low(q.shape)
        qt_ref[:, :tq] = jnp.where(low, q, 0.0).T.astype(BF16)
        qt_ref[:, tq:] = jnp.where(low, 0.0, q).T.astype(BF16)

    chunks =[slice(c * cw, (c + 1) * cw) for c in range(2 * tq // cw)]

    def tile_rows(t):
        return pl.ds(pl.multiple_of(t * tk, tk), tk)

    def scores(kt, slot, cols):
        kc = kt
        if mode == "mla":
            kc = kt[:, :LANES] if cols.start < tq else kt[:, LANES:]
        s_buf[slot, :, cols] = _dot(kc, qt_ref[:, cols])

    def softmax(slot, cols):
        s = s_buf[slot, :, cols]
        m_old = m_ref[:, cols]
        m_new = jnp.maximum(m_old, jnp.max(s, axis=0, keepdims=True))
        alpha = jnp.exp2(m_old - m_new)
        p = jnp.exp2(s - m_new)
        l_ref[:, cols] = alpha * l_ref[:, cols] + jnp.sum(p, axis=0, keepdims=True)
        m_ref[:, cols] = m_new
        a_buf[slot, :, cols] = alpha
        p_buf[slot, :, cols] = p.astype(BF16)

    def values(vt, slot, cols):
        acc_ref[:, cols] = a_buf[slot, :, cols] * acc_ref[:, cols] + _dot(vt, p_buf[slot, :, cols])

    def half_step(t_next, t_prev, cur, nxt):
        kt = k_ref[tile_rows(t_next), :]
        vt = v_ref[tile_rows(t_prev), :].T
        for cols in chunks:
            scores(kt, nxt, cols)
            softmax(cur, cols)
            values(vt, nxt, cols)

    m_ref[...] = jnp.full(m_ref.shape, -jnp.inf, F32)
    l_ref[...] = jnp.zeros(l_ref.shape, F32)
    acc_ref[...] = jnp.zeros(acc_ref.shape, F32)
    p_buf[1] = jnp.zeros(p_buf.shape[1:], BF16)
    a_buf[1] = jnp.ones(a_buf.shape[1:], F32)
    kt0 = k_ref[tile_rows(0), :]
    for cols in chunks:
        scores(kt0, 0, cols)

    def body(i, carry):
        for u in range(ATT_PAIRS):
            t = 2 * (i * ATT_PAIRS + u)
            half_step(t + 1, jnp.maximum(t - 1, 0), 0, 1)
            half_step(jnp.minimum(t + 2, n - 1), t, 1, 0)
        return carry

    lax.fori_loop(0, n // (2 * ATT_PAIRS), body, 0)
    vt_last = v_ref[tile_rows(n - 1), :].T
    for cols in chunks:
        values(vt_last, 1, cols)

    o = acc_ref[...] / l_ref[...]
    if mode == "diff":
        lv = lam_ref[...]
        lam = (jnp.exp(jnp.sum(lv[0:1] * lv[1:2], axis=-1, keepdims=True))
               - jnp.exp(jnp.sum(lv[2:3] * lv[3:4], axis=-1, keepdims=True)) + lam_init)
        ot = (o[:, :tq] - lam * o[:, tq:]).T
        o_ref[...] = (_rms_rows(ot, sub_ref[...]) * (1.0 - lam_init)).astype(o_ref.dtype)
    else:
        half = LANES // 2
        ot = jnp.concatenate([o[:half, :tq], o[half:, tq:]], axis=0).T
        o_ref[...] = ot.astype(o_ref.dtype)


def _attention_call(q, k, v, mode, extra=(), lam_init=0.0):
    b, seq, _ = q.shape
    qw = 2 * LANES if mode == "mla" else LANES
    nblk = q.shape[2] // qw
    tq, tk = ATT_TQ, ATT_TK
    assert seq % (2 * ATT_PAIRS * tk) == 0 and seq % tq == 0 and tq % ATT_CW == 0
    shared = mode == "gqa"
    kv_idx = (lambda bi, j, i: (bi, 0, 0)) if shared else (lambda bi, j, i: (bi, 0, j))
    in_specs = [pl.BlockSpec((None, tq, qw), lambda bi, j, i: (bi, i, j)),
                pl.BlockSpec((None, seq, qw), kv_idx),
                pl.BlockSpec((None, seq, LANES), kv_idx)]
    in_specs += [_const_spec(e.shape) for e in extra]
    return pl.pallas_call(
        functools.partial(_attention_kernel, mode=mode, lam_init=lam_init),
        grid=(b, nblk, seq // tq), in_specs=in_specs,
        out_specs=pl.BlockSpec((None, tq, LANES), lambda bi, j, i: (bi, i, j)),
        out_shape=jax.ShapeDtypeStruct((b, seq, nblk * LANES), BF16),
        scratch_shapes=[pltpu.VMEM((LANES, 2 * tq), BF16),
                        pltpu.VMEM((2, tk, 2 * tq), F32),
                        pltpu.VMEM((2, tk, 2 * tq), BF16),
                        pltpu.VMEM((2, 1, 2 * tq), F32),
                        pltpu.VMEM((1, 2 * tq), F32), pltpu.VMEM((1, 2 * tq), F32),
                        pltpu.VMEM((LANES, 2 * tq), F32)],
        compiler_params=_params(("parallel", "parallel", "arbitrary")),
        name="attn_" + mode,
    )(q, k, v, *extra)


def _angles(pos, dim):
    inv = ROPE_THETA ** (-jnp.arange(0, dim, 2, dtype=F32) / dim)
    ang = pos.astype(F32)[:, None] * inv[None, :]
    return jnp.cos(ang), jnp.sin(ang)


def _rope_tables(seq):
    pos = jnp.arange(seq, dtype=jnp.int32)
    zeros = lambda n: jnp.zeros((seq, n), F32)
    ones = lambda n: jnp.ones((seq, n), F32)
    c, s = _angles(pos, DIFF_D)
    z = zeros(DIFF_D // 2)
    full = (jnp.tile(jnp.concatenate([c, c], 1), (1, 2)),
            jnp.tile(jnp.concatenate([-s, z], 1), (1, 2)),
            jnp.tile(jnp.concatenate([z, s], 1), (1, 2)))
    c, s = _angles(pos, MLA_ROPE)
    z = zeros(MLA_ROPE // 2)
    tail = zeros(LANES - MLA_QK)
    mla = (jnp.concatenate([ones(MLA_NOPE), c, c, tail], 1),
           jnp.concatenate([zeros(MLA_NOPE), -s, z, tail], 1),
           jnp.concatenate([zeros(MLA_NOPE), z, s, tail], 1))
    cr, sr = _angles(pos // GRID_W, GQA_DIM // 2)
    cc, sc = _angles(pos % GRID_W, GQA_DIM // 2)
    z = zeros(GQA_DIM // 4)
    axial = (jnp.tile(jnp.concatenate([cr, cr, cc, cc], 1), (1, 2)),
             jnp.tile(jnp.concatenate([-sr, z, -sc, z], 1), (1, 2)),
             jnp.tile(jnp.concatenate([z, sr, z, sc], 1), (1, 2)))
    return full, mla, axial


def kernel(x, ffn1_norm, ffn1_w_gu, ffn1_w_down, ffn2_norm, ffn2_w_gu, ffn2_w_down, ev_norm, ev_w_in, ev_sgu_norm, ev_w_s, ev_b_s, ev_q_norm, ev_k_norm, ev_lam_q1, ev_lam_k1, ev_lam_q2, ev_lam_k2, ev_sub_norm, ev_w_out, od_norm, od_w_in, od_cq_norm, od_ckv_norm, od_w_uq, od_w_ukv, od_mla_q_norm, od_mla_k_norm, od_gqa_q_norm, od_gqa_k_norm, od_w_out):
    b, seq, d = x.shape
    t = b * seq
    full_tab, mla_tab, axial_tab = _rope_tables(seq)
    x2 = x.reshape(t, d)

    x2 = _ffn_call(x2, ffn1_norm[0], ffn1_w_gu[0], ffn1_w_down[0])
    out_a, q, k, v = _even_proj_call(x2, seq, ev_norm[0], ev_w_in[0], ev_sgu_norm[0], ev_w_s[0],
                                     ev_b_s[0], ev_q_norm[0], ev_k_norm[0], full_tab)
    lam_init = 0.8 - 0.6 * float(np.exp(-0.3 * 0))
    lam_vecs = jnp.stack([ev_lam_q1[0], ev_lam_k1[0], ev_lam_q2[0], ev_lam_k2[0]])
    shp = (b, seq, GMLP_WIDTH)
    out_b = _attention_call(q.reshape(shp), k.reshape(shp), v.reshape(shp), "diff",
                            extra=(lam_vecs, ev_sub_norm[0].reshape(1, DIFF_V)), lam_init=lam_init)
    w_out = ev_w_out[0].astype(BF16)
    x2 = _ffn_call(x2, ffn2_norm[0], ffn2_w_gu[0], ffn2_w_down[0],
                   mix=(out_a, out_b.reshape(t, -1), w_out[:GMLP_WIDTH], w_out[GMLP_WIDTH:]))

    x2 = _ffn_call(x2, ffn1_norm[1], ffn1_w_gu[1], ffn1_w_down[1])
    qc, kc, vc, qd, kd, vd = _odd_proj_call(
        x2, seq, od_norm[0], od_w_in[0], od_cq_norm[0], od_ckv_norm[0], od_w_uq[0], od_w_ukv[0],
        od_mla_q_norm[0], od_mla_k_norm[0], od_gqa_q_norm[0], od_gqa_k_norm[0], mla_tab, axial_tab)
    r3 = lambda a: a.reshape(b, seq, a.shape[1])
    out_c = _attention_call(r3(qc), r3(kc), r3(vc), "mla")
    out_d = _attention_call(r3(qd), r3(kd), r3(vd), "gqa")
    w_out = od_w_out[0].astype(BF16)
    n_c = MLA_HEADS * MLA_V
    w_d = w_out[n_c:].reshape(GQA_KV_HEADS, GQA_GROUP, GQA_DIM, d).transpose(1, 0, 2, 3)
    x2 = _ffn_call(x2, ffn2_norm[1], ffn2_w_gu[1], ffn2_w_down[1],
                   mix=(out_c.reshape(t, -1), out_d.reshape(t, -1), w_out[:n_c],
                        w_d.reshape(GQA_Q_HEADS * GQA_DIM, d)))
    return x2.reshape(b, seq, d)
```

```python
import functools
import math

import numpy as np
import jax
import jax.numpy as jnp
from jax import lax
from jax.experimental import pallas as pl
from jax.experimental.pallas import tpu as pltpu

D_MODEL = 1024
D_FF = 2816
ROPE_THETA = 10000.0
GRID_W = 64
EPS = 1e-6
GMLP_GROUPS = 8
GMLP_GROUP_DIM = 64
GMLP_CHUNK = 128
GMLP_WIDTH = GMLP_GROUPS * GMLP_GROUP_DIM
DIFF_HEADS = 4
DIFF_D = 64
DIFF_V = 128
MLA_HEADS = 8
MLA_Q_RANK = 256
MLA_KV_RANK = 128
MLA_NOPE = 64
MLA_ROPE = 32
MLA_V = 64
MLA_QK = MLA_NOPE + MLA_ROPE
GQA_Q_HEADS = 8
GQA_KV_HEADS = 2
GQA_GROUP = GQA_Q_HEADS // GQA_KV_HEADS
GQA_DIM = 64

LANES = 128
VMEM_LIMIT_BYTES = 56 * 1024 * 1024

FFN_TM = 256
PROJ_TM = 256
ATT_TQ = 512
ATT_TK = 256
ATT_CW = 256
ATT_PAIRS = 8
SUM_ROWS = 16
LOG2E = math.log2(math.e)

BF16 = jnp.bfloat16
F32 = jnp.float32


def _params(semantics):
    return pltpu.CompilerParams(dimension_semantics=semantics,
                                vmem_limit_bytes=VMEM_LIMIT_BYTES)


def _const_spec(shape):
    nd = len(shape)
    return pl.BlockSpec(shape, lambda *_: (0,) * nd, pipeline_mode=pl.Buffered(1))


def _rms_rows(x, gain):
    ms = jnp.mean(x * x, axis=-1, keepdims=True)
    return x * lax.rsqrt(ms + EPS) * gain


def _dot(a, b):
    return jnp.dot(a, b, preferred_element_type=F32)


def _lane_is_low(shape):
    return lax.broadcasted_iota(jnp.int32, shape, len(shape) - 1) < (LANES // 2)


def _rms_half_blocks(x, gain):
    low = _lane_is_low(x.shape)
    x2 = x * x
    x2_lo = jnp.where(low, x2, 0.0)
    x2_hi = x2 - x2_lo
    ms_lo = jnp.sum(x2_lo, axis=-1, keepdims=True) * (2.0 / LANES)
    ms_hi = jnp.sum(x2_hi, axis=-1, keepdims=True) * (2.0 / LANES)
    r = jnp.where(low, lax.rsqrt(ms_lo + EPS), lax.rsqrt(ms_hi + EPS))
    return x * r * gain


def _rope_block(x, cos, sin_up, sin_dn, shift):
    up = pltpu.roll(x, LANES - shift, axis=1)
    dn = pltpu.roll(x, shift, axis=1)
    return x * cos + up * sin_up + dn * sin_dn


def _ffn_body(x, g_ref, wgu_ref, wd_ref, o_ref):
    xn = _rms_rows(x, g_ref[...]).astype(BF16)
    h = _dot(xn, wgu_ref[...])
    gate = h[:, :D_FF]
    up = h[:, D_FF:]
    act = (gate / (1.0 + jnp.exp(-gate)) * up).astype(BF16)
    o_ref[...] = x + 0.5 * _dot(act, wd_ref[...])


def _ffn_kernel(x_ref, g_ref, wgu_ref, wd_ref, o_ref):
    _ffn_body(x_ref[...], g_ref, wgu_ref, wd_ref, o_ref)


def _mix_ffn_kernel(x_ref, a_ref, b_ref, wa_ref, wb_ref, g_ref, wgu_ref, wd_ref, o_ref):
    x = x_ref[...] + _dot(a_ref[...], wa_ref[...]) + _dot(b_ref[...], wb_ref[...])
    _ffn_body(x, g_ref, wgu_ref, wd_ref, o_ref)


def _ffn_call(x, gain, w_gu, w_down, mix=None):
    t = x.shape[0]
    tm = FFN_TM
    row = lambda i: (i, 0)
    x_spec = pl.BlockSpec((tm, D_MODEL), row)
    w_specs = [_const_spec((1, D_MODEL)), _const_spec((D_MODEL, 2 * D_FF)),
               _const_spec((D_FF, D_MODEL))]
    w_args = [gain.reshape(1, D_MODEL), w_gu.astype(BF16), w_down.astype(BF16)]
    if mix is None:
        kern, in_specs, args = _ffn_kernel, [x_spec] + w_specs, [x] + w_args
    else:
        a, b, wa, wb = mix
        kern = _mix_ffn_kernel
        in_specs = [x_spec, pl.BlockSpec((tm, a.shape[1]), row), pl.BlockSpec((tm, b.shape[1]), row),
                    _const_spec(wa.shape), _const_spec(wb.shape)] + w_specs
        args = [x, a, b, wa, wb] + w_args
    return pl.pallas_call(
        kern, grid=(t // tm,), in_specs=in_specs,
        out_specs=pl.BlockSpec((tm, D_MODEL), row),
        out_shape=jax.ShapeDtypeStruct((t, D_MODEL), F32),
        compiler_params=_params(("parallel",)),
        name="ffn" if mix is None else "mix_ffn",
    )(*args)


def _gelu_tanh(x):
    c = math.sqrt(2.0 / math.pi)
    return 0.5 * x * (1.0 + jnp.tanh(c * (x + 0.044715 * (x * x * x))))


def _even_proj_kernel(x_ref, g_ref, win_ref, sgu_ref, wpair_ref, bias_ref, qg_ref, kg_ref,
                      cos_ref, sup_ref, sdn_ref, oa_ref, q_ref, k_ref, v_ref):
    xn = _rms_rows(x_ref[...], g_ref[...]).astype(BF16)
    proj = _dot(xn, win_ref[...])
    w = GMLP_WIDTH
    nblk = w // LANES
    tm = proj.shape[0]
    for j in range(nblk):
        u = _gelu_tanh(proj[:, j * LANES:(j + 1) * LANES])
        v = _gelu_tanh(proj[:, w + j * LANES:w + (j + 1) * LANES])
        vn = _rms_half_blocks(v, sgu_ref[:, j * LANES:(j + 1) * LANES])
        low = _lane_is_low(vn.shape)
        vn_lo = jnp.where(low, vn, 0.0).astype(BF16)
        vn_hi = jnp.where(low, 0.0, vn).astype(BF16)
        wp = wpair_ref[j]
        bias = bias_ref[:, j * LANES:(j + 1) * LANES]
        for c in range(tm // GMLP_CHUNK):
            rows = slice(c * GMLP_CHUNK, (c + 1) * GMLP_CHUNK)
            stacked = jnp.concatenate([vn_lo[rows], vn_hi[rows]], axis=0)
            mixed = _dot(wp, stacked) + bias
            oa_ref[rows, j * LANES:(j + 1) * LANES] = (u[rows] * mixed).astype(BF16)
    cos, sup, sdn = cos_ref[...], sup_ref[...], sdn_ref[...]
    for j in range(nblk):
        qb = proj[:, 2 * w + j * LANES:2 * w + (j + 1) * LANES]
        kb = proj[:, 3 * w + j * LANES:3 * w + (j + 1) * LANES]
        qb = _rope_block(_rms_half_blocks(qb, qg_ref[...]), cos, sup, sdn, DIFF_D // 2)
        kb = _rope_block(_rms_half_blocks(kb, kg_ref[...]), cos, sup, sdn, DIFF_D // 2)
        q_ref[:, j * LANES:(j + 1) * LANES] = (qb * (LOG2E * DIFF_D ** -0.5)).astype(BF16)
        k_ref[:, j * LANES:(j + 1) * LANES] = kb.astype(BF16)
    v_ref[...] = proj[:, 4 * w:].astype(BF16)


def _even_proj_call(x, seq, gain, w_in, sgu_norm, w_s, b_s, q_norm, k_norm, tables):
    t = x.shape[0]
    tm = PROJ_TM
    w = GMLP_WIDTH
    row = lambda i: (i, 0)
    pos = lambda i: (i % (seq // tm), 0)
    wpair = jnp.concatenate([w_s[0::2], w_s[1::2]], axis=2).astype(BF16)
    bias = jnp.repeat(b_s.T, GMLP_GROUP_DIM, axis=1)
    two = lambda g: jnp.tile(g, 2).reshape(1, LANES)
    args = [x, gain.reshape(1, D_MODEL), w_in.astype(BF16), sgu_norm.reshape(1, w), wpair, bias,
            two(q_norm), two(k_norm), *tables]
    in_specs = [pl.BlockSpec((tm, D_MODEL), row), _const_spec((1, D_MODEL)),
                _const_spec(w_in.shape), _const_spec((1, w)), _const_spec(wpair.shape),
                _const_spec(bias.shape), _const_spec((1, LANES)), _const_spec((1, LANES)),
                pl.BlockSpec((tm, LANES), pos), pl.BlockSpec((tm, LANES), pos),
                pl.BlockSpec((tm, LANES), pos)]
    out = jax.ShapeDtypeStruct((t, w), BF16)
    return pl.pallas_call(
        _even_proj_kernel, grid=(t // tm,), in_specs=in_specs,
        out_specs=[pl.BlockSpec((tm, w), row)] * 4, out_shape=[out] * 4,
        compiler_params=_params(("parallel",)), name="even_proj",
    )(*args)


def _rms_padded_block(x, gain, width):
    ms = jnp.sum(x * x, axis=-1, keepdims=True) * (1.0 / width)
    return x * lax.rsqrt(ms + EPS) * gain


def _odd_proj_kernel(x_ref, g_ref, win_ref, cqg_ref, ckvg_ref, wuq_ref, wuk_ref, wuv_ref,
                     mqg_ref, mkg_ref, gqg_ref, gkg_ref,
                     mcos_ref, msup_ref, msdn_ref, acos_ref, asup_ref, asdn_ref,
                     qc_ref, kc_ref, vc_ref, qd_ref, kd_ref, vd_ref):
    xn = _rms_rows(x_ref[...], g_ref[...]).astype(BF16)
    proj = _dot(xn, win_ref[...])
    o1 = MLA_Q_RANK
    o2 = o1 + MLA_KV_RANK
    o3 = o2 + LANES
    o4 = o3 + GQA_Q_HEADS * GQA_DIM
    o5 = o4 + LANES
    mcos, msup, msdn = mcos_ref[...], msup_ref[...], msdn_ref[...]
    cq = _rms_rows(proj[:, :o1], cqg_ref[...]).astype(BF16)
    q_all = _dot(cq, wuq_ref[...])
    ckv = _rms_rows(proj[:, o1:o2], ckvg_ref[...]).astype(BF16)
    kn_all = _dot(ckv, wuk_ref[...])
    vc_ref[...] = _dot(ckv, wuv_ref[...]).astype(BF16)
    kpe = proj[:, o2:o3]
    for h in range(MLA_HEADS):
        blk = slice(h * LANES, (h + 1) * LANES)
        qb = _rms_padded_block(q_all[:, blk], mqg_ref[...], MLA_QK)
        qb = _rope_block(qb, mcos, msup, msdn, MLA_ROPE // 2)
        qc_ref[:, blk] = (qb * (LOG2E * MLA_QK ** -0.5)).astype(BF16)
        kb = _rms_padded_block(kn_all[:, blk] + kpe, mkg_ref[...], MLA_QK)
        kc_ref[:, blk] = _rope_block(kb, mcos, msup, msdn, MLA_ROPE // 2).astype(BF16)
    acos, asup, asdn = acos_ref[...], asup_ref[...], asdn_ref[...]
    for j in range(GQA_Q_HEADS * GQA_DIM // LANES):
        blk = slice(j * LANES, (j + 1) * LANES)
        qb = _rms_half_blocks(proj[:, o3 + j * LANES:o3 + (j + 1) * LANES], gqg_ref[...])
        qb = _rope_block(qb, acos, asup, asdn, GQA_DIM // 4)
        qd_ref[:, blk] = (qb * (LOG2E * GQA_DIM ** -0.5)).astype(BF16)
    kb = _rms_half_blocks(proj[:, o4:o5], gkg_ref[...])
    kd_ref[...] = _rope_block(kb, acos, asup, asdn, GQA_DIM // 4).astype(BF16)
    vd_ref[...] = proj[:, o5:].astype(BF16)


def _pad_heads(w, heads, width):
    r = w.shape[0]
    w = w.reshape(r, heads, width)
    return jnp.pad(w, ((0, 0), (0, 0), (0, LANES - width))).reshape(r, heads * LANES)


def _odd_proj_call(x, seq, gain, w_in, cq_norm, ckv_norm, w_uq, w_ukv, mq_norm, mk_norm,
                   gq_norm, gk_norm, mla_tables, axial_tables):
    t = x.shape[0]
    tm = PROJ_TM
    row = lambda i: (i, 0)
    pos = lambda i: (i % (seq // tm), 0)
    o1 = MLA_Q_RANK
    o2 = o1 + MLA_KV_RANK
    o3 = o2 + MLA_ROPE
    o4 = o3 + GQA_Q_HEADS * GQA_DIM
    o5 = o4 + GQA_KV_HEADS * GQA_DIM
    kpe_cols = jnp.pad(w_in[:, o2:o3], ((0, 0), (MLA_NOPE, LANES - MLA_QK)))
    gq_cols = w_in[:, o3:o4].reshape(D_MODEL, GQA_KV_HEADS, GQA_GROUP, GQA_DIM)
    gq_cols = gq_cols.transpose(0, 2, 1, 3).reshape(D_MODEL, GQA_Q_HEADS * GQA_DIM)
    win = jnp.concatenate([w_in[:, :o2], kpe_cols, gq_cols, w_in[:, o4:]], axis=1).astype(BF16)
    wuq = _pad_heads(w_uq, MLA_HEADS, MLA_QK).astype(BF16)
    w_ukv = w_ukv.reshape(MLA_KV_RANK, MLA_HEADS, MLA_NOPE + MLA_V)
    wuk = _pad_heads(w_ukv[:, :, :MLA_NOPE].reshape(MLA_KV_RANK, -1), MLA_HEADS, MLA_NOPE).astype(BF16)
    wuv = w_ukv[:, :, MLA_NOPE:].reshape(MLA_KV_RANK, MLA_HEADS * MLA_V).astype(BF16)
    pad_gain = lambda g: jnp.pad(g, (0, LANES - MLA_QK)).reshape(1, LANES)
    two = lambda g: jnp.tile(g, 2).reshape(1, LANES)
    args = [x, gain.reshape(1, D_MODEL), win, cq_norm.reshape(1, -1), ckv_norm.reshape(1, -1),
            wuq, wuk, wuv, pad_gain(mq_norm), pad_gain(mk_norm), two(gq_norm), two(gk_norm),
            *mla_tables, *axial_tables]
    in_specs = ([pl.BlockSpec((tm, D_MODEL), row)]
                + [_const_spec(a.shape) for a in args[1:12]]
                + [pl.BlockSpec((tm, LANES), pos)] * 6)
    widths = [MLA_HEADS * LANES, MLA_HEADS * LANES, MLA_HEADS * MLA_V,
              GQA_Q_HEADS * GQA_DIM, LANES, LANES]
    return pl.pallas_call(
        _odd_proj_kernel, grid=(t // tm,), in_specs=in_specs,
        out_specs=[pl.BlockSpec((tm, n), row) for n in widths],
        out_shape=[jax.ShapeDtypeStruct((t, n), BF16) for n in widths],
        compiler_params=_params(("parallel",)), name="odd_proj",
    )(*args)


def _attention_kernel(*refs, mode, lam_init):
    if mode == "diff":
        q_ref, k_ref, v_ref, lam_ref, sub_ref, o_ref = refs[:6]
    else:
        q_ref, k_ref, v_ref, o_ref = refs[:4]
    qt_ref, s_buf, p_buf, a_buf, m_ref, acc_ref = refs[-6:]
    tq = q_ref.shape[0]
    seq = k_ref.shape[0]
    tk = ATT_TK
    cw = ATT_CW
    n = seq // tk
    q = q_ref[...].astype(F32)
    if mode == "mla":
        qt_ref[:, :tq] = q[:, :LANES].T.astype(BF16)
        qt_ref[:, tq:] = q[:, LANES:].T.astype(BF16)
    else:
        low = _lane_is_low(q.shape)
        qt_ref[:, :tq] = jnp.where(low, q, 0.0).T.astype(BF16)
        qt_ref[:, tq:] = jnp.where(low, 0.0, q).T.astype(BF16)

    chunks =[slice(c * cw, (c + 1) * cw) for c in range(2 * tq // cw)]

    def tile_rows(t):
        return pl.ds(pl.multiple_of(t * tk, tk), tk)

    def scores(kt, slot, cols):
        kc = kt
        if mode == "mla":
            kc = kt[:, :LANES] if cols.start < tq else kt[:, LANES:]
        s_buf[slot, :, cols] = _dot(kc, qt_ref[:, cols])

    def softmax(slot, cols):
        s = s_buf[slot, :, cols]
        m_old = m_ref[:, cols]
        m_new = jnp.maximum(m_old, jnp.max(s, axis=0, keepdims=True))
        alpha = jnp.exp2(m_old - m_new)
        p = jnp.exp2(s - m_new)
        m_ref[:, cols] = m_new
        a_buf[slot, :, cols] = alpha
        p_buf[slot, :, cols] = p.astype(BF16)

    def value_rows(t):
        ones = jnp.ones((SUM_ROWS, tk), BF16)
        return jnp.concatenate([v_ref[tile_rows(t), :].T, ones], axis=0)

    def values(vt, slot, cols):
        acc_ref[:, cols] = a_buf[slot, :, cols] * acc_ref[:, cols] + _dot(vt, p_buf[slot, :, cols])

    def half_step(t_next, t_prev, cur, nxt):
        kt = k_ref[tile_rows(t_next), :]
        vt = value_rows(t_prev)
        for cols in chunks:
            scores(kt, nxt, cols)
            softmax(cur, cols)
            values(vt, nxt, cols)

    m_ref[...] = jnp.full(m_ref.shape, -jnp.inf, F32)
    acc_ref[...] = jnp.zeros(acc_ref.shape, F32)
    p_buf[1] = jnp.zeros(p_buf.shape[1:], BF16)
    a_buf[1] = jnp.ones(a_buf.shape[1:], F32)
    kt0 = k_ref[tile_rows(0), :]
    for cols in chunks:
        scores(kt0, 0, cols)

    def body(i, carry):
        for u in range(ATT_PAIRS):
            t = 2 * (i * ATT_PAIRS + u)
            half_step(t + 1, jnp.maximum(t - 1, 0), 0, 1)
            half_step(jnp.minimum(t + 2, n - 1), t, 1, 0)
        return carry

    lax.fori_loop(0, n // (2 * ATT_PAIRS), body, 0)
    vt_last = value_rows(n - 1)
    for cols in chunks:
        values(vt_last, 1, cols)

    o = acc_ref[:LANES, :] / acc_ref[LANES:LANES + 1, :]
    if mode == "diff":
        lv = lam_ref[...]
        lam = (jnp.exp(jnp.sum(lv[0:1] * lv[1:2], axis=-1, keepdims=True))
               - jnp.exp(jnp.sum(lv[2:3] * lv[3:4], axis=-1, keepdims=True)) + lam_init)
        ot = (o[:, :tq] - lam * o[:, tq:]).T
        o_ref[...] = (_rms_rows(ot, sub_ref[...]) * (1.0 - lam_init)).astype(o_ref.dtype)
    else:
        half = LANES // 2
        ot = jnp.concatenate([o[:half, :tq], o[half:, tq:]], axis=0).T
        o_ref[...] = ot.astype(o_ref.dtype)


def _attention_call(q, k, v, mode, extra=(), lam_init=0.0):
    b, seq, _ = q.shape
    qw = 2 * LANES if mode == "mla" else LANES
    nblk = q.shape[2] // qw
    tq, tk = ATT_TQ, ATT_TK
    assert seq % (2 * ATT_PAIRS * tk) == 0 and seq % tq == 0 and tq % ATT_CW == 0
    shared = mode == "gqa"
    kv_idx = (lambda bi, j, i: (bi, 0, 0)) if shared else (lambda bi, j, i: (bi, 0, j))
    in_specs = [pl.BlockSpec((None, tq, qw), lambda bi, j, i: (bi, i, j)),
                pl.BlockSpec((None, seq, qw), kv_idx),
                pl.BlockSpec((None, seq, LANES), kv_idx)]
    in_specs += [_const_spec(e.shape) for e in extra]
    return pl.pallas_call(
        functools.partial(_attention_kernel, mode=mode, lam_init=lam_init),
        grid=(b, nblk, seq // tq), in_specs=in_specs,
        out_specs=pl.BlockSpec((None, tq, LANES), lambda bi, j, i: (bi, i, j)),
        out_shape=jax.ShapeDtypeStruct((b, seq, nblk * LANES), BF16),
        scratch_shapes=[pltpu.VMEM((LANES, 2 * tq), BF16),
                        pltpu.VMEM((2, tk, 2 * tq), F32),
                        pltpu.VMEM((2, tk, 2 * tq), BF16),
                        pltpu.VMEM((2, 1, 2 * tq), F32),
                        pltpu.VMEM((1, 2 * tq), F32),
                        pltpu.VMEM((LANES + SUM_ROWS, 2 * tq), F32)],
        compiler_params=_params(("parallel", "parallel", "arbitrary")),
        name="attn_" + mode,
    )(q, k, v, *extra)


def _angles(pos, dim):
    inv = ROPE_THETA ** (-jnp.arange(0, dim, 2, dtype=F32) / dim)
    ang = pos.astype(F32)[:, None] * inv[None, :]
    return jnp.cos(ang), jnp.sin(ang)


def _rope_tables(seq):
    pos = jnp.arange(seq, dtype=jnp.int32)
    zeros = lambda n: jnp.zeros((seq, n), F32)
    ones = lambda n: jnp.ones((seq, n), F32)
    c, s = _angles(pos, DIFF_D)
    z = zeros(DIFF_D // 2)
    full = (jnp.tile(jnp.concatenate([c, c], 1), (1, 2)),
            jnp.tile(jnp.concatenate([-s, z], 1), (1, 2)),
            jnp.tile(jnp.concatenate([z, s], 1), (1, 2)))
    c, s = _angles(pos, MLA_ROPE)
    z = zeros(MLA_ROPE // 2)
    tail = zeros(LANES - MLA_QK)
    mla = (jnp.concatenate([ones(MLA_NOPE), c, c, tail], 1),
           jnp.concatenate([zeros(MLA_NOPE), -s, z, tail], 1),
           jnp.concatenate([zeros(MLA_NOPE), z, s, tail], 1))
    cr, sr = _angles(pos // GRID_W, GQA_DIM // 2)
    cc, sc = _angles(pos % GRID_W, GQA_DIM // 2)
    z = zeros(GQA_DIM // 4)
    axial = (jnp.tile(jnp.concatenate([cr, cr, cc, cc], 1), (1, 2)),
             jnp.tile(jnp.concatenate([-sr, z, -sc, z], 1), (1, 2)),
             jnp.tile(jnp.concatenate([z, sr, z, sc], 1), (1, 2)))
    return full, mla, axial


def kernel(x, ffn1_norm, ffn1_w_gu, ffn1_w_down, ffn2_norm, ffn2_w_gu, ffn2_w_down, ev_norm, ev_w_in, ev_sgu_norm, ev_w_s, ev_b_s, ev_q_norm, ev_k_norm, ev_lam_q1, ev_lam_k1, ev_lam_q2, ev_lam_k2, ev_sub_norm, ev_w_out, od_norm, od_w_in, od_cq_norm, od_ckv_norm, od_w_uq, od_w_ukv, od_mla_q_norm, od_mla_k_norm, od_gqa_q_norm, od_gqa_k_norm, od_w_out):
    b, seq, d = x.shape
    t = b * seq
    full_tab, mla_tab, axial_tab = _rope_tables(seq)
    x2 = x.reshape(t, d)

    x2 = _ffn_call(x2, ffn1_norm[0], ffn1_w_gu[0], ffn1_w_down[0])
    out_a, q, k, v = _even_proj_call(x2, seq, ev_norm[0], ev_w_in[0], ev_sgu_norm[0], ev_w_s[0],
                                     ev_b_s[0], ev_q_norm[0], ev_k_norm[0], full_tab)
    lam_init = 0.8 - 0.6 * float(np.exp(-0.3 * 0))
    lam_vecs = jnp.stack([ev_lam_q1[0], ev_lam_k1[0], ev_lam_q2[0], ev_lam_k2[0]])
    shp = (b, seq, GMLP_WIDTH)
    out_b = _attention_call(q.reshape(shp), k.reshape(shp), v.reshape(shp), "diff",
                            extra=(lam_vecs, ev_sub_norm[0].reshape(1, DIFF_V)), lam_init=lam_init)
    w_out = ev_w_out[0].astype(BF16)
    x2 = _ffn_call(x2, ffn2_norm[0], ffn2_w_gu[0], ffn2_w_down[0],
                   mix=(out_a, out_b.reshape(t, -1), w_out[:GMLP_WIDTH], w_out[GMLP_WIDTH:]))

    x2 = _ffn_call(x2, ffn1_norm[1], ffn1_w_gu[1], ffn1_w_down[1])
    qc, kc, vc, qd, kd, vd = _odd_proj_call(
        x2, seq, od_norm[0], od_w_in[0], od_cq_norm[0], od_ckv_norm[0], od_w_uq[0], od_w_ukv[0],
        od_mla_q_norm[0], od_mla_k_norm[0], od_gqa_q_norm[0], od_gqa_k_norm[0], mla_tab, axial_tab)
    r3 = lambda a: a.reshape(b, seq, a.shape[1])
    out_c = _attention_call(r3(qc), r3(kc), r3(vc), "mla")
    out_d = _attention_call(r3(qd), r3(kd), r3(vd), "gqa")
    w_out = od_w_out[0].astype(BF16)
    n_c = MLA_HEADS * MLA_V
    w_d = w_out[n_c:].reshape(GQA_KV_HEADS, GQA_GROUP, GQA_DIM, d).transpose(1, 0, 2, 3)
    x2 = _ffn_call(x2, ffn2_norm[1], ffn2_w_gu[1], ffn2_w_down[1],
                   mix=(out_c.reshape(t, -1), out_d.reshape(t, -1), w_out[:n_c],
                        w_d.reshape(GQA_Q_HEADS * GQA_DIM, d)))
    return x2.reshape(b, seq, d)
```

```python
import functools
import math

import numpy as np
import jax
import jax.numpy as jnp
from jax import lax
from jax.experimental import pallas as pl
from jax.experimental.pallas import tpu as pltpu

D_MODEL = 1024
D_FF = 2816
ROPE_THETA = 10000.0
GRID_W = 64
EPS = 1e-6
GMLP_GROUPS = 8
GMLP_GROUP_DIM = 64
GMLP_CHUNK = 128
GMLP_WIDTH = GMLP_GROUPS * GMLP_GROUP_DIM
DIFF_HEADS = 4
DIFF_D = 64
DIFF_V = 128
MLA_HEADS = 8
MLA_Q_RANK = 256
MLA_KV_RANK = 128
MLA_NOPE = 64
MLA_ROPE = 32
MLA_V = 64
MLA_QK = MLA_NOPE + MLA_ROPE
GQA_Q_HEADS = 8
GQA_KV_HEADS = 2
GQA_GROUP = GQA_Q_HEADS // GQA_KV_HEADS
GQA_DIM = 64

LANES = 128
VMEM_LIMIT_BYTES = 56 * 1024 * 1024

FFN_TM = 512
PROJ_TM = 256
ATT_TQ = 512
ATT_TK = 256
ATT_CW = 256
ATT_PAIRS = 8
SUM_ROWS = 16
LOG2E = math.log2(math.e)

BF16 = jnp.bfloat16
F32 = jnp.float32


def _params(semantics):
    return pltpu.CompilerParams(dimension_semantics=semantics,
                                vmem_limit_bytes=VMEM_LIMIT_BYTES)


def _const_spec(shape):
    nd = len(shape)
    return pl.BlockSpec(shape, lambda *_: (0,) * nd, pipeline_mode=pl.Buffered(1))


def _rms_rows(x, gain):
    ms = jnp.mean(x * x, axis=-1, keepdims=True)
    return x * lax.rsqrt(ms + EPS) * gain


def _dot(a, b):
    return jnp.dot(a, b, preferred_element_type=F32)


def _lane_is_low(shape):
    return lax.broadcasted_iota(jnp.int32, shape, len(shape) - 1) < (LANES // 2)


def _inv_rms_half_blocks(x):
    low = _lane_is_low(x.shape)
    x2 = x * x
    x2_lo = jnp.where(low, x2, 0.0)
    x2_hi = x2 - x2_lo
    ms_lo = jnp.sum(x2_lo, axis=-1, keepdims=True) * (2.0 / LANES)
    ms_hi = jnp.sum(x2_hi, axis=-1, keepdims=True) * (2.0 / LANES)
    return jnp.where(low, lax.rsqrt(ms_lo + EPS), lax.rsqrt(ms_hi + EPS))


def _inv_rms_padded_block(x, width):
    ms = jnp.sum(x * x, axis=-1, keepdims=True) * (1.0 / width)
    return lax.rsqrt(ms + EPS)


def _swap_pairs(a, group):
    shape = a.shape
    a = a.reshape(shape[:-1] + (shape[-1] // group, 2, group // 2))
    return jnp.flip(a, axis=-2).reshape(shape)


def _ffn_body(x, g_ref, wgu_ref, wd_ref, o_ref):
    xn = _rms_rows(x, g_ref[...]).astype(BF16)
    h = _dot(xn, wgu_ref[...])
    gate = h[:, :D_FF]
    up = h[:, D_FF:]
    act = (gate / (1.0 + jnp.exp(-gate)) * up).astype(BF16)
    o_ref[...] = x + 0.5 * _dot(act, wd_ref[...])


def _ffn_kernel(x_ref, g_ref, wgu_ref, wd_ref, o_ref):
    _ffn_body(x_ref[...], g_ref, wgu_ref, wd_ref, o_ref)


def _mix_ffn_kernel(x_ref, a_ref, b_ref, wa_ref, wb_ref, g_ref, wgu_ref, wd_ref, o_ref):
    x = x_ref[...] + _dot(a_ref[...], wa_ref[...]) + _dot(b_ref[...], wb_ref[...])
    _ffn_body(x, g_ref, wgu_ref, wd_ref, o_ref)


def _layer_spec(shape, layer):
    nd = len(shape) - 1
    return pl.BlockSpec((None,) + tuple(shape[1:]), lambda *_: (layer,) + (0,) * nd,
                        pipeline_mode=pl.Buffered(1))


def _ffn_call(x, layer, gains, w_gu, w_down, mix=None):
    t = x.shape[0]
    tm = FFN_TM
    row = lambda i: (i, 0)
    x_spec = pl.BlockSpec((tm, D_MODEL), row)
    w_args = [gains, w_gu, w_down]
    w_specs = [_layer_spec(a.shape, layer) for a in w_args]
    if mix is None:
        kern, in_specs, args = _ffn_kernel, [x_spec] + w_specs, [x] + w_args
    else:
        a, b, wa, wb = mix
        kern = _mix_ffn_kernel
        in_specs = [x_spec, pl.BlockSpec((tm, a.shape[1]), row), pl.BlockSpec((tm, b.shape[1]), row),
                    _const_spec(wa.shape), _const_spec(wb.shape)] + w_specs
        args = [x, a, b, wa, wb] + w_args
    return pl.pallas_call(
        kern, grid=(t // tm,), in_specs=in_specs,
        out_specs=pl.BlockSpec((tm, D_MODEL), row),
        out_shape=jax.ShapeDtypeStruct((t, D_MODEL), F32),
        compiler_params=_params(("parallel",)),
        name="ffn" if mix is None else "mix_ffn",
    )(*args)


def _gelu_tanh(x):
    c = math.sqrt(2.0 / math.pi)
    return 0.5 * x * (1.0 + jnp.tanh(c * (x + 0.044715 * (x * x * x))))


def _even_proj_kernel(x_ref, g_ref, win_ref, sgu_ref, wpair_ref, bias_ref, qg_ref, kg_ref,
                      cos_ref, sin_ref, oa_ref, q_ref, k_ref, v_ref):
    xn = _rms_rows(x_ref[...], g_ref[...]).astype(BF16)
    proj = _dot(xn, win_ref[...])
    w = GMLP_WIDTH
    nblk = w // LANES
    tm = proj.shape[0]
    for j in range(nblk):
        u = _gelu_tanh(proj[:, j * LANES:(j + 1) * LANES])
        v = _gelu_tanh(proj[:, w + j * LANES:w + (j + 1) * LANES])
        vn = v * _inv_rms_half_blocks(v) * sgu_ref[:, j * LANES:(j + 1) * LANES]
        low = _lane_is_low(vn.shape)
        vn_lo = jnp.where(low, vn, 0.0).astype(BF16)
        vn_hi = jnp.where(low, 0.0, vn).astype(BF16)
        wp = wpair_ref[j]
        bias = bias_ref[:, j * LANES:(j + 1) * LANES]
        for c in range(tm // GMLP_CHUNK):
            rows = slice(c * GMLP_CHUNK, (c + 1) * GMLP_CHUNK)
            stacked = jnp.concatenate([vn_lo[rows], vn_hi[rows]], axis=0)
            mixed = _dot(wp, stacked) + bias
            oa_ref[rows, j * LANES:(j + 1) * LANES] = (u[rows] * mixed).astype(BF16)
    cos, sin = cos_ref[...], sin_ref[...]
    q_cos, q_sin = qg_ref[0:1] * cos, qg_ref[1:2] * sin
    k_cos, k_sin = kg_ref[0:1] * cos, kg_ref[1:2] * sin
    for j in range(nblk):
        blk = slice(j * LANES, (j + 1) * LANES)
        qb = proj[:, 2 * w + j * LANES:2 * w + (j + 1) * LANES]
        kb = proj[:, 3 * w + j * LANES:3 * w + (j + 1) * LANES]
        qp = proj[:, 5 * w + j * LANES:5 * w + (j + 1) * LANES]
        kp = proj[:, 6 * w + j * LANES:6 * w + (j + 1) * LANES]
        qr = _inv_rms_half_blocks(qb) * (LOG2E * DIFF_D ** -0.5)
        q_ref[:, blk] = ((qb * q_cos + qp * q_sin) * qr).astype(BF16)
        k_ref[:, blk] = ((kb * k_cos + kp * k_sin) * _inv_rms_half_blocks(kb)).astype(BF16)
    v_ref[...] = proj[:, 4 * w:5 * w].astype(BF16)


def _gain_pair(g, group):
    return jnp.stack([jnp.tile(g, 2), jnp.tile(_swap_pairs(g, group), 2)])


def _even_proj_call(x, seq, gain, w_in, sgu_norm, w_s, b_s, q_norm, k_norm, tables):
    t = x.shape[0]
    tm = PROJ_TM
    w = GMLP_WIDTH
    row = lambda i: (i, 0)
    pos = lambda i: (i % (seq // tm), 0)
    wpair = jnp.concatenate([w_s[0::2], w_s[1::2]], axis=2).astype(BF16)
    bias = jnp.repeat(b_s.T, GMLP_GROUP_DIM, axis=1)
    qk_cols = w_in[:, 2 * w:4 * w]
    win = jnp.concatenate([w_in, _swap_pairs(qk_cols, DIFF_D)], axis=1).astype(BF16)
    args = [x, gain.reshape(1, D_MODEL), win, sgu_norm.reshape(1, w), wpair, bias,
            _gain_pair(q_norm, DIFF_D), _gain_pair(k_norm, DIFF_D), *tables]
    in_specs = [pl.BlockSpec((tm, D_MODEL), row)] + [_const_spec(a.shape) for a in args[1:8]]
    in_specs += [pl.BlockSpec((tm, LANES), pos)] * 2
    out = jax.ShapeDtypeStruct((t, w), BF16)
    return pl.pallas_call(
        _even_proj_kernel, grid=(t // tm,), in_specs=in_specs,
        out_specs=[pl.BlockSpec((tm, w), row)] * 4, out_shape=[out] * 4,
        compiler_params=_params(("parallel",)), name="even_proj",
    )(*args)


def _odd_proj_kernel(x_ref, g_ref, win_ref, cqg_ref, ckvg_ref, wuq_ref, wuk_ref, wuv_ref,
                     mqg_ref, mkg_ref, gqg_ref, gkg_ref, mcos_ref, msin_ref, acos_ref, asin_ref,
                     qc_ref, kc_ref, vc_ref, qd_ref, kd_ref, vd_ref):
    xn = _rms_rows(x_ref[...], g_ref[...]).astype(BF16)
    proj = _dot(xn, win_ref[...])
    nq = GQA_Q_HEADS * GQA_DIM
    o1 = MLA_Q_RANK
    o2 = o1 + MLA_KV_RANK
    o3 = o2 + 2 * LANES
    o4 = o3 + 2 * nq
    o5 = o4 + 2 * LANES
    mcos, msin = mcos_ref[...], msin_ref[...]
    q_cos, q_sin = mqg_ref[0:1] * mcos, mqg_ref[1:2] * msin
    k_cos, k_sin = mkg_ref[0:1] * mcos, mkg_ref[1:2] * msin
    cq = _rms_rows(proj[:, :o1], cqg_ref[...]).astype(BF16)
    q_all = _dot(cq, wuq_ref[...])
    ckv = _rms_rows(proj[:, o1:o2], ckvg_ref[...]).astype(BF16)
    kn_all = _dot(ckv, wuk_ref[...])
    vc_ref[...] = _dot(ckv, wuv_ref[...]).astype(BF16)
    kpe = proj[:, o2:o2 + LANES]
    kpe_sin = proj[:, o2 + LANES:o3] * k_sin
    nh = MLA_HEADS * LANES
    for h in range(MLA_HEADS):
        blk = slice(h * LANES, (h + 1) * LANES)
        qb = q_all[:, blk]
        qp = q_all[:, nh + h * LANES:nh + (h + 1) * LANES]
        qr = _inv_rms_padded_block(qb, MLA_QK) * (LOG2E * MLA_QK ** -0.5)
        qc_ref[:, blk] = ((qb * q_cos + qp * q_sin) * qr).astype(BF16)
        kb = kn_all[:, blk] + kpe
        kc_ref[:, blk] = ((kb * k_cos + kpe_sin) * _inv_rms_padded_block(kb, MLA_QK)).astype(BF16)
    acos, asin = acos_ref[...], asin_ref[...]
    q_cos, q_sin = gqg_ref[0:1] * acos, gqg_ref[1:2] * asin
    for j in range(nq // LANES):
        blk = slice(j * LANES, (j + 1) * LANES)
        qb = proj[:, o3 + j * LANES:o3 + (j + 1) * LANES]
        qp = proj[:, o3 + nq + j * LANES:o3 + nq + (j + 1) * LANES]
        qr = _inv_rms_half_blocks(qb) * (LOG2E * GQA_DIM ** -0.5)
        qd_ref[:, blk] = ((qb * q_cos + qp * q_sin) * qr).astype(BF16)
    kb = proj[:, o4:o4 + LANES]
    kp = proj[:, o4 + LANES:o5]
    kd_ref[...] = ((kb * (gkg_ref[0:1] * acos) + kp * (gkg_ref[1:2] * asin))
                   * _inv_rms_half_blocks(kb)).astype(BF16)
    vd_ref[...] = proj[:, o5:].astype(BF16)


def _pad_heads(w, heads, width):
    r = w.shape[0]
    w = w.reshape(r, heads, width)
    return jnp.pad(w, ((0, 0), (0, 0), (0, LANES - width))).reshape(r, heads * LANES)


def _rope_partner_cols(w, heads):
    r = w.shape[0]
    w = w.reshape(r, heads, MLA_QK)
    rope = _swap_pairs(w[:, :, MLA_NOPE:], MLA_ROPE)
    return jnp.concatenate([jnp.zeros_like(w[:, :, :MLA_NOPE]), rope], axis=2).reshape(r, heads * MLA_QK)


def _odd_proj_call(x, seq, gain, w_in, cq_norm, ckv_norm, w_uq, w_ukv, mq_norm, mk_norm,
                   gq_norm, gk_norm, mla_tables, axial_tables):
    t = x.shape[0]
    tm = PROJ_TM
    row = lambda i: (i, 0)
    pos = lambda i: (i % (seq // tm), 0)
    o1 = MLA_Q_RANK
    o2 = o1 + MLA_KV_RANK
    o3 = o2 + MLA_ROPE
    o4 = o3 + GQA_Q_HEADS * GQA_DIM
    o5 = o4 + GQA_KV_HEADS * GQA_DIM
    place_rope = lambda c: jnp.pad(c, ((0, 0), (MLA_NOPE, LANES - MLA_QK)))
    kpe_cols = w_in[:, o2:o3]
    gq_cols = w_in[:, o3:o4].reshape(D_MODEL, GQA_KV_HEADS, GQA_GROUP, GQA_DIM)
    gq_cols = gq_cols.transpose(0, 2, 1, 3).reshape(D_MODEL, GQA_Q_HEADS * GQA_DIM)
    gk_cols = w_in[:, o4:o5]
    half = GQA_DIM // 2
    win = jnp.concatenate(
        [w_in[:, :o2], place_rope(kpe_cols), place_rope(_swap_pairs(kpe_cols, MLA_ROPE)),
         gq_cols, _swap_pairs(gq_cols, half), gk_cols, _swap_pairs(gk_cols, half), w_in[:, o5:]],
        axis=1).astype(BF16)
    wuq = jnp.concatenate([_pad_heads(w_uq, MLA_HEADS, MLA_QK),
                           _pad_heads(_rope_partner_cols(w_uq, MLA_HEADS), MLA_HEADS, MLA_QK)],
                          axis=1).astype(BF16)
    w_ukv = w_ukv.reshape(MLA_KV_RANK, MLA_HEADS, MLA_NOPE + MLA_V)
    wuk = _pad_heads(w_ukv[:, :, :MLA_NOPE].reshape(MLA_KV_RANK, -1), MLA_HEADS, MLA_NOPE).astype(BF16)
    wuv = w_ukv[:, :, MLA_NOPE:].reshape(MLA_KV_RANK, MLA_HEADS * MLA_V).astype(BF16)

    def mla_gain_pair(g):
        partner = jnp.concatenate([g[:MLA_NOPE], _swap_pairs(g[MLA_NOPE:], MLA_ROPE)])
        return jnp.pad(jnp.stack([g, partner]), ((0, 0), (0, LANES - MLA_QK)))

    args = [x, gain.reshape(1, D_MODEL), win, cq_norm.reshape(1, -1), ckv_norm.reshape(1, -1),
            wuq, wuk, wuv, mla_gain_pair(mq_norm), mla_gain_pair(mk_norm),
            _gain_pair(gq_norm, half), _gain_pair(gk_norm, half), *mla_tables, *axial_tables]
    in_specs = ([pl.BlockSpec((tm, D_MODEL), row)]
                + [_const_spec(a.shape) for a in args[1:12]]
                + [pl.BlockSpec((tm, LANES), pos)] * 4)
    widths = [MLA_HEADS * LANES, MLA_HEADS * LANES, MLA_HEADS * MLA_V,
              GQA_Q_HEADS * GQA_DIM, LANES, LANES]
    return pl.pallas_call(
        _odd_proj_kernel, grid=(t // tm,), in_specs=in_specs,
        out_specs=[pl.BlockSpec((tm, n), row) for n in widths],
        out_shape=[jax.ShapeDtypeStruct((t, n), BF16) for n in widths],
        compiler_params=_params(("parallel",)), name="odd_proj",
    )(*args)


def _attention_kernel(*refs, mode, lam_init):
    if mode == "diff":
        q_ref, k_ref, v_ref, lam_ref, sub_ref, o_ref = refs[:6]
    else:
        q_ref, k_ref, v_ref, o_ref = refs[:4]
    qt_ref, s_buf, m_ref, acc_ref = refs[-4:]
    tq = q_ref.shape[0]
    seq = k_ref.shape[0]
    tk = ATT_TK
    cw = ATT_CW
    n = seq // tk
    q = q_ref[...].astype(F32)
    if mode == "mla":
        qt_ref[:, :tq] = q[:, :LANES].T.astype(BF16)
        qt_ref[:, tq:] = q[:, LANES:].T.astype(BF16)
    else:
        low = _lane_is_low(q.shape)
        qt_ref[:, :tq] = jnp.where(low, q, 0.0).T.astype(BF16)
        qt_ref[:, tq:] = jnp.where(low, 0.0, q).T.astype(BF16)

    chunks = [slice(c * cw, (c + 1) * cw) for c in range(2 * tq // cw)]

    def tile_rows(t):
        return pl.ds(pl.multiple_of(t * tk, tk), tk)

    def scores(kt, slot, cols):
        kc = kt
        if mode == "mla":
            kc = kt[:, :LANES] if cols.start < tq else kt[:, LANES:]
        s_buf[slot, :, cols] = _dot(kc, qt_ref[:, cols])

    def softmax_values(vt, slot, cols):
        s = s_buf[slot, :, cols]
        m_old = m_ref[:, cols]
        m_new = jnp.maximum(m_old, jnp.max(s, axis=0, keepdims=True))
        alpha = jnp.exp2(m_old - m_new)
        p = jnp.exp2(s - m_new).astype(BF16)
        m_ref[:, cols] = m_new
        acc_ref[:, cols] = alpha * acc_ref[:, cols] + _dot(vt, p)

    def value_rows(t):
        ones = jnp.ones((SUM_ROWS, tk), BF16)
        return jnp.concatenate([v_ref[tile_rows(t), :].T, ones], axis=0)

    def half_step(t, t_next, cur, nxt):
        kt = k_ref[tile_rows(t_next), :]
        vt = value_rows(t)
        for cols in chunks:
            scores(kt, nxt, cols)
            softmax_values(vt, cur, cols)

    m_ref[...] = jnp.full(m_ref.shape, -jnp.inf, F32)
    acc_ref[...] = jnp.zeros(acc_ref.shape, F32)
    kt0 = k_ref[tile_rows(0), :]
    for cols in chunks:
        scores(kt0, 0, cols)

    def body(i, carry):
        for u in range(ATT_PAIRS):
            t = 2 * (i * ATT_PAIRS + u)
            half_step(t, t + 1, 0, 1)
            half_step(t + 1, jnp.minimum(t + 2, n - 1), 1, 0)
        return carry

    lax.fori_loop(0, n // (2 * ATT_PAIRS), body, 0)

    o = acc_ref[:LANES, :] / acc_ref[LANES:LANES + 1, :]
    if mode == "diff":
        lv = lam_ref[...]
        lam = (jnp.exp(jnp.sum(lv[0:1] * lv[1:2], axis=-1, keepdims=True))
               - jnp.exp(jnp.sum(lv[2:3] * lv[3:4], axis=-1, keepdims=True)) + lam_init)
        ot = (o[:, :tq] - lam * o[:, tq:]).T
        o_ref[...] = (_rms_rows(ot, sub_ref[...]) * (1.0 - lam_init)).astype(o_ref.dtype)
    else:
        half = LANES // 2
        ot = jnp.concatenate([o[:half, :tq], o[half:, tq:]], axis=0).T
        o_ref[...] = ot.astype(o_ref.dtype)


def _attention_call(q, k, v, mode, extra=(), lam_init=0.0):
    b, seq, _ = q.shape
    qw = 2 * LANES if mode == "mla" else LANES
    nblk = q.shape[2] // qw
    tq, tk = ATT_TQ, ATT_TK
    assert seq % (2 * ATT_PAIRS * tk) == 0 and seq % tq == 0 and tq % ATT_CW == 0
    shared = mode == "gqa"
    kv_idx = (lambda bi, j, i: (bi, 0, 0)) if shared else (lambda bi, j, i: (bi, 0, j))
    in_specs = [pl.BlockSpec((None, tq, qw), lambda bi, j, i: (bi, i, j)),
                pl.BlockSpec((None, seq, qw), kv_idx),
                pl.BlockSpec((None, seq, LANES), kv_idx)]
    in_specs += [_const_spec(e.shape) for e in extra]
    return pl.pallas_call(
        functools.partial(_attention_kernel, mode=mode, lam_init=lam_init),
        grid=(b, nblk, seq // tq), in_specs=in_specs,
        out_specs=pl.BlockSpec((None, tq, LANES), lambda bi, j, i: (bi, i, j)),
        out_shape=jax.ShapeDtypeStruct((b, seq, nblk * LANES), BF16),
        scratch_shapes=[pltpu.VMEM((LANES, 2 * tq), BF16),
                        pltpu.VMEM((2, tk, 2 * tq), F32),
                        pltpu.VMEM((1, 2 * tq), F32),
                        pltpu.VMEM((LANES + SUM_ROWS, 2 * tq), F32)],
        compiler_params=_params(("parallel", "parallel", "arbitrary")),
        name="attn_" + mode,
    )(q, k, v, *extra)


def _angles(pos, dim):
    inv = ROPE_THETA ** (-jnp.arange(0, dim, 2, dtype=F32) / dim)
    ang = pos.astype(F32)[:, None] * inv[None, :]
    return jnp.cos(ang), jnp.sin(ang)


def _rope_tables(seq):
    pos = jnp.arange(seq, dtype=jnp.int32)
    zeros = lambda n: jnp.zeros((seq, n), F32)
    c, s = _angles(pos, DIFF_D)
    full = (jnp.tile(jnp.concatenate([c, c], 1), (1, 2)), jnp.tile(jnp.concatenate([-s, s], 1), (1, 2)))
    c, s = _angles(pos, MLA_ROPE)
    tail = zeros(LANES - MLA_QK)
    mla = (jnp.concatenate([jnp.ones((seq, MLA_NOPE), F32), c, c, tail], 1),
           jnp.concatenate([zeros(MLA_NOPE), -s, s, tail], 1))
    cr, sr = _angles(pos // GRID_W, GQA_DIM // 2)
    cc, sc = _angles(pos % GRID_W, GQA_DIM // 2)
    axial = (jnp.tile(jnp.concatenate([cr, cr, cc, cc], 1), (1, 2)),
             jnp.tile(jnp.concatenate([-sr, sr, -sc, sc], 1), (1, 2)))
    return full, mla, axial


def kernel(x, ffn1_norm, ffn1_w_gu, ffn1_w_down, ffn2_norm, ffn2_w_gu, ffn2_w_down, ev_norm, ev_w_in, ev_sgu_norm, ev_w_s, ev_b_s, ev_q_norm, ev_k_norm, ev_lam_q1, ev_lam_k1, ev_lam_q2, ev_lam_k2, ev_sub_norm, ev_w_out, od_norm, od_w_in, od_cq_norm, od_ckv_norm, od_w_uq, od_w_ukv, od_mla_q_norm, od_mla_k_norm, od_gqa_q_norm, od_gqa_k_norm, od_w_out):
    b, seq, d = x.shape
    t = b * seq
    full_tab, mla_tab, axial_tab = _rope_tables(seq)
    x2 = x.reshape(t, d)
    ffn1 = (ffn1_norm[:, None, :], ffn1_w_gu.astype(BF16), ffn1_w_down.astype(BF16))
    ffn2 = (ffn2_norm[:, None, :], ffn2_w_gu.astype(BF16), ffn2_w_down.astype(BF16))

    x2 = _ffn_call(x2, 0, *ffn1)
    out_a, q, k, v = _even_proj_call(x2, seq, ev_norm[0], ev_w_in[0], ev_sgu_norm[0], ev_w_s[0],
                                     ev_b_s[0], ev_q_norm[0], ev_k_norm[0], full_tab)
    lam_init = 0.8 - 0.6 * float(np.exp(-0.3 * 0))
    lam_vecs = jnp.stack([ev_lam_q1[0], ev_lam_k1[0], ev_lam_q2[0], ev_lam_k2[0]])
    shp = (b, seq, GMLP_WIDTH)
    out_b = _attention_call(q.reshape(shp), k.reshape(shp), v.reshape(shp), "diff",
                            extra=(lam_vecs, ev_sub_norm[0].reshape(1, DIFF_V)), lam_init=lam_init)
    w_out = ev_w_out[0].astype(BF16)
    x2 = _ffn_call(x2, 0, *ffn2,
                   mix=(out_a, out_b.reshape(t, -1), w_out[:GMLP_WIDTH], w_out[GMLP_WIDTH:]))

    x2 = _ffn_call(x2, 1, *ffn1)
    qc, kc, vc, qd, kd, vd = _odd_proj_call(
        x2, seq, od_norm[0], od_w_in[0], od_cq_norm[0], od_ckv_norm[0], od_w_uq[0], od_w_ukv[0],
        od_mla_q_norm[0], od_mla_k_norm[0], od_gqa_q_norm[0], od_gqa_k_norm[0], mla_tab, axial_tab)
    r3 = lambda a: a.reshape(b, seq, a.shape[1])
    out_c = _attention_call(r3(qc), r3(kc), r3(vc), "mla")
    out_d = _attention_call(r3(qd), r3(kd), r3(vd), "gqa")
    w_out = od_w_out[0].astype(BF16)
    n_c = MLA_HEADS * MLA_V
    w_d = w_out[n_c:].reshape(GQA_KV_HEADS, GQA_GROUP, GQA_DIM, d).transpose(1, 0, 2, 3)
    x2 = _ffn_call(x2, 1, *ffn2,
                   mix=(out_c.reshape(t, -1), out_d.reshape(t, -1), w_out[:n_c],
                        w_d.reshape(GQA_Q_HEADS * GQA_DIM, d)))
    return x2.reshape(b, seq, d)
```

```python
import functools
import math

import numpy as np
import jax
import jax.numpy as jnp
from jax import lax
from jax.experimental import pallas as pl
from jax.experimental.pallas import tpu as pltpu

D_MODEL = 1024
D_FF = 2816
ROPE_THETA = 10000.0
GRID_W = 64
EPS = 1e-6
GMLP_GROUPS = 8
GMLP_GROUP_DIM = 64
GMLP_CHUNK = 128
GMLP_WIDTH = GMLP_GROUPS * GMLP_GROUP_DIM
DIFF_HEADS = 4
DIFF_D = 64
DIFF_V = 128
MLA_HEADS = 8
MLA_Q_RANK = 256
MLA_KV_RANK = 128
MLA_NOPE = 64
MLA_ROPE = 32
MLA_V = 64
MLA_QK = MLA_NOPE + MLA_ROPE
GQA_Q_HEADS = 8
GQA_KV_HEADS = 2
GQA_GROUP = GQA_Q_HEADS // GQA_KV_HEADS
GQA_DIM = 64

LANES = 128
VMEM_LIMIT_BYTES = 56 * 1024 * 1024

FFN_TM = 512
PROJ_TM = 512
ATT_TQ = 1024
ATT_TK = 256
ATT_CW = 512
ATT_PAIRS = 4
SUM_ROWS = 16
LOG2E = math.log2(math.e)

BF16 = jnp.bfloat16
F32 = jnp.float32


def _params(semantics):
    return pltpu.CompilerParams(dimension_semantics=semantics,
                                vmem_limit_bytes=VMEM_LIMIT_BYTES)


def _const_spec(shape):
    nd = len(shape)
    return pl.BlockSpec(shape, lambda *_: (0,) * nd, pipeline_mode=pl.Buffered(1))


def _rms_rows(x, gain):
    ms = jnp.mean(x * x, axis=-1, keepdims=True)
    return x * lax.rsqrt(ms + EPS) * gain


def _dot(a, b):
    return jnp.dot(a, b, preferred_element_type=F32)


def _lane_is_low(shape):
    return lax.broadcasted_iota(jnp.int32, shape, len(shape) - 1) < (LANES // 2)


def _inv_rms_half_blocks(x):
    low = _lane_is_low(x.shape)
    x2 = x * x
    x2_lo = jnp.where(low, x2, 0.0)
    x2_hi = x2 - x2_lo
    ms_lo = jnp.sum(x2_lo, axis=-1, keepdims=True) * (2.0 / LANES)
    ms_hi = jnp.sum(x2_hi, axis=-1, keepdims=True) * (2.0 / LANES)
    return jnp.where(low, lax.rsqrt(ms_lo + EPS), lax.rsqrt(ms_hi + EPS))


def _inv_rms_padded_block(x, width):
    ms = jnp.sum(x * x, axis=-1, keepdims=True) * (1.0 / width)
    return lax.rsqrt(ms + EPS)


def _swap_pairs(a, group):
    shape = a.shape
    a = a.reshape(shape[:-1] + (shape[-1] // group, 2, group // 2))
    return jnp.flip(a, axis=-2).reshape(shape)


def _ffn_body(x, g_ref, wgu_ref, wd_ref, o_ref):
    xn = _rms_rows(x, g_ref[...]).astype(BF16)
    h = _dot(xn, wgu_ref[...])
    gate = h[:, :D_FF]
    up = h[:, D_FF:]
    act = (gate / (1.0 + jnp.exp(-gate)) * up).astype(BF16)
    o_ref[...] = x + 0.5 * _dot(act, wd_ref[...])


def _ffn_kernel(x_ref, g_ref, wgu_ref, wd_ref, o_ref):
    _ffn_body(x_ref[...], g_ref, wgu_ref, wd_ref, o_ref)


def _mix_ffn_kernel(x_ref, a_ref, b_ref, wa_ref, wb_ref, g_ref, wgu_ref, wd_ref, o_ref):
    x = x_ref[...] + _dot(a_ref[...], wa_ref[...]) + _dot(b_ref[...], wb_ref[...])
    _ffn_body(x, g_ref, wgu_ref, wd_ref, o_ref)


def _layer_spec(shape, layer):
    nd = len(shape) - 1
    return pl.BlockSpec((None,) + tuple(shape[1:]), lambda *_: (layer,) + (0,) * nd,
                        pipeline_mode=pl.Buffered(1))


def _ffn_call(x, layer, gains, w_gu, w_down, mix=None):
    t = x.shape[0]
    tm = FFN_TM
    row = lambda i: (i, 0)
    x_spec = pl.BlockSpec((tm, D_MODEL), row)
    w_args = [gains, w_gu, w_down]
    w_specs = [_layer_spec(a.shape, layer) for a in w_args]
    if mix is None:
        kern, in_specs, args = _ffn_kernel, [x_spec] + w_specs, [x] + w_args
    else:
        a, b, wa, wb = mix
        kern = _mix_ffn_kernel
        in_specs = [x_spec, pl.BlockSpec((tm, a.shape[1]), row), pl.BlockSpec((tm, b.shape[1]), row),
                    _const_spec(wa.shape), _const_spec(wb.shape)] + w_specs
        args = [x, a, b, wa, wb] + w_args
    return pl.pallas_call(
        kern, grid=(t // tm,), in_specs=in_specs,
        out_specs=pl.BlockSpec((tm, D_MODEL), row),
        out_shape=jax.ShapeDtypeStruct((t, D_MODEL), F32),
        compiler_params=_params(("parallel",)),
        name="ffn" if mix is None else "mix_ffn",
    )(*args)


def _gelu_tanh(x):
    c = math.sqrt(2.0 / math.pi)
    return 0.5 * x * (1.0 + jnp.tanh(c * (x + 0.044715 * (x * x * x))))


def _even_proj_kernel(x_ref, g_ref, win_ref, sgu_ref, wpair_ref, bias_ref, qg_ref, kg_ref,
                      cos_ref, sin_ref, oa_ref, q_ref, k_ref, v_ref):
    xn = _rms_rows(x_ref[...], g_ref[...]).astype(BF16)
    proj = _dot(xn, win_ref[...])
    w = GMLP_WIDTH
    nblk = w // LANES
    tm = proj.shape[0]
    for j in range(nblk):
        u = _gelu_tanh(proj[:, j * LANES:(j + 1) * LANES])
        v = _gelu_tanh(proj[:, w + j * LANES:w + (j + 1) * LANES])
        vn = v * _inv_rms_half_blocks(v) * sgu_ref[:, j * LANES:(j + 1) * LANES]
        low = _lane_is_low(vn.shape)
        vn_lo = jnp.where(low, vn, 0.0).astype(BF16)
        vn_hi = jnp.where(low, 0.0, vn).astype(BF16)
        wp = wpair_ref[j]
        bias = bias_ref[:, j * LANES:(j + 1) * LANES]
        for c in range(tm // GMLP_CHUNK):
            rows = slice(c * GMLP_CHUNK, (c + 1) * GMLP_CHUNK)
            stacked = jnp.concatenate([vn_lo[rows], vn_hi[rows]], axis=0)
            mixed = _dot(wp, stacked) + bias
            oa_ref[rows, j * LANES:(j + 1) * LANES] = (u[rows] * mixed).astype(BF16)
    cos, sin = cos_ref[...], sin_ref[...]
    q_cos, q_sin = qg_ref[0:1] * cos, qg_ref[1:2] * sin
    k_cos, k_sin = kg_ref[0:1] * cos, kg_ref[1:2] * sin
    for j in range(nblk):
        blk = slice(j * LANES, (j + 1) * LANES)
        qb = proj[:, 2 * w + j * LANES:2 * w + (j + 1) * LANES]
        kb = proj[:, 3 * w + j * LANES:3 * w + (j + 1) * LANES]
        qp = proj[:, 5 * w + j * LANES:5 * w + (j + 1) * LANES]
        kp = proj[:, 6 * w + j * LANES:6 * w + (j + 1) * LANES]
        qr = _inv_rms_half_blocks(qb) * (LOG2E * DIFF_D ** -0.5)
        q_ref[:, blk] = ((qb * q_cos + qp * q_sin) * qr).astype(BF16)
        k_ref[:, blk] = ((kb * k_cos + kp * k_sin) * _inv_rms_half_blocks(kb)).astype(BF16)
    v_ref[...] = proj[:, 4 * w:5 * w].astype(BF16)


def _gain_pair(g, group):
    return jnp.stack([jnp.tile(g, 2), jnp.tile(_swap_pairs(g, group), 2)])


def _even_proj_call(x, seq, gain, w_in, sgu_norm, w_s, b_s, q_norm, k_norm, tables):
    t = x.shape[0]
    tm = PROJ_TM
    w = GMLP_WIDTH
    row = lambda i: (i, 0)
    pos = lambda i: (i % (seq // tm), 0)
    wpair = jnp.concatenate([w_s[0::2], w_s[1::2]], axis=2).astype(BF16)
    bias = jnp.repeat(b_s.T, GMLP_GROUP_DIM, axis=1)
    qk_cols = w_in[:, 2 * w:4 * w]
    win = jnp.concatenate([w_in, _swap_pairs(qk_cols, DIFF_D)], axis=1).astype(BF16)
    args = [x, gain.reshape(1, D_MODEL), win, sgu_norm.reshape(1, w), wpair, bias,
            _gain_pair(q_norm, DIFF_D), _gain_pair(k_norm, DIFF_D), *tables]
    in_specs = [pl.BlockSpec((tm, D_MODEL), row)] + [_const_spec(a.shape) for a in args[1:8]]
    in_specs += [pl.BlockSpec((tm, LANES), pos)] * 2
    out = jax.ShapeDtypeStruct((t, w), BF16)
    return pl.pallas_call(
        _even_proj_kernel, grid=(t // tm,), in_specs=in_specs,
        out_specs=[pl.BlockSpec((tm, w), row)] * 4, out_shape=[out] * 4,
        compiler_params=_params(("parallel",)), name="even_proj",
    )(*args)


def _odd_proj_kernel(x_ref, g_ref, win_ref, cqg_ref, ckvg_ref, wuq_ref, wuk_ref, wuv_ref,
                     mqg_ref, mkg_ref, gqg_ref, gkg_ref, mcos_ref, msin_ref, acos_ref, asin_ref,
                     qc_ref, kc_ref, vc_ref, qd_ref, kd_ref, vd_ref):
    xn = _rms_rows(x_ref[...], g_ref[...]).astype(BF16)
    proj = _dot(xn, win_ref[...])
    nq = GQA_Q_HEADS * GQA_DIM
    o1 = MLA_Q_RANK
    o2 = o1 + MLA_KV_RANK
    o3 = o2 + 2 * LANES
    o4 = o3 + 2 * nq
    o5 = o4 + 2 * LANES
    mcos, msin = mcos_ref[...], msin_ref[...]
    q_cos, q_sin = mqg_ref[0:1] * mcos, mqg_ref[1:2] * msin
    k_cos, k_sin = mkg_ref[0:1] * mcos, mkg_ref[1:2] * msin
    cq = _rms_rows(proj[:, :o1], cqg_ref[...]).astype(BF16)
    q_all = _dot(cq, wuq_ref[...])
    ckv = _rms_rows(proj[:, o1:o2], ckvg_ref[...]).astype(BF16)
    kn_all = _dot(ckv, wuk_ref[...])
    vc_ref[...] = _dot(ckv, wuv_ref[...]).astype(BF16)
    kpe = proj[:, o2:o2 + LANES]
    kpe_sin = proj[:, o2 + LANES:o3] * k_sin
    nh = MLA_HEADS * LANES
    for h in range(MLA_HEADS):
        blk = slice(h * LANES, (h + 1) * LANES)
        qb = q_all[:, blk]
        qp = q_all[:, nh + h * LANES:nh + (h + 1) * LANES]
        qr = _inv_rms_padded_block(qb, MLA_QK) * (LOG2E * MLA_QK ** -0.5)
        qc_ref[:, blk] = ((qb * q_cos + qp * q_sin) * qr).astype(BF16)
        kb = kn_all[:, blk] + kpe
        kc_ref[:, blk] = ((kb * k_cos + kpe_sin) * _inv_rms_padded_block(kb, MLA_QK)).astype(BF16)
    acos, asin = acos_ref[...], asin_ref[...]
    q_cos, q_sin = gqg_ref[0:1] * acos, gqg_ref[1:2] * asin
    for j in range(nq // LANES):
        blk = slice(j * LANES, (j + 1) * LANES)
        qb = proj[:, o3 + j * LANES:o3 + (j + 1) * LANES]
        qp = proj[:, o3 + nq + j * LANES:o3 + nq + (j + 1) * LANES]
        qr = _inv_rms_half_blocks(qb) * (LOG2E * GQA_DIM ** -0.5)
        qd_ref[:, blk] = ((qb * q_cos + qp * q_sin) * qr).astype(BF16)
    kb = proj[:, o4:o4 + LANES]
    kp = proj[:, o4 + LANES:o5]
    kd_ref[...] = ((kb * (gkg_ref[0:1] * acos) + kp * (gkg_ref[1:2] * asin))
                   * _inv_rms_half_blocks(kb)).astype(BF16)
    vd_ref[...] = proj[:, o5:].astype(BF16)


def _pad_heads(w, heads, width):
    r = w.shape[0]
    w = w.reshape(r, heads, width)
    return jnp.pad(w, ((0, 0), (0, 0), (0, LANES - width))).reshape(r, heads * LANES)


def _rope_partner_cols(w, heads):
    r = w.shape[0]
    w = w.reshape(r, heads, MLA_QK)
    rope = _swap_pairs(w[:, :, MLA_NOPE:], MLA_ROPE)
    return jnp.concatenate([jnp.zeros_like(w[:, :, :MLA_NOPE]), rope], axis=2).reshape(r, heads * MLA_QK)


def _odd_proj_call(x, seq, gain, w_in, cq_norm, ckv_norm, w_uq, w_ukv, mq_norm, mk_norm,
                   gq_norm, gk_norm, mla_tables, axial_tables):
    t = x.shape[0]
    tm = PROJ_TM
    row = lambda i: (i, 0)
    pos = lambda i: (i % (seq // tm), 0)
    o1 = MLA_Q_RANK
    o2 = o1 + MLA_KV_RANK
    o3 = o2 + MLA_ROPE
    o4 = o3 + GQA_Q_HEADS * GQA_DIM
    o5 = o4 + GQA_KV_HEADS * GQA_DIM
    place_rope = lambda c: jnp.pad(c, ((0, 0), (MLA_NOPE, LANES - MLA_QK)))
    kpe_cols = w_in[:, o2:o3]
    gq_cols = w_in[:, o3:o4].reshape(D_MODEL, GQA_KV_HEADS, GQA_GROUP, GQA_DIM)
    gq_cols = gq_cols.transpose(0, 2, 1, 3).reshape(D_MODEL, GQA_Q_HEADS * GQA_DIM)
    gk_cols = w_in[:, o4:o5]
    half = GQA_DIM // 2
    win = jnp.concatenate(
        [w_in[:, :o2], place_rope(kpe_cols), place_rope(_swap_pairs(kpe_cols, MLA_ROPE)),
         gq_cols, _swap_pairs(gq_cols, half), gk_cols, _swap_pairs(gk_cols, half), w_in[:, o5:]],
        axis=1).astype(BF16)
    wuq = jnp.concatenate([_pad_heads(w_uq, MLA_HEADS, MLA_QK),
                           _pad_heads(_rope_partner_cols(w_uq, MLA_HEADS), MLA_HEADS, MLA_QK)],
                          axis=1).astype(BF16)
    w_ukv = w_ukv.reshape(MLA_KV_RANK, MLA_HEADS, MLA_NOPE + MLA_V)
    wuk = _pad_heads(w_ukv[:, :, :MLA_NOPE].reshape(MLA_KV_RANK, -1), MLA_HEADS, MLA_NOPE).astype(BF16)
    wuv = w_ukv[:, :, MLA_NOPE:].reshape(MLA_KV_RANK, MLA_HEADS * MLA_V).astype(BF16)

    def mla_gain_pair(g):
        partner = jnp.concatenate([g[:MLA_NOPE], _swap_pairs(g[MLA_NOPE:], MLA_ROPE)])
        return jnp.pad(jnp.stack([g, partner]), ((0, 0), (0, LANES - MLA_QK)))

    args = [x, gain.reshape(1, D_MODEL), win, cq_norm.reshape(1, -1), ckv_norm.reshape(1, -1),
            wuq, wuk, wuv, mla_gain_pair(mq_norm), mla_gain_pair(mk_norm),
            _gain_pair(gq_norm, half), _gain_pair(gk_norm, half), *mla_tables, *axial_tables]
    in_specs = ([pl.BlockSpec((tm, D_MODEL), row)]
                + [_const_spec(a.shape) for a in args[1:12]]
                + [pl.BlockSpec((tm, LANES), pos)] * 4)
    widths = [MLA_HEADS * LANES, MLA_HEADS * LANES, MLA_HEADS * MLA_V,
              GQA_Q_HEADS * GQA_DIM, LANES, LANES]
    return pl.pallas_call(
        _odd_proj_kernel, grid=(t // tm,), in_specs=in_specs,
        out_specs=[pl.BlockSpec((tm, n), row) for n in widths],
        out_shape=[jax.ShapeDtypeStruct((t, n), BF16) for n in widths],
        compiler_params=_params(("parallel",)), name="odd_proj",
    )(*args)


def _attention_kernel(*refs, mode, lam_init):
    if mode == "diff":
        q_ref, k_ref, v_ref, lam_ref, sub_ref, o_ref = refs[:6]
    else:
        q_ref, k_ref, v_ref, o_ref = refs[:4]
    qt_ref, s_buf, m_ref, acc_ref = refs[-4:]
    tq = q_ref.shape[0]
    seq = k_ref.shape[0]
    tk = ATT_TK
    cw = ATT_CW
    n = seq // tk
    q = q_ref[...].astype(F32)
    if mode == "mla":
        qt_ref[:, :tq] = q[:, :LANES].T.astype(BF16)
        qt_ref[:, tq:] = q[:, LANES:].T.astype(BF16)
    else:
        qt = q.T
        first = lax.broadcasted_iota(jnp.int32, qt.shape, 0) < (LANES // 2)
        qt_ref[:, :tq] = jnp.where(first, qt, 0.0).astype(BF16)
        qt_ref[:, tq:] = jnp.where(first, 0.0, qt).astype(BF16)

    chunks = [slice(c * cw, (c + 1) * cw) for c in range(2 * tq // cw)]

    def tile_rows(t):
        return pl.ds(pl.multiple_of(t * tk, tk), tk)

    def scores(kt, slot, cols):
        kc = kt
        if mode == "mla":
            kc = kt[:, :LANES] if cols.start < tq else kt[:, LANES:]
        s_buf[slot, :, cols] = _dot(kc, qt_ref[:, cols])

    def softmax_values(vt, slot, cols):
        s = s_buf[slot, :, cols]
        m_old = m_ref[:, cols]
        m_new = jnp.maximum(m_old, jnp.max(s, axis=0, keepdims=True))
        alpha = jnp.exp2(m_old - m_new)
        p = jnp.exp2(s - m_new).astype(BF16)
        m_ref[:, cols] = m_new
        acc_ref[:, cols] = alpha * acc_ref[:, cols] + _dot(vt, p)

    def value_rows(t):
        ones = jnp.ones((SUM_ROWS, tk), BF16)
        vt = v_ref[tile_rows(t), :].T
        if mode == "diff":
            both = jnp.concatenate([vt, ones], axis=0)
            return both, both
        half = LANES // 2
        return (jnp.concatenate([vt[:half], ones], axis=0), jnp.concatenate([vt[half:], ones], axis=0))

    def half_step(t, t_next, cur, nxt):
        kt = k_ref[tile_rows(t_next), :]
        vt = value_rows(t)
        for cols in chunks:
            scores(kt, nxt, cols)
            softmax_values(vt[0] if cols.start < tq else vt[1], cur, cols)

    m_ref[...] = jnp.full(m_ref.shape, -jnp.inf, F32)
    acc_ref[...] = jnp.zeros(acc_ref.shape, F32)
    kt0 = k_ref[tile_rows(0), :]
    for cols in chunks:
        scores(kt0, 0, cols)

    def body(i, carry):
        for u in range(ATT_PAIRS):
            t = 2 * (i * ATT_PAIRS + u)
            half_step(t, t + 1, 0, 1)
            half_step(t + 1, jnp.minimum(t + 2, n - 1), 1, 0)
        return carry

    lax.fori_loop(0, n // (2 * ATT_PAIRS), body, 0)

    nv = acc_ref.shape[0] - SUM_ROWS
    o = acc_ref[:nv, :] / acc_ref[nv:nv + 1, :]
    if mode == "diff":
        lv = lam_ref[...]
        lam = (jnp.exp(jnp.sum(lv[0:1] * lv[1:2], axis=-1, keepdims=True))
               - jnp.exp(jnp.sum(lv[2:3] * lv[3:4], axis=-1, keepdims=True)) + lam_init)
        ot = o[:, :tq] - lam * o[:, tq:]
        ms = jnp.mean(ot * ot, axis=0, keepdims=True)
        ot = ot * lax.rsqrt(ms + EPS) * (sub_ref[...] * (1.0 - lam_init))
        o_ref[...] = ot.T.astype(o_ref.dtype)
    else:
        ot = jnp.concatenate([o[:, :tq], o[:, tq:]], axis=0).T
        o_ref[...] = ot.astype(o_ref.dtype)


def _attention_call(q, k, v, mode, extra=(), lam_init=0.0):
    b, seq, _ = q.shape
    qw = 2 * LANES if mode == "mla" else LANES
    nblk = q.shape[2] // qw
    tq, tk = ATT_TQ, ATT_TK
    assert seq % (2 * ATT_PAIRS * tk) == 0 and seq % tq == 0 and tq % ATT_CW == 0
    shared = mode == "gqa"
    value_rows = LANES if mode == "diff" else LANES // 2
    kv_idx = (lambda bi, j, i: (bi, 0, 0)) if shared else (lambda bi, j, i: (bi, 0, j))
    in_specs = [pl.BlockSpec((None, tq, qw), lambda bi, j, i: (bi, i, j)),
                pl.BlockSpec((None, seq, qw), kv_idx),
                pl.BlockSpec((None, seq, LANES), kv_idx)]
    in_specs += [_const_spec(e.shape) for e in extra]
    return pl.pallas_call(
        functools.partial(_attention_kernel, mode=mode, lam_init=lam_init),
        grid=(b, nblk, seq // tq), in_specs=in_specs,
        out_specs=pl.BlockSpec((None, tq, LANES), lambda bi, j, i: (bi, i, j)),
        out_shape=jax.ShapeDtypeStruct((b, seq, nblk * LANES), BF16),
        scratch_shapes=[pltpu.VMEM((LANES, 2 * tq), BF16),
                        pltpu.VMEM((2, tk, 2 * tq), F32),
                        pltpu.VMEM((1, 2 * tq), F32),
                        pltpu.VMEM((value_rows + SUM_ROWS, 2 * tq), F32)],
        compiler_params=_params(("parallel", "parallel", "arbitrary")),
        name="attn_" + mode,
    )(q, k, v, *extra)


def _angles(pos, dim):
    inv = ROPE_THETA ** (-jnp.arange(0, dim, 2, dtype=F32) / dim)
    ang = pos.astype(F32)[:, None] * inv[None, :]
    return lax.optimization_barrier((jnp.cos(ang), jnp.sin(ang)))


def _rope_tables(seq):
    pos = jnp.arange(seq, dtype=jnp.int32)
    zeros = lambda n: jnp.zeros((seq, n), F32)
    c, s = _angles(pos, DIFF_D)
    full = (jnp.tile(jnp.concatenate([c, c], 1), (1, 2)), jnp.tile(jnp.concatenate([-s, s], 1), (1, 2)))
    c, s = _angles(pos, MLA_ROPE)
    tail = zeros(LANES - MLA_QK)
    mla = (jnp.concatenate([jnp.ones((seq, MLA_NOPE), F32), c, c, tail], 1),
           jnp.concatenate([zeros(MLA_NOPE), -s, s, tail], 1))
    cr, sr = _angles(pos // GRID_W, GQA_DIM // 2)
    cc, sc = _angles(pos % GRID_W, GQA_DIM // 2)
    axial = (jnp.tile(jnp.concatenate([cr, cr, cc, cc], 1), (1, 2)),
             jnp.tile(jnp.concatenate([-sr, sr, -sc, sc], 1), (1, 2)))
    return full, mla, axial


def kernel(x, ffn1_norm, ffn1_w_gu, ffn1_w_down, ffn2_norm, ffn2_w_gu, ffn2_w_down, ev_norm, ev_w_in, ev_sgu_norm, ev_w_s, ev_b_s, ev_q_norm, ev_k_norm, ev_lam_q1, ev_lam_k1, ev_lam_q2, ev_lam_k2, ev_sub_norm, ev_w_out, od_norm, od_w_in, od_cq_norm, od_ckv_norm, od_w_uq, od_w_ukv, od_mla_q_norm, od_mla_k_norm, od_gqa_q_norm, od_gqa_k_norm, od_w_out):
    b, seq, d = x.shape
    t = b * seq
    full_tab, mla_tab, axial_tab = _rope_tables(seq)
    x2 = x.reshape(t, d)
    ffn1 = (ffn1_norm[:, None, :], ffn1_w_gu.astype(BF16), ffn1_w_down.astype(BF16))
    ffn2 = (ffn2_norm[:, None, :], ffn2_w_gu.astype(BF16), ffn2_w_down.astype(BF16))

    x2 = _ffn_call(x2, 0, *ffn1)
    out_a, q, k, v = _even_proj_call(x2, seq, ev_norm[0], ev_w_in[0], ev_sgu_norm[0], ev_w_s[0],
                                     ev_b_s[0], ev_q_norm[0], ev_k_norm[0], full_tab)
    lam_init = 0.8 - 0.6 * float(np.exp(-0.3 * 0))
    lam_vecs = jnp.stack([ev_lam_q1[0], ev_lam_k1[0], ev_lam_q2[0], ev_lam_k2[0]])
    shp = (b, seq, GMLP_WIDTH)
    out_b = _attention_call(q.reshape(shp), k.reshape(shp), v.reshape(shp), "diff",
                            extra=(lam_vecs, ev_sub_norm[0].reshape(DIFF_V, 1)), lam_init=lam_init)
    w_out = ev_w_out[0].astype(BF16)
    x2 = _ffn_call(x2, 0, *ffn2,
                   mix=(out_a, out_b.reshape(t, -1), w_out[:GMLP_WIDTH], w_out[GMLP_WIDTH:]))

    x2 = _ffn_call(x2, 1, *ffn1)
    qc, kc, vc, qd, kd, vd = _odd_proj_call(
        x2, seq, od_norm[0], od_w_in[0], od_cq_norm[0], od_ckv_norm[0], od_w_uq[0], od_w_ukv[0],
        od_mla_q_norm[0], od_mla_k_norm[0], od_gqa_q_norm[0], od_gqa_k_norm[0], mla_tab, axial_tab)
    r3 = lambda a: a.reshape(b, seq, a.shape[1])
    out_c = _attention_call(r3(qc), r3(kc), r3(vc), "mla")
    out_d = _attention_call(r3(qd), r3(kd), r3(vd), "gqa")
    w_out = od_w_out[0].astype(BF16)
    n_c = MLA_HEADS * MLA_V
    w_d = w_out[n_c:].reshape(GQA_KV_HEADS, GQA_GROUP, GQA_DIM, d).transpose(1, 0, 2, 3)
    x2 = _ffn_call(x2, 1, *ffn2,
                   mix=(out_c.reshape(t, -1), out_d.reshape(t, -1), w_out[:n_c],
                        w_d.reshape(GQA_Q_HEADS * GQA_DIM, d)))
    return x2.reshape(b, seq, d)
```

```python
import functools
import math

import numpy as np
import jax
import jax.numpy as jnp
from jax import lax
from jax.experimental import pallas as pl
from jax.experimental.pallas import tpu as pltpu

D_MODEL = 1024
D_FF = 2816
ROPE_THETA = 10000.0
GRID_W = 64
EPS = 1e-6
GMLP_GROUPS = 8
GMLP_GROUP_DIM = 64
GMLP_CHUNK = 128
GMLP_WIDTH = GMLP_GROUPS * GMLP_GROUP_DIM
DIFF_HEADS = 4
DIFF_D = 64
DIFF_V = 128
MLA_HEADS = 8
MLA_Q_RANK = 256
MLA_KV_RANK = 128
MLA_NOPE = 64
MLA_ROPE = 32
MLA_V = 64
MLA_QK = MLA_NOPE + MLA_ROPE
GQA_Q_HEADS = 8
GQA_KV_HEADS = 2
GQA_GROUP = GQA_Q_HEADS // GQA_KV_HEADS
GQA_DIM = 64

LANES = 128
VMEM_LIMIT_BYTES = 56 * 1024 * 1024

FFN_TM = 512
PROJ_TM = 512
ATT_TQ = 1024
ATT_TK = 256
ATT_CW = 512
ATT_PAIRS = 2
SUM_ROWS = 16
LOG2E = math.log2(math.e)

BF16 = jnp.bfloat16
F32 = jnp.float32


def _params(semantics):
    return pltpu.CompilerParams(dimension_semantics=semantics,
                                vmem_limit_bytes=VMEM_LIMIT_BYTES)


def _const_spec(shape):
    nd = len(shape)
    return pl.BlockSpec(shape, lambda *_: (0,) * nd, pipeline_mode=pl.Buffered(1))


def _rms_rows(x, gain):
    ms = jnp.mean(x * x, axis=-1, keepdims=True)
    return x * lax.rsqrt(ms + EPS) * gain


def _dot(a, b):
    return jnp.dot(a, b, preferred_element_type=F32)


def _lane_is_low(shape):
    return lax.broadcasted_iota(jnp.int32, shape, len(shape) - 1) < (LANES // 2)


def _inv_rms_half_blocks(x):
    low = _lane_is_low(x.shape)
    x2 = x * x
    x2_lo = jnp.where(low, x2, 0.0)
    x2_hi = x2 - x2_lo
    ms_lo = jnp.sum(x2_lo, axis=-1, keepdims=True) * (2.0 / LANES)
    ms_hi = jnp.sum(x2_hi, axis=-1, keepdims=True) * (2.0 / LANES)
    return jnp.where(low, lax.rsqrt(ms_lo + EPS), lax.rsqrt(ms_hi + EPS))


def _inv_rms_padded_block(x, width):
    ms = jnp.sum(x * x, axis=-1, keepdims=True) * (1.0 / width)
    return lax.rsqrt(ms + EPS)


def _swap_pairs(a, group):
    shape = a.shape
    a = a.reshape(shape[:-1] + (shape[-1] // group, 2, group // 2))
    return jnp.flip(a, axis=-2).reshape(shape)


def _ffn_body(x, g_ref, wgu_ref, wd_ref, o_ref):
    xn = _rms_rows(x, g_ref[...]).astype(BF16)
    h = _dot(xn, wgu_ref[...])
    gate = h[:, :D_FF]
    up = h[:, D_FF:]
    act = (gate / (1.0 + jnp.exp(-gate)) * up).astype(BF16)
    o_ref[...] = x + 0.5 * _dot(act, wd_ref[...])


def _ffn_kernel(x_ref, g_ref, wgu_ref, wd_ref, o_ref):
    _ffn_body(x_ref[...], g_ref, wgu_ref, wd_ref, o_ref)


def _mix_ffn_kernel(x_ref, a_ref, b_ref, wa_ref, wb_ref, g_ref, wgu_ref, wd_ref, o_ref):
    x = x_ref[...] + _dot(a_ref[...], wa_ref[...]) + _dot(b_ref[...], wb_ref[...])
    _ffn_body(x, g_ref, wgu_ref, wd_ref, o_ref)


def _layer_spec(shape, layer):
    nd = len(shape) - 1
    return pl.BlockSpec((None,) + tuple(shape[1:]), lambda *_: (layer,) + (0,) * nd,
                        pipeline_mode=pl.Buffered(1))


def _ffn_call(x, layer, gains, w_gu, w_down, mix=None):
    t = x.shape[0]
    tm = FFN_TM
    row = lambda i: (i, 0)
    x_spec = pl.BlockSpec((tm, D_MODEL), row)
    w_args = [gains, w_gu, w_down]
    w_specs = [_layer_spec(a.shape, layer) for a in w_args]
    if mix is None:
        kern, in_specs, args = _ffn_kernel, [x_spec] + w_specs, [x] + w_args
    else:
        a, b, wa, wb = mix
        kern = _mix_ffn_kernel
        in_specs = [x_spec, pl.BlockSpec((tm, a.shape[1]), row), pl.BlockSpec((tm, b.shape[1]), row),
                    _const_spec(wa.shape), _const_spec(wb.shape)] + w_specs
        args = [x, a, b, wa, wb] + w_args
    return pl.pallas_call(
        kern, grid=(t // tm,), in_specs=in_specs,
        out_specs=pl.BlockSpec((tm, D_MODEL), row),
        out_shape=jax.ShapeDtypeStruct((t, D_MODEL), F32),
        compiler_params=_params(("parallel",)),
        name="ffn" if mix is None else "mix_ffn",
    )(*args)


def _gelu_tanh(x):
    c = math.sqrt(2.0 / math.pi)
    return 0.5 * x * (1.0 + jnp.tanh(c * (x + 0.044715 * (x * x * x))))


def _even_proj_kernel(x_ref, g_ref, win_ref, sgu_ref, wpair_ref, bias_ref, qg_ref, kg_ref,
                      cos_ref, sin_ref, oa_ref, q_ref, k_ref, v_ref):
    xn = _rms_rows(x_ref[...], g_ref[...]).astype(BF16)
    proj = _dot(xn, win_ref[...])
    w = GMLP_WIDTH
    nblk = w // LANES
    tm = proj.shape[0]
    for j in range(nblk):
        u = _gelu_tanh(proj[:, j * LANES:(j + 1) * LANES])
        v = _gelu_tanh(proj[:, w + j * LANES:w + (j + 1) * LANES])
        vn = v * _inv_rms_half_blocks(v) * sgu_ref[:, j * LANES:(j + 1) * LANES]
        low = _lane_is_low(vn.shape)
        vn_lo = jnp.where(low, vn, 0.0).astype(BF16)
        vn_hi = jnp.where(low, 0.0, vn).astype(BF16)
        wp = wpair_ref[j]
        bias = bias_ref[:, j * LANES:(j + 1) * LANES]
        for c in range(tm // GMLP_CHUNK):
            rows = slice(c * GMLP_CHUNK, (c + 1) * GMLP_CHUNK)
            stacked = jnp.concatenate([vn_lo[rows], vn_hi[rows]], axis=0)
            mixed = _dot(wp, stacked) + bias
            oa_ref[rows, j * LANES:(j + 1) * LANES] = (u[rows] * mixed).astype(BF16)
    cos, sin = cos_ref[...], sin_ref[...]
    q_cos, q_sin = qg_ref[0:1] * cos, qg_ref[1:2] * sin
    k_cos, k_sin = kg_ref[0:1] * cos, kg_ref[1:2] * sin
    for j in range(nblk):
        blk = slice(j * LANES, (j + 1) * LANES)
        qb = proj[:, 2 * w + j * LANES:2 * w + (j + 1) * LANES]
        kb = proj[:, 3 * w + j * LANES:3 * w + (j + 1) * LANES]
        qp = proj[:, 5 * w + j * LANES:5 * w + (j + 1) * LANES]
        kp = proj[:, 6 * w + j * LANES:6 * w + (j + 1) * LANES]
        qr = _inv_rms_half_blocks(qb) * (LOG2E * DIFF_D ** -0.5)
        q_ref[:, blk] = ((qb * q_cos + qp * q_sin) * qr).astype(BF16)
        k_ref[:, blk] = ((kb * k_cos + kp * k_sin) * _inv_rms_half_blocks(kb)).astype(BF16)
    v_ref[...] = proj[:, 4 * w:5 * w].astype(BF16)


def _gain_pair(g, group):
    return jnp.stack([jnp.tile(g, 2), jnp.tile(_swap_pairs(g, group), 2)])


def _even_proj_call(x, seq, gain, w_in, sgu_norm, w_s, b_s, q_norm, k_norm, tables):
    t = x.shape[0]
    tm = PROJ_TM
    w = GMLP_WIDTH
    row = lambda i: (i, 0)
    pos = lambda i: (i % (seq // tm), 0)
    wpair = jnp.concatenate([w_s[0::2], w_s[1::2]], axis=2).astype(BF16)
    bias = jnp.repeat(b_s.T, GMLP_GROUP_DIM, axis=1)
    qk_cols = w_in[:, 2 * w:4 * w]
    win = jnp.concatenate([w_in, _swap_pairs(qk_cols, DIFF_D)], axis=1).astype(BF16)
    args = [x, gain.reshape(1, D_MODEL), win, sgu_norm.reshape(1, w), wpair, bias,
            _gain_pair(q_norm, DIFF_D), _gain_pair(k_norm, DIFF_D), *tables]
    in_specs = [pl.BlockSpec((tm, D_MODEL), row)] + [_const_spec(a.shape) for a in args[1:8]]
    in_specs += [pl.BlockSpec((tm, LANES), pos)] * 2
    out = jax.ShapeDtypeStruct((t, w), BF16)
    return pl.pallas_call(
        _even_proj_kernel, grid=(t // tm,), in_specs=in_specs,
        out_specs=[pl.BlockSpec((tm, w), row)] * 4, out_shape=[out] * 4,
        compiler_params=_params(("parallel",)), name="even_proj",
    )(*args)


def _odd_proj_kernel(x_ref, g_ref, win_ref, cqg_ref, ckvg_ref, wuq_ref, wuk_ref, wuv_ref,
                     mqg_ref, mkg_ref, gqg_ref, gkg_ref, mcos_ref, msin_ref, acos_ref, asin_ref,
                     qc_ref, kc_ref, vc_ref, qd_ref, kd_ref, vd_ref):
    xn = _rms_rows(x_ref[...], g_ref[...]).astype(BF16)
    proj = _dot(xn, win_ref[...])
    nq = GQA_Q_HEADS * GQA_DIM
    o1 = MLA_Q_RANK
    o2 = o1 + MLA_KV_RANK
    o3 = o2 + 2 * LANES
    o4 = o3 + 2 * nq
    o5 = o4 + 2 * LANES
    mcos, msin = mcos_ref[...], msin_ref[...]
    q_cos, q_sin = mqg_ref[0:1] * mcos, mqg_ref[1:2] * msin
    k_cos, k_sin = mkg_ref[0:1] * mcos, mkg_ref[1:2] * msin
    cq = _rms_rows(proj[:, :o1], cqg_ref[...]).astype(BF16)
    q_all = _dot(cq, wuq_ref[...])
    ckv = _rms_rows(proj[:, o1:o2], ckvg_ref[...]).astype(BF16)
    kn_all = _dot(ckv, wuk_ref[...])
    vc_ref[...] = _dot(ckv, wuv_ref[...]).astype(BF16)
    kpe = proj[:, o2:o2 + LANES]
    kpe_sin = proj[:, o2 + LANES:o3] * k_sin
    nh = MLA_HEADS * LANES
    for h in range(MLA_HEADS):
        blk = slice(h * LANES, (h + 1) * LANES)
        qb = q_all[:, blk]
        qp = q_all[:, nh + h * LANES:nh + (h + 1) * LANES]
        qr = _inv_rms_padded_block(qb, MLA_QK) * (LOG2E * MLA_QK ** -0.5)
        qc_ref[:, blk] = ((qb * q_cos + qp * q_sin) * qr).astype(BF16)
        kb = kn_all[:, blk] + kpe
        kc_ref[:, blk] = ((kb * k_cos + kpe_sin) * _inv_rms_padded_block(kb, MLA_QK)).astype(BF16)
    acos, asin = acos_ref[...], asin_ref[...]
    q_cos, q_sin = gqg_ref[0:1] * acos, gqg_ref[1:2] * asin
    for j in range(nq // LANES):
        blk = slice(j * LANES, (j + 1) * LANES)
        qb = proj[:, o3 + j * LANES:o3 + (j + 1) * LANES]
        qp = proj[:, o3 + nq + j * LANES:o3 + nq + (j + 1) * LANES]
        qr = _inv_rms_half_blocks(qb) * (LOG2E * GQA_DIM ** -0.5)
        qd_ref[:, blk] = ((qb * q_cos + qp * q_sin) * qr).astype(BF16)
    kb = proj[:, o4:o4 + LANES]
    kp = proj[:, o4 + LANES:o5]
    kd_ref[...] = ((kb * (gkg_ref[0:1] * acos) + kp * (gkg_ref[1:2] * asin))
                   * _inv_rms_half_blocks(kb)).astype(BF16)
    vd_ref[...] = proj[:, o5:].astype(BF16)


def _pad_heads(w, heads, width):
    r = w.shape[0]
    w = w.reshape(r, heads, width)
    return jnp.pad(w, ((0, 0), (0, 0), (0, LANES - width))).reshape(r, heads * LANES)


def _rope_partner_cols(w, heads):
    r = w.shape[0]
    w = w.reshape(r, heads, MLA_QK)
    rope = _swap_pairs(w[:, :, MLA_NOPE:], MLA_ROPE)
    return jnp.concatenate([jnp.zeros_like(w[:, :, :MLA_NOPE]), rope], axis=2).reshape(r, heads * MLA_QK)


def _odd_proj_call(x, seq, gain, w_in, cq_norm, ckv_norm, w_uq, w_ukv, mq_norm, mk_norm,
                   gq_norm, gk_norm, mla_tables, axial_tables):
    t = x.shape[0]
    tm = PROJ_TM
    row = lambda i: (i, 0)
    pos = lambda i: (i % (seq // tm), 0)
    o1 = MLA_Q_RANK
    o2 = o1 + MLA_KV_RANK
    o3 = o2 + MLA_ROPE
    o4 = o3 + GQA_Q_HEADS * GQA_DIM
    o5 = o4 + GQA_KV_HEADS * GQA_DIM
    place_rope = lambda c: jnp.pad(c, ((0, 0), (MLA_NOPE, LANES - MLA_QK)))
    kpe_cols = w_in[:, o2:o3]
    gq_cols = w_in[:, o3:o4].reshape(D_MODEL, GQA_KV_HEADS, GQA_GROUP, GQA_DIM)
    gq_cols = gq_cols.transpose(0, 2, 1, 3).reshape(D_MODEL, GQA_Q_HEADS * GQA_DIM)
    gk_cols = w_in[:, o4:o5]
    half = GQA_DIM // 2
    win = jnp.concatenate(
        [w_in[:, :o2], place_rope(kpe_cols), place_rope(_swap_pairs(kpe_cols, MLA_ROPE)),
         gq_cols, _swap_pairs(gq_cols, half), gk_cols, _swap_pairs(gk_cols, half), w_in[:, o5:]],
        axis=1).astype(BF16)
    wuq = jnp.concatenate([_pad_heads(w_uq, MLA_HEADS, MLA_QK),
                           _pad_heads(_rope_partner_cols(w_uq, MLA_HEADS), MLA_HEADS, MLA_QK)],
                          axis=1).astype(BF16)
    w_ukv = w_ukv.reshape(MLA_KV_RANK, MLA_HEADS, MLA_NOPE + MLA_V)
    wuk = _pad_heads(w_ukv[:, :, :MLA_NOPE].reshape(MLA_KV_RANK, -1), MLA_HEADS, MLA_NOPE).astype(BF16)
    wuv = w_ukv[:, :, MLA_NOPE:].reshape(MLA_KV_RANK, MLA_HEADS * MLA_V).astype(BF16)

    def mla_gain_pair(g):
        partner = jnp.concatenate([g[:MLA_NOPE], _swap_pairs(g[MLA_NOPE:], MLA_ROPE)])
        return jnp.pad(jnp.stack([g, partner]), ((0, 0), (0, LANES - MLA_QK)))

    args = [x, gain.reshape(1, D_MODEL), win, cq_norm.reshape(1, -1), ckv_norm.reshape(1, -1),
            wuq, wuk, wuv, mla_gain_pair(mq_norm), mla_gain_pair(mk_norm),
            _gain_pair(gq_norm, half), _gain_pair(gk_norm, half), *mla_tables, *axial_tables]
    in_specs = ([pl.BlockSpec((tm, D_MODEL), row)]
                + [_const_spec(a.shape) for a in args[1:12]]
                + [pl.BlockSpec((tm, LANES), pos)] * 4)
    widths = [MLA_HEADS * LANES, MLA_HEADS * LANES, MLA_HEADS * MLA_V,
              GQA_Q_HEADS * GQA_DIM, LANES, LANES]
    return pl.pallas_call(
        _odd_proj_kernel, grid=(t // tm,), in_specs=in_specs,
        out_specs=[pl.BlockSpec((tm, n), row) for n in widths],
        out_shape=[jax.ShapeDtypeStruct((t, n), BF16) for n in widths],
        compiler_params=_params(("parallel",)), name="odd_proj",
    )(*args)


def _attention_kernel(*refs, mode, lam_init):
    if mode == "diff":
        q_ref, k_ref, v_ref, lam_ref, sub_ref, o_ref = refs[:6]
    else:
        q_ref, k_ref, v_ref, o_ref = refs[:4]
    qt_ref, s_buf, m_ref, acc_ref = refs[-4:]
    tq = q_ref.shape[0]
    seq = k_ref.shape[0]
    tk = ATT_TK
    cw = ATT_CW
    n = seq // tk
    q = q_ref[...].astype(F32)
    if mode == "mla":
        qt_ref[:, :tq] = q[:, :LANES].T.astype(BF16)
        qt_ref[:, tq:] = q[:, LANES:].T.astype(BF16)
    else:
        qt = q.T
        first = lax.broadcasted_iota(jnp.int32, qt.shape, 0) < (LANES // 2)
        qt_ref[:, :tq] = jnp.where(first, qt, 0.0).astype(BF16)
        qt_ref[:, tq:] = jnp.where(first, 0.0, qt).astype(BF16)

    chunks = [slice(c * cw, (c + 1) * cw) for c in range(2 * tq // cw)]

    def tile_rows(t):
        return pl.ds(pl.multiple_of(t * tk, tk), tk)

    def scores(kt, slot, cols):
        kc = kt
        if mode == "mla":
            kc = kt[:, :LANES] if cols.start < tq else kt[:, LANES:]
        s_buf[slot, :, cols] = _dot(kc, qt_ref[:, cols])

    def softmax_values(vt, slot, cols):
        s = s_buf[slot, :, cols]
        m_old = m_ref[:, cols]
        m_new = jnp.maximum(m_old, jnp.max(s, axis=0, keepdims=True))
        alpha = jnp.exp2(m_old - m_new)
        p = jnp.exp2(s - m_new).astype(BF16)
        m_ref[:, cols] = m_new
        acc_ref[:, cols] = alpha * acc_ref[:, cols] + _dot(vt, p)

    def value_rows(t):
        ones = jnp.ones((SUM_ROWS, tk), BF16)
        vt = v_ref[tile_rows(t), :].T
        if mode == "diff":
            both = jnp.concatenate([vt, ones], axis=0)
            return both, both
        half = LANES // 2
        return (jnp.concatenate([vt[:half], ones], axis=0), jnp.concatenate([vt[half:], ones], axis=0))

    def half_step(t, t_next, cur, nxt):
        kt = k_ref[tile_rows(t_next), :]
        vt = value_rows(t)
        for cols in chunks:
            scores(kt, nxt, cols)
            softmax_values(vt[0] if cols.start < tq else vt[1], cur, cols)

    m_ref[...] = jnp.full(m_ref.shape, -jnp.inf, F32)
    acc_ref[...] = jnp.zeros(acc_ref.shape, F32)
    kt0 = k_ref[tile_rows(0), :]
    for cols in chunks:
        scores(kt0, 0, cols)

    def body(i, carry):
        for u in range(ATT_PAIRS):
            t = 2 * (i * ATT_PAIRS + u)
            half_step(t, t + 1, 0, 1)
            half_step(t + 1, jnp.minimum(t + 2, n - 1), 1, 0)
        return carry

    lax.fori_loop(0, n // (2 * ATT_PAIRS), body, 0)

    nv = acc_ref.shape[0] - SUM_ROWS
    o = acc_ref[:nv, :] / acc_ref[nv:nv + 1, :]
    if mode == "diff":
        lv = lam_ref[...]
        lam = (jnp.exp(jnp.sum(lv[0:1] * lv[1:2], axis=-1, keepdims=True))
               - jnp.exp(jnp.sum(lv[2:3] * lv[3:4], axis=-1, keepdims=True)) + lam_init)
        ot = o[:, :tq] - lam * o[:, tq:]
        ms = jnp.mean(ot * ot, axis=0, keepdims=True)
        ot = ot * lax.rsqrt(ms + EPS) * (sub_ref[...] * (1.0 - lam_init))
        o_ref[...] = ot.T.astype(o_ref.dtype)
    else:
        ot = jnp.concatenate([o[:, :tq], o[:, tq:]], axis=0).T
        o_ref[...] = ot.astype(o_ref.dtype)


def _attention_call(q, k, v, mode, extra=(), lam_init=0.0):
    b, seq, _ = q.shape
    qw = 2 * LANES if mode == "mla" else LANES
    nblk = q.shape[2] // qw
    tq, tk = ATT_TQ, ATT_TK
    assert seq % (2 * ATT_PAIRS * tk) == 0 and seq % tq == 0 and tq % ATT_CW == 0
    shared = mode == "gqa"
    value_rows = LANES if mode == "diff" else LANES // 2
    kv_idx = (lambda bi, j, i: (bi, 0, 0)) if shared else (lambda bi, j, i: (bi, 0, j))
    in_specs = [pl.BlockSpec((None, tq, qw), lambda bi, j, i: (bi, i, j)),
                pl.BlockSpec((None, seq, qw), kv_idx),
                pl.BlockSpec((None, seq, LANES), kv_idx)]
    in_specs += [_const_spec(e.shape) for e in extra]
    return pl.pallas_call(
        functools.partial(_attention_kernel, mode=mode, lam_init=lam_init),
        grid=(b, nblk, seq // tq), in_specs=in_specs,
        out_specs=pl.BlockSpec((None, tq, LANES), lambda bi, j, i: (bi, i, j)),
        out_shape=jax.ShapeDtypeStruct((b, seq, nblk * LANES), BF16),
        scratch_shapes=[pltpu.VMEM((LANES, 2 * tq), BF16),
                        pltpu.VMEM((2, tk, 2 * tq), F32),
                        pltpu.VMEM((1, 2 * tq), F32),
                        pltpu.VMEM((value_rows + SUM_ROWS, 2 * tq), F32)],
        compiler_params=_params(("parallel", "parallel", "arbitrary")),
        name="attn_" + mode,
    )(q, k, v, *extra)


def _rope_tables(seq):
    pos = jnp.arange(seq, dtype=jnp.int32)
    zeros = lambda n: jnp.zeros((seq, n), F32)
    inv = ROPE_THETA ** (-jnp.arange(0, DIFF_D, 2, dtype=F32) / DIFF_D)
    ang = pos.astype(F32)[:, None] * inv[None, :]
    c, s = lax.optimization_barrier((jnp.cos(ang), jnp.sin(ang)))
    full = (jnp.tile(jnp.concatenate([c, c], 1), (1, 2)), jnp.tile(jnp.concatenate([-s, s], 1), (1, 2)))
    c, s = c[:, ::2], s[:, ::2]
    tail = zeros(LANES - MLA_QK)
    mla = (jnp.concatenate([jnp.ones((seq, MLA_NOPE), F32), c, c, tail], 1),
           jnp.concatenate([zeros(MLA_NOPE), -s, s, tail], 1))
    rows = seq // GRID_W
    cr, sr = jnp.repeat(c[:rows], GRID_W, axis=0), jnp.repeat(s[:rows], GRID_W, axis=0)
    cc, sc = jnp.tile(c[:GRID_W], (rows, 1)), jnp.tile(s[:GRID_W], (rows, 1))
    axial = (jnp.tile(jnp.concatenate([cr, cr, cc, cc], 1), (1, 2)),
             jnp.tile(jnp.concatenate([-sr, sr, -sc, sc], 1), (1, 2)))
    return full, mla, axial


def kernel(x, ffn1_norm, ffn1_w_gu, ffn1_w_down, ffn2_norm, ffn2_w_gu, ffn2_w_down, ev_norm, ev_w_in, ev_sgu_norm, ev_w_s, ev_b_s, ev_q_norm, ev_k_norm, ev_lam_q1, ev_lam_k1, ev_lam_q2, ev_lam_k2, ev_sub_norm, ev_w_out, od_norm, od_w_in, od_cq_norm, od_ckv_norm, od_w_uq, od_w_ukv, od_mla_q_norm, od_mla_k_norm, od_gqa_q_norm, od_gqa_k_norm, od_w_out):
    b, seq, d = x.shape
    t = b * seq
    full_tab, mla_tab, axial_tab = _rope_tables(seq)
    x2 = x.reshape(t, d)
    ffn1 = (ffn1_norm[:, None, :], ffn1_w_gu.astype(BF16), ffn1_w_down.astype(BF16))
    ffn2 = (ffn2_norm[:, None, :], ffn2_w_gu.astype(BF16), ffn2_w_down.astype(BF16))

    x2 = _ffn_call(x2, 0, *ffn1)
    out_a, q, k, v = _even_proj_call(x2, seq, ev_norm[0], ev_w_in[0], ev_sgu_norm[0], ev_w_s[0],
                                     ev_b_s[0], ev_q_norm[0], ev_k_norm[0], full_tab)
    lam_init = 0.8 - 0.6 * float(np.exp(-0.3 * 0))
    lam_vecs = jnp.stack([ev_lam_q1[0], ev_lam_k1[0], ev_lam_q2[0], ev_lam_k2[0]])
    shp = (b, seq, GMLP_WIDTH)
    out_b = _attention_call(q.reshape(shp), k.reshape(shp), v.reshape(shp), "diff",
                            extra=(lam_vecs, ev_sub_norm[0].reshape(DIFF_V, 1)), lam_init=lam_init)
    w_out = ev_w_out[0].astype(BF16)
    x2 = _ffn_call(x2, 0, *ffn2,
                   mix=(out_a, out_b.reshape(t, -1), w_out[:GMLP_WIDTH], w_out[GMLP_WIDTH:]))

    x2 = _ffn_call(x2, 1, *ffn1)
    qc, kc, vc, qd, kd, vd = _odd_proj_call(
        x2, seq, od_norm[0], od_w_in[0], od_cq_norm[0], od_ckv_norm[0], od_w_uq[0], od_w_ukv[0],
        od_mla_q_norm[0], od_mla_k_norm[0], od_gqa_q_norm[0], od_gqa_k_norm[0], mla_tab, axial_tab)
    r3 = lambda a: a.reshape(b, seq, a.shape[1])
    out_c = _attention_call(r3(qc), r3(kc), r3(vc), "mla")
    out_d = _attention_call(r3(qd), r3(kd), r3(vd), "gqa")
    w_out = od_w_out[0].astype(BF16)
    n_c = MLA_HEADS * MLA_V
    w_d = w_out[n_c:].reshape(GQA_KV_HEADS, GQA_GROUP, GQA_DIM, d).transpose(1, 0, 2, 3)
    x2 = _ffn_call(x2, 1, *ffn2,
                   mix=(out_c.reshape(t, -1), out_d.reshape(t, -1), w_out[:n_c],
                        w_d.reshape(GQA_Q_HEADS * GQA_DIM, d)))
    return x2.reshape(b, seq, d)
```

```python
import functools
import math

import numpy as np
import jax
import jax.numpy as jnp
from jax import lax
from jax.experimental import pallas as pl
from jax.experimental.pallas import tpu as pltpu

D_MODEL = 1024
D_FF = 2816
ROPE_THETA = 10000.0
GRID_W = 64
EPS = 1e-6
GMLP_GROUPS = 8
GMLP_GROUP_DIM = 64
GMLP_CHUNK = 128
GMLP_WIDTH = GMLP_GROUPS * GMLP_GROUP_DIM
DIFF_HEADS = 4
DIFF_D = 64
DIFF_V = 128
MLA_HEADS = 8
MLA_Q_RANK = 256
MLA_KV_RANK = 128
MLA_NOPE = 64
MLA_ROPE = 32
MLA_V = 64
MLA_QK = MLA_NOPE + MLA_ROPE
GQA_Q_HEADS = 8
GQA_KV_HEADS = 2
GQA_GROUP = GQA_Q_HEADS // GQA_KV_HEADS
GQA_DIM = 64

LANES = 128
VMEM_LIMIT_BYTES = 56 * 1024 * 1024

FFN_TM = 512
PROJ_TM = 512
ATT_TQ = 1024
ATT_TK = 256
ATT_CW = 512
ATT_PAIRS = 8
SUM_ROWS = 16
LOG2E = math.log2(math.e)

BF16 = jnp.bfloat16
F32 = jnp.float32


def _params(semantics):
    return pltpu.CompilerParams(dimension_semantics=semantics,
                                vmem_limit_bytes=VMEM_LIMIT_BYTES)


def _const_spec(shape):
    nd = len(shape)
    return pl.BlockSpec(shape, lambda *_: (0,) * nd, pipeline_mode=pl.Buffered(1))


def _rms_rows(x, gain):
    ms = jnp.mean(x * x, axis=-1, keepdims=True)
    return x * lax.rsqrt(ms + EPS) * gain


def _dot(a, b):
    return jnp.dot(a, b, preferred_element_type=F32)


def _lane_is_low(shape):
    return lax.broadcasted_iota(jnp.int32, shape, len(shape) - 1) < (LANES // 2)


def _inv_rms_half_blocks(x):
    low = _lane_is_low(x.shape)
    x2 = x * x
    x2_lo = jnp.where(low, x2, 0.0)
    x2_hi = x2 - x2_lo
    ms_lo = jnp.sum(x2_lo, axis=-1, keepdims=True) * (2.0 / LANES)
    ms_hi = jnp.sum(x2_hi, axis=-1, keepdims=True) * (2.0 / LANES)
    return jnp.where(low, lax.rsqrt(ms_lo + EPS), lax.rsqrt(ms_hi + EPS))


def _inv_rms_padded_block(x, width):
    ms = jnp.sum(x * x, axis=-1, keepdims=True) * (1.0 / width)
    return lax.rsqrt(ms + EPS)


def _swap_pairs(a, group):
    shape = a.shape
    a = a.reshape(shape[:-1] + (shape[-1] // group, 2, group // 2))
    return jnp.flip(a, axis=-2).reshape(shape)


def _ffn_body(x, g_ref, wgu_ref, wd_ref, o_ref):
    xn = _rms_rows(x, g_ref[...]).astype(BF16)
    h = _dot(xn, wgu_ref[...])
    gate = h[:, :D_FF]
    up = h[:, D_FF:]
    act = (gate / (1.0 + jnp.exp(-gate)) * up).astype(BF16)
    o_ref[...] = x + 0.5 * _dot(act, wd_ref[...])


def _ffn_kernel(x_ref, g_ref, wgu_ref, wd_ref, o_ref):
    _ffn_body(x_ref[...], g_ref, wgu_ref, wd_ref, o_ref)


def _mix_ffn_kernel(x_ref, a_ref, b_ref, wa_ref, wb_ref, g_ref, wgu_ref, wd_ref, o_ref):
    x = x_ref[...] + _dot(a_ref[...], wa_ref[...]) + _dot(b_ref[...], wb_ref[...])
    _ffn_body(x, g_ref, wgu_ref, wd_ref, o_ref)


def _layer_spec(shape, layer):
    nd = len(shape) - 1
    return pl.BlockSpec((None,) + tuple(shape[1:]), lambda *_: (layer,) + (0,) * nd,
                        pipeline_mode=pl.Buffered(1))


def _ffn_call(x, layer, gains, w_gu, w_down, mix=None):
    t = x.shape[0]
    tm = FFN_TM
    row = lambda i: (i, 0)
    x_spec = pl.BlockSpec((tm, D_MODEL), row)
    w_args = [gains, w_gu, w_down]
    w_specs = [_layer_spec(a.shape, layer) for a in w_args]
    if mix is None:
        kern, in_specs, args = _ffn_kernel, [x_spec] + w_specs, [x] + w_args
    else:
        a, b, wa, wb = mix
        kern = _mix_ffn_kernel
        in_specs = [x_spec, pl.BlockSpec((tm, a.shape[1]), row), pl.BlockSpec((tm, b.shape[1]), row),
                    _const_spec(wa.shape), _const_spec(wb.shape)] + w_specs
        args = [x, a, b, wa, wb] + w_args
    return pl.pallas_call(
        kern, grid=(t // tm,), in_specs=in_specs,
        out_specs=pl.BlockSpec((tm, D_MODEL), row),
        out_shape=jax.ShapeDtypeStruct((t, D_MODEL), F32),
        compiler_params=_params(("parallel",)),
        name="ffn" if mix is None else "mix_ffn",
    )(*args)


def _gelu_tanh(x):
    c = math.sqrt(2.0 / math.pi)
    return 0.5 * x * (1.0 + jnp.tanh(c * (x + 0.044715 * (x * x * x))))


def _even_proj_kernel(x_ref, g_ref, win_ref, sgu_ref, wpair_ref, bias_ref, qg_ref, kg_ref,
                      cos_ref, sin_ref, oa_ref, q_ref, k_ref, v_ref):
    xn = _rms_rows(x_ref[...], g_ref[...]).astype(BF16)
    proj = _dot(xn, win_ref[...])
    w = GMLP_WIDTH
    nblk = w // LANES
    tm = proj.shape[0]
    for j in range(nblk):
        u = _gelu_tanh(proj[:, j * LANES:(j + 1) * LANES])
        v = _gelu_tanh(proj[:, w + j * LANES:w + (j + 1) * LANES])
        vn = v * _inv_rms_half_blocks(v) * sgu_ref[:, j * LANES:(j + 1) * LANES]
        low = _lane_is_low(vn.shape)
        vn_lo = jnp.where(low, vn, 0.0).astype(BF16)
        vn_hi = jnp.where(low, 0.0, vn).astype(BF16)
        wp = wpair_ref[j]
        bias = bias_ref[:, j * LANES:(j + 1) * LANES]
        for c in range(tm // GMLP_CHUNK):
            rows = slice(c * GMLP_CHUNK, (c + 1) * GMLP_CHUNK)
            stacked = jnp.concatenate([vn_lo[rows], vn_hi[rows]], axis=0)
            mixed = _dot(wp, stacked) + bias
            oa_ref[rows, j * LANES:(j + 1) * LANES] = (u[rows] * mixed).astype(BF16)
    cos, sin = cos_ref[...], sin_ref[...]
    q_cos, q_sin = qg_ref[0:1] * cos, qg_ref[1:2] * sin
    k_cos, k_sin = kg_ref[0:1] * cos, kg_ref[1:2] * sin
    for j in range(nblk):
        blk = slice(j * LANES, (j + 1) * LANES)
        qb = proj[:, 2 * w + j * LANES:2 * w + (j + 1) * LANES]
        kb = proj[:, 3 * w + j * LANES:3 * w + (j + 1) * LANES]
        qp = proj[:, 5 * w + j * LANES:5 * w + (j + 1) * LANES]
        kp = proj[:, 6 * w + j * LANES:6 * w + (j + 1) * LANES]
        qr = _inv_rms_half_blocks(qb) * (LOG2E * DIFF_D ** -0.5)
        q_ref[:, blk] = ((qb * q_cos + qp * q_sin) * qr).astype(BF16)
        k_ref[:, blk] = ((kb * k_cos + kp * k_sin) * _inv_rms_half_blocks(kb)).astype(BF16)
    v_ref[...] = proj[:, 4 * w:5 * w].astype(BF16)


def _gain_pair(g, group):
    return jnp.stack([jnp.tile(g, 2), jnp.tile(_swap_pairs(g, group), 2)])


def _even_proj_call(x, seq, gain, w_in, sgu_norm, w_s, b_s, q_norm, k_norm, tables):
    t = x.shape[0]
    tm = PROJ_TM
    w = GMLP_WIDTH
    row = lambda i: (i, 0)
    pos = lambda i: (i % (seq // tm), 0)
    wpair = jnp.concatenate([w_s[0::2], w_s[1::2]], axis=2).astype(BF16)
    bias = jnp.repeat(b_s.T, GMLP_GROUP_DIM, axis=1)
    qk_cols = w_in[:, 2 * w:4 * w]
    win = jnp.concatenate([w_in, _swap_pairs(qk_cols, DIFF_D)], axis=1).astype(BF16)
    args = [x, gain.reshape(1, D_MODEL), win, sgu_norm.reshape(1, w), wpair, bias,
            _gain_pair(q_norm, DIFF_D), _gain_pair(k_norm, DIFF_D), *tables]
    in_specs = [pl.BlockSpec((tm, D_MODEL), row)] + [_const_spec(a.shape) for a in args[1:8]]
    in_specs += [pl.BlockSpec((tm, LANES), pos)] * 2
    out = jax.ShapeDtypeStruct((t, w), BF16)
    return pl.pallas_call(
        _even_proj_kernel, grid=(t // tm,), in_specs=in_specs,
        out_specs=[pl.BlockSpec((tm, w), row)] * 4, out_shape=[out] * 4,
        compiler_params=_params(("parallel",)), name="even_proj",
    )(*args)


def _odd_proj_kernel(x_ref, g_ref, win_ref, cqg_ref, ckvg_ref, wuq_ref, wuk_ref, wuv_ref,
                     mqg_ref, mkg_ref, gqg_ref, gkg_ref, mcos_ref, msin_ref, acos_ref, asin_ref,
                     qc_ref, kc_ref, vc_ref, qd_ref, kd_ref, vd_ref):
    xn = _rms_rows(x_ref[...], g_ref[...]).astype(BF16)
    proj = _dot(xn, win_ref[...])
    nq = GQA_Q_HEADS * GQA_DIM
    o1 = MLA_Q_RANK
    o2 = o1 + MLA_KV_RANK
    o3 = o2 + 2 * LANES
    o4 = o3 + 2 * nq
    o5 = o4 + 2 * LANES
    mcos, msin = mcos_ref[...], msin_ref[...]
    q_cos, q_sin = mqg_ref[0:1] * mcos, mqg_ref[1:2] * msin
    k_cos, k_sin = mkg_ref[0:1] * mcos, mkg_ref[1:2] * msin
    cq = _rms_rows(proj[:, :o1], cqg_ref[...]).astype(BF16)
    q_all = _dot(cq, wuq_ref[...])
    ckv = _rms_rows(proj[:, o1:o2], ckvg_ref[...]).astype(BF16)
    kn_all = _dot(ckv, wuk_ref[...])
    vc_ref[...] = _dot(ckv, wuv_ref[...]).astype(BF16)
    kpe = proj[:, o2:o2 + LANES]
    kpe_sin = proj[:, o2 + LANES:o3] * k_sin
    nh = MLA_HEADS * LANES
    for h in range(MLA_HEADS):
        blk = slice(h * LANES, (h + 1) * LANES)
        qb = q_all[:, blk]
        qp = q_all[:, nh + h * LANES:nh + (h + 1) * LANES]
        qr = _inv_rms_padded_block(qb, MLA_QK) * (LOG2E * MLA_QK ** -0.5)
        qc_ref[:, blk] = ((qb * q_cos + qp * q_sin) * qr).astype(BF16)
        kb = kn_all[:, blk] + kpe
        kc_ref[:, blk] = ((kb * k_cos + kpe_sin) * _inv_rms_padded_block(kb, MLA_QK)).astype(BF16)
    acos, asin = acos_ref[...], asin_ref[...]
    q_cos, q_sin = gqg_ref[0:1] * acos, gqg_ref[1:2] * asin
    for j in range(nq // LANES):
        blk = slice(j * LANES, (j + 1) * LANES)
        qb = proj[:, o3 + j * LANES:o3 + (j + 1) * LANES]
        qp = proj[:, o3 + nq + j * LANES:o3 + nq + (j + 1) * LANES]
        qr = _inv_rms_half_blocks(qb) * (LOG2E * GQA_DIM ** -0.5)
        qd_ref[:, blk] = ((qb * q_cos + qp * q_sin) * qr).astype(BF16)
    kb = proj[:, o4:o4 + LANES]
    kp = proj[:, o4 + LANES:o5]
    kd_ref[...] = ((kb * (gkg_ref[0:1] * acos) + kp * (gkg_ref[1:2] * asin))
                   * _inv_rms_half_blocks(kb)).astype(BF16)
    vd_ref[...] = proj[:, o5:].astype(BF16)


def _pad_heads(w, heads, width):
    r = w.shape[0]
    w = w.reshape(r, heads, width)
    return jnp.pad(w, ((0, 0), (0, 0), (0, LANES - width))).reshape(r, heads * LANES)


def _rope_partner_cols(w, heads):
    r = w.shape[0]
    w = w.reshape(r, heads, MLA_QK)
    rope = _swap_pairs(w[:, :, MLA_NOPE:], MLA_ROPE)
    return jnp.concatenate([jnp.zeros_like(w[:, :, :MLA_NOPE]), rope], axis=2).reshape(r, heads * MLA_QK)


def _odd_proj_call(x, seq, gain, w_in, cq_norm, ckv_norm, w_uq, w_ukv, mq_norm, mk_norm,
                   gq_norm, gk_norm, mla_tables, axial_tables):
    t = x.shape[0]
    tm = PROJ_TM
    row = lambda i: (i, 0)
    pos = lambda i: (i % (seq // tm), 0)
    o1 = MLA_Q_RANK
    o2 = o1 + MLA_KV_RANK
    o3 = o2 + MLA_ROPE
    o4 = o3 + GQA_Q_HEADS * GQA_DIM
    o5 = o4 + GQA_KV_HEADS * GQA_DIM
    place_rope = lambda c: jnp.pad(c, ((0, 0), (MLA_NOPE, LANES - MLA_QK)))
    kpe_cols = w_in[:, o2:o3]
    gq_cols = w_in[:, o3:o4].reshape(D_MODEL, GQA_KV_HEADS, GQA_GROUP, GQA_DIM)
    gq_cols = gq_cols.transpose(0, 2, 1, 3).reshape(D_MODEL, GQA_Q_HEADS * GQA_DIM)
    gk_cols = w_in[:, o4:o5]
    half = GQA_DIM // 2
    win = jnp.concatenate(
        [w_in[:, :o2], place_rope(kpe_cols), place_rope(_swap_pairs(kpe_cols, MLA_ROPE)),
         gq_cols, _swap_pairs(gq_cols, half), gk_cols, _swap_pairs(gk_cols, half), w_in[:, o5:]],
        axis=1).astype(BF16)
    wuq = jnp.concatenate([_pad_heads(w_uq, MLA_HEADS, MLA_QK),
                           _pad_heads(_rope_partner_cols(w_uq, MLA_HEADS), MLA_HEADS, MLA_QK)],
                          axis=1).astype(BF16)
    w_ukv = w_ukv.reshape(MLA_KV_RANK, MLA_HEADS, MLA_NOPE + MLA_V)
    wuk = _pad_heads(w_ukv[:, :, :MLA_NOPE].reshape(MLA_KV_RANK, -1), MLA_HEADS, MLA_NOPE).astype(BF16)
    wuv = w_ukv[:, :, MLA_NOPE:].reshape(MLA_KV_RANK, MLA_HEADS * MLA_V).astype(BF16)

    def mla_gain_pair(g):
        partner = jnp.concatenate([g[:MLA_NOPE], _swap_pairs(g[MLA_NOPE:], MLA_ROPE)])
        return jnp.pad(jnp.stack([g, partner]), ((0, 0), (0, LANES - MLA_QK)))

    args = [x, gain.reshape(1, D_MODEL), win, cq_norm.reshape(1, -1), ckv_norm.reshape(1, -1),
            wuq, wuk, wuv, mla_gain_pair(mq_norm), mla_gain_pair(mk_norm),
            _gain_pair(gq_norm, half), _gain_pair(gk_norm, half), *mla_tables, *axial_tables]
    in_specs = ([pl.BlockSpec((tm, D_MODEL), row)]
                + [_const_spec(a.shape) for a in args[1:12]]
                + [pl.BlockSpec((tm, LANES), pos)] * 4)
    widths = [MLA_HEADS * LANES, MLA_HEADS * LANES, MLA_HEADS * MLA_V,
              GQA_Q_HEADS * GQA_DIM, LANES, LANES]
    return pl.pallas_call(
        _odd_proj_kernel, grid=(t // tm,), in_specs=in_specs,
        out_specs=[pl.BlockSpec((tm, n), row) for n in widths],
        out_shape=[jax.ShapeDtypeStruct((t, n), BF16) for n in widths],
        compiler_params=_params(("parallel",)), name="odd_proj",
    )(*args)


def _attention_kernel(*refs, mode, lam_init):
    if mode == "diff":
        q_ref, k_ref, v_ref, lam_ref, sub_ref, o_ref = refs[:6]
    else:
        q_ref, k_ref, v_ref, o_ref = refs[:4]
    qt_ref, s_buf, m_ref, acc_ref = refs[-4:]
    tq = q_ref.shape[0]
    seq = k_ref.shape[0]
    tk = ATT_TK
    cw = ATT_CW
    n = seq // tk
    q = q_ref[...].astype(F32)
    if mode == "mla":
        qt_ref[:, :tq] = q[:, :LANES].T.astype(BF16)
        qt_ref[:, tq:] = q[:, LANES:].T.astype(BF16)
    else:
        qt = q.T
        first = lax.broadcasted_iota(jnp.int32, qt.shape, 0) < (LANES // 2)
        qt_ref[:, :tq] = jnp.where(first, qt, 0.0).astype(BF16)
        qt_ref[:, tq:] = jnp.where(first, 0.0, qt).astype(BF16)

    chunks = [slice(c * cw, (c + 1) * cw) for c in range(2 * tq // cw)]

    def tile_rows(t):
        return pl.ds(pl.multiple_of(t * tk, tk), tk)

    def scores(kt, slot, cols):
        kc = kt
        if mode == "mla":
            kc = kt[:, :LANES] if cols.start < tq else kt[:, LANES:]
        s_buf[slot, :, cols] = _dot(kc, qt_ref[:, cols])

    def softmax_values(vt, slot, cols):
        s = s_buf[slot, :, cols]
        m_old = m_ref[:, cols]
        m_new = jnp.maximum(m_old, jnp.max(s, axis=0, keepdims=True))
        alpha = jnp.exp2(m_old - m_new)
        p = jnp.exp2(s - m_new).astype(BF16)
        m_ref[:, cols] = m_new
        acc_ref[:, cols] = alpha * acc_ref[:, cols] + _dot(vt, p)

    def value_rows(t):
        ones = jnp.ones((SUM_ROWS, tk), BF16)
        vt = v_ref[tile_rows(t), :].T
        if mode == "diff":
            both = jnp.concatenate([vt, ones], axis=0)
            return both, both
        half = LANES // 2
        return (jnp.concatenate([vt[:half], ones], axis=0), jnp.concatenate([vt[half:], ones], axis=0))

    def half_step(t, t_next, cur, nxt):
        kt = k_ref[tile_rows(t_next), :]
        vt = value_rows(t)
        for cols in chunks:
            scores(kt, nxt, cols)
            softmax_values(vt[0] if cols.start < tq else vt[1], cur, cols)

    m_ref[...] = jnp.full(m_ref.shape, -jnp.inf, F32)
    acc_ref[...] = jnp.zeros(acc_ref.shape, F32)
    kt0 = k_ref[tile_rows(0), :]
    for cols in chunks:
        scores(kt0, 0, cols)

    def body(i, carry):
        for u in range(ATT_PAIRS):
            t = 2 * (i * ATT_PAIRS + u)
            half_step(t, t + 1, 0, 1)
            half_step(t + 1, jnp.minimum(t + 2, n - 1), 1, 0)
        return carry

    lax.fori_loop(0, n // (2 * ATT_PAIRS), body, 0)

    nv = acc_ref.shape[0] - SUM_ROWS
    o = acc_ref[:nv, :] / acc_ref[nv:nv + 1, :]
    if mode == "diff":
        lv = lam_ref[...]
        lam = (jnp.exp(jnp.sum(lv[0:1] * lv[1:2], axis=-1, keepdims=True))
               - jnp.exp(jnp.sum(lv[2:3] * lv[3:4], axis=-1, keepdims=True)) + lam_init)
        ot = o[:, :tq] - lam * o[:, tq:]
        ms = jnp.mean(ot * ot, axis=0, keepdims=True)
        ot = ot * lax.rsqrt(ms + EPS) * (sub_ref[...] * (1.0 - lam_init))
        o_ref[...] = ot.T.astype(o_ref.dtype)
    else:
        ot = jnp.concatenate([o[:, :tq], o[:, tq:]], axis=0).T
        o_ref[...] = ot.astype(o_ref.dtype)


def _attention_call(q, k, v, mode, extra=(), lam_init=0.0):
    b, seq, _ = q.shape
    qw = 2 * LANES if mode == "mla" else LANES
    nblk = q.shape[2] // qw
    tq, tk = ATT_TQ, ATT_TK
    assert seq % (2 * ATT_PAIRS * tk) == 0 and seq % tq == 0 and tq % ATT_CW == 0
    shared = mode == "gqa"
    value_rows = LANES if mode == "diff" else LANES // 2
    kv_idx = (lambda bi, j, i: (bi, 0, 0)) if shared else (lambda bi, j, i: (bi, 0, j))
    in_specs = [pl.BlockSpec((None, tq, qw), lambda bi, j, i: (bi, i, j)),
                pl.BlockSpec((None, seq, qw), kv_idx),
                pl.BlockSpec((None, seq, LANES), kv_idx)]
    in_specs += [_const_spec(e.shape) for e in extra]
    return pl.pallas_call(
        functools.partial(_attention_kernel, mode=mode, lam_init=lam_init),
        grid=(b, nblk, seq // tq), in_specs=in_specs,
        out_specs=pl.BlockSpec((None, tq, LANES), lambda bi, j, i: (bi, i, j)),
        out_shape=jax.ShapeDtypeStruct((b, seq, nblk * LANES), BF16),
        scratch_shapes=[pltpu.VMEM((LANES, 2 * tq), BF16),
                        pltpu.VMEM((2, tk, 2 * tq), F32),
                        pltpu.VMEM((1, 2 * tq), F32),
                        pltpu.VMEM((value_rows + SUM_ROWS, 2 * tq), F32)],
        compiler_params=_params(("parallel", "parallel", "arbitrary")),
        name="attn_" + mode,
    )(q, k, v, *extra)


def _rope_tables(seq):
    assert MLA_ROPE == GQA_DIM // 2 and seq % GRID_W == 0
    pos = jnp.arange(seq, dtype=jnp.int32)
    zeros = lambda n: jnp.zeros((seq, n), F32)

    def angles(dim):
        inv = ROPE_THETA ** (-jnp.arange(0, dim, 2, dtype=F32) / dim)
        ang = pos.astype(F32)[:, None] * inv[None, :]
        return lax.optimization_barrier((jnp.cos(ang), jnp.sin(ang)))

    c, s = angles(DIFF_D)
    full = (jnp.tile(jnp.concatenate([c, c], 1), (1, 2)), jnp.tile(jnp.concatenate([-s, s], 1), (1, 2)))
    c, s = angles(MLA_ROPE)
    tail = zeros(LANES - MLA_QK)
    mla = (jnp.concatenate([jnp.ones((seq, MLA_NOPE), F32), c, c, tail], 1),
           jnp.concatenate([zeros(MLA_NOPE), -s, s, tail], 1))
    rows = seq // GRID_W
    cr, sr = jnp.repeat(c[:rows], GRID_W, axis=0), jnp.repeat(s[:rows], GRID_W, axis=0)
    cc, sc = jnp.tile(c[:GRID_W], (rows, 1)), jnp.tile(s[:GRID_W], (rows, 1))
    axial = (jnp.tile(jnp.concatenate([cr, cr, cc, cc], 1), (1, 2)),
             jnp.tile(jnp.concatenate([-sr, sr, -sc, sc], 1), (1, 2)))
    return full, mla, axial


def kernel(x, ffn1_norm, ffn1_w_gu, ffn1_w_down, ffn2_norm, ffn2_w_gu, ffn2_w_down, ev_norm, ev_w_in, ev_sgu_norm, ev_w_s, ev_b_s, ev_q_norm, ev_k_norm, ev_lam_q1, ev_lam_k1, ev_lam_q2, ev_lam_k2, ev_sub_norm, ev_w_out, od_norm, od_w_in, od_cq_norm, od_ckv_norm, od_w_uq, od_w_ukv, od_mla_q_norm, od_mla_k_norm, od_gqa_q_norm, od_gqa_k_norm, od_w_out):
    b, seq, d = x.shape
    t = b * seq
    full_tab, mla_tab, axial_tab = _rope_tables(seq)
    x2 = x.reshape(t, d)
    ffn1 = (ffn1_norm[:, None, :], ffn1_w_gu.astype(BF16), ffn1_w_down.astype(BF16))
    ffn2 = (ffn2_norm[:, None, :], ffn2_w_gu.astype(BF16), ffn2_w_down.astype(BF16))

    x2 = _ffn_call(x2, 0, *ffn1)
    out_a, q, k, v = _even_proj_call(x2, seq, ev_norm[0], ev_w_in[0], ev_sgu_norm[0], ev_w_s[0],
                                     ev_b_s[0], ev_q_norm[0], ev_k_norm[0], full_tab)
    lam_init = 0.8 - 0.6 * float(np.exp(-0.3 * 0))
    lam_vecs = jnp.stack([ev_lam_q1[0], ev_lam_k1[0], ev_lam_q2[0], ev_lam_k2[0]])
    shp = (b, seq, GMLP_WIDTH)
    out_b = _attention_call(q.reshape(shp), k.reshape(shp), v.reshape(shp), "diff",
                            extra=(lam_vecs, ev_sub_norm[0].reshape(DIFF_V, 1)), lam_init=lam_init)
    w_out = ev_w_out[0].astype(BF16)
    x2 = _ffn_call(x2, 0, *ffn2,
                   mix=(out_a, out_b.reshape(t, -1), w_out[:GMLP_WIDTH], w_out[GMLP_WIDTH:]))

    x2 = _ffn_call(x2, 1, *ffn1)
    qc, kc, vc, qd, kd, vd = _odd_proj_call(
        x2, seq, od_norm[0], od_w_in[0], od_cq_norm[0], od_ckv_norm[0], od_w_uq[0], od_w_ukv[0],
        od_mla_q_norm[0], od_mla_k_norm[0], od_gqa_q_norm[0], od_gqa_k_norm[0], mla_tab, axial_tab)
    r3 = lambda a: a.reshape(b, seq, a.shape[1])
    out_c = _attention_call(r3(qc), r3(kc), r3(vc), "mla")
    out_d = _attention_call(r3(qd), r3(kd), r3(vd), "gqa")
    w_out = od_w_out[0].astype(BF16)
    n_c = MLA_HEADS * MLA_V
    w_d = w_out[n_c:].reshape(GQA_KV_HEADS, GQA_GROUP, GQA_DIM, d).transpose(1, 0, 2, 3)
    x2 = _ffn_call(x2, 1, *ffn2,
                   mix=(out_c.reshape(t, -1), out_d.reshape(t, -1), w_out[:n_c],
                        w_d.reshape(GQA_Q_HEADS * GQA_DIM, d)))
    return x2.reshape(b, seq, d)
```

```python
import functools
import math

import numpy as np
import jax
import jax.numpy as jnp
from jax import lax
from jax.experimental import pallas as pl
from jax.experimental.pallas import tpu as pltpu

D_MODEL = 1024
D_FF = 2816
ROPE_THETA = 10000.0
GRID_W = 64
EPS = 1e-6
GMLP_GROUPS = 8
GMLP_GROUP_DIM = 64
GMLP_CHUNK = 128
GMLP_WIDTH = GMLP_GROUPS * GMLP_GROUP_DIM
DIFF_HEADS = 4
DIFF_D = 64
DIFF_V = 128
MLA_HEADS = 8
MLA_Q_RANK = 256
MLA_KV_RANK = 128
MLA_NOPE = 64
MLA_ROPE = 32
MLA_V = 64
MLA_QK = MLA_NOPE + MLA_ROPE
GQA_Q_HEADS = 8
GQA_KV_HEADS = 2
GQA_GROUP = GQA_Q_HEADS // GQA_KV_HEADS
GQA_DIM = 64

LANES = 128
VMEM_LIMIT_BYTES = 56 * 1024 * 1024

FFN_TM = 512
PROJ_TM = 512
ATT_TQ = 1024
ATT_TK = 256
ATT_CW = 512
ATT_PAIRS = 5
SUM_ROWS = 16
LOG2E = math.log2(math.e)
SAFE_SCORE_RANGE = 60.0
ROUNDING_SLACK = 1.0 + 2.0 ** -6

BF16 = jnp.bfloat16
F32 = jnp.float32


def _params(semantics):
    return pltpu.CompilerParams(dimension_semantics=semantics,
                                vmem_limit_bytes=VMEM_LIMIT_BYTES)


def _const_spec(shape):
    nd = len(shape)
    return pl.BlockSpec(shape, lambda *_: (0,) * nd, pipeline_mode=pl.Buffered(1))


def _rms_rows(x, gain):
    ms = jnp.mean(x * x, axis=-1, keepdims=True)
    return x * lax.rsqrt(ms + EPS) * gain


def _dot(a, b):
    return jnp.dot(a, b, preferred_element_type=F32)


def _lane_is_low(shape):
    return lax.broadcasted_iota(jnp.int32, shape, len(shape) - 1) < (LANES // 2)


def _inv_rms_half_blocks(x):
    low = _lane_is_low(x.shape)
    x2 = x * x
    x2_lo = jnp.where(low, x2, 0.0)
    x2_hi = x2 - x2_lo
    ms_lo = jnp.sum(x2_lo, axis=-1, keepdims=True) * (2.0 / LANES)
    ms_hi = jnp.sum(x2_hi, axis=-1, keepdims=True) * (2.0 / LANES)
    return jnp.where(low, lax.rsqrt(ms_lo + EPS), lax.rsqrt(ms_hi + EPS))


def _inv_rms_padded_block(x, width):
    ms = jnp.sum(x * x, axis=-1, keepdims=True) * (1.0 / width)
    return lax.rsqrt(ms + EPS)


def _swap_pairs(a, group):
    shape = a.shape
    a = a.reshape(shape[:-1] + (shape[-1] // group, 2, group // 2))
    return jnp.flip(a, axis=-2).reshape(shape)


def _ffn_body(x, g_ref, wgu_ref, wd_ref, o_ref):
    xn = _rms_rows(x, g_ref[...]).astype(BF16)
    h = _dot(xn, wgu_ref[...])
    gate = h[:, :D_FF]
    up = h[:, D_FF:]
    act = (gate / (1.0 + jnp.exp(-gate)) * up).astype(BF16)
    o_ref[...] = x + 0.5 * _dot(act, wd_ref[...])


def _ffn_kernel(x_ref, g_ref, wgu_ref, wd_ref, o_ref):
    _ffn_body(x_ref[...], g_ref, wgu_ref, wd_ref, o_ref)


def _mix_ffn_kernel(x_ref, a_ref, b_ref, wa_ref, wb_ref, g_ref, wgu_ref, wd_ref, o_ref):
    x = x_ref[...] + _dot(a_ref[...], wa_ref[...]) + _dot(b_ref[...], wb_ref[...])
    _ffn_body(x, g_ref, wgu_ref, wd_ref, o_ref)


def _layer_spec(shape, layer):
    nd = len(shape) - 1
    return pl.BlockSpec((None,) + tuple(shape[1:]), lambda *_: (layer,) + (0,) * nd,
                        pipeline_mode=pl.Buffered(1))


def _ffn_call(x, layer, gains, w_gu, w_down, mix=None):
    t = x.shape[0]
    tm = FFN_TM
    row = lambda i: (i, 0)
    x_spec = pl.BlockSpec((tm, D_MODEL), row)
    w_args = [gains, w_gu, w_down]
    w_specs = [_layer_spec(a.shape, layer) for a in w_args]
    if mix is None:
        kern, in_specs, args = _ffn_kernel, [x_spec] + w_specs, [x] + w_args
    else:
        a, b, wa, wb = mix
        kern = _mix_ffn_kernel
        in_specs = [x_spec, pl.BlockSpec((tm, a.shape[1]), row), pl.BlockSpec((tm, b.shape[1]), row),
                    _const_spec(wa.shape), _const_spec(wb.shape)] + w_specs
        args = [x, a, b, wa, wb] + w_args
    return pl.pallas_call(
        kern, grid=(t // tm,), in_specs=in_specs,
        out_specs=pl.BlockSpec((tm, D_MODEL), row),
        out_shape=jax.ShapeDtypeStruct((t, D_MODEL), F32),
        compiler_params=_params(("parallel",)),
        name="ffn" if mix is None else "mix_ffn",
    )(*args)


def _gelu_tanh(x):
    c = math.sqrt(2.0 / math.pi)
    return 0.5 * x * (1.0 + jnp.tanh(c * (x + 0.044715 * (x * x * x))))


def _even_proj_kernel(x_ref, g_ref, win_ref, sgu_ref, wpair_ref, bias_ref, qg_ref, kg_ref,
                      cos_ref, sin_ref, oa_ref, q_ref, k_ref, v_ref):
    xn = _rms_rows(x_ref[...], g_ref[...]).astype(BF16)
    proj = _dot(xn, win_ref[...])
    w = GMLP_WIDTH
    nblk = w // LANES
    tm = proj.shape[0]
    for j in range(nblk):
        u = _gelu_tanh(proj[:, j * LANES:(j + 1) * LANES])
        v = _gelu_tanh(proj[:, w + j * LANES:w + (j + 1) * LANES])
        vn = v * _inv_rms_half_blocks(v) * sgu_ref[:, j * LANES:(j + 1) * LANES]
        low = _lane_is_low(vn.shape)
        vn_lo = jnp.where(low, vn, 0.0).astype(BF16)
        vn_hi = jnp.where(low, 0.0, vn).astype(BF16)
        wp = wpair_ref[j]
        bias = bias_ref[:, j * LANES:(j + 1) * LANES]
        for c in range(tm // GMLP_CHUNK):
            rows = slice(c * GMLP_CHUNK, (c + 1) * GMLP_CHUNK)
            stacked = jnp.concatenate([vn_lo[rows], vn_hi[rows]], axis=0)
            mixed = _dot(wp, stacked) + bias
            oa_ref[rows, j * LANES:(j + 1) * LANES] = (u[rows] * mixed).astype(BF16)
    cos, sin = cos_ref[...], sin_ref[...]
    q_cos, q_sin = qg_ref[0:1] * cos, qg_ref[1:2] * sin
    k_cos, k_sin = kg_ref[0:1] * cos, kg_ref[1:2] * sin
    for j in range(nblk):
        blk = slice(j * LANES, (j + 1) * LANES)
        qb = proj[:, 2 * w + j * LANES:2 * w + (j + 1) * LANES]
        kb = proj[:, 3 * w + j * LANES:3 * w + (j + 1) * LANES]
        qp = proj[:, 5 * w + j * LANES:5 * w + (j + 1) * LANES]
        kp = proj[:, 6 * w + j * LANES:6 * w + (j + 1) * LANES]
        qr = _inv_rms_half_blocks(qb) * (LOG2E * DIFF_D ** -0.5)
        q_ref[:, blk] = ((qb * q_cos + qp * q_sin) * qr).astype(BF16)
        k_ref[:, blk] = ((kb * k_cos + kp * k_sin) * _inv_rms_half_blocks(kb)).astype(BF16)
    v_ref[...] = proj[:, 4 * w:5 * w].astype(BF16)


def _gain_pair(g, group):
    return jnp.stack([jnp.tile(g, 2), jnp.tile(_swap_pairs(g, group), 2)])


def _even_proj_call(x, seq, gain, w_in, sgu_norm, w_s, b_s, q_norm, k_norm, tables):
    t = x.shape[0]
    tm = PROJ_TM
    w = GMLP_WIDTH
    row = lambda i: (i, 0)
    pos = lambda i: (i % (seq // tm), 0)
    wpair = jnp.concatenate([w_s[0::2], w_s[1::2]], axis=2).astype(BF16)
    bias = jnp.repeat(b_s.T, GMLP_GROUP_DIM, axis=1)
    qk_cols = w_in[:, 2 * w:4 * w]
    win = jnp.concatenate([w_in, _swap_pairs(qk_cols, DIFF_D)], axis=1).astype(BF16)
    args = [x, gain.reshape(1, D_MODEL), win, sgu_norm.reshape(1, w), wpair, bias,
            _gain_pair(q_norm, DIFF_D), _gain_pair(k_norm, DIFF_D), *tables]
    in_specs = [pl.BlockSpec((tm, D_MODEL), row)] + [_const_spec(a.shape) for a in args[1:8]]
    in_specs += [pl.BlockSpec((tm, LANES), pos)] * 2
    out = jax.ShapeDtypeStruct((t, w), BF16)
    return pl.pallas_call(
        _even_proj_kernel, grid=(t // tm,), in_specs=in_specs,
        out_specs=[pl.BlockSpec((tm, w), row)] * 4, out_shape=[out] * 4,
        compiler_params=_params(("parallel",)), name="even_proj",
    )(*args)


def _odd_proj_kernel(x_ref, g_ref, win_ref, cqg_ref, ckvg_ref, wuq_ref, wuk_ref, wuv_ref,
                     mqg_ref, mkg_ref, gqg_ref, gkg_ref, mcos_ref, msin_ref, acos_ref, asin_ref,
                     qc_ref, kc_ref, vc_ref, qd_ref, kd_ref, vd_ref):
    xn = _rms_rows(x_ref[...], g_ref[...]).astype(BF16)
    proj = _dot(xn, win_ref[...])
    nq = GQA_Q_HEADS * GQA_DIM
    o1 = MLA_Q_RANK
    o2 = o1 + MLA_KV_RANK
    o3 = o2 + 2 * LANES
    o4 = o3 + 2 * nq
    o5 = o4 + 2 * LANES
    mcos, msin = mcos_ref[...], msin_ref[...]
    q_cos, q_sin = mqg_ref[0:1] * mcos, mqg_ref[1:2] * msin
    k_cos, k_sin = mkg_ref[0:1] * mcos, mkg_ref[1:2] * msin
    cq = _rms_rows(proj[:, :o1], cqg_ref[...]).astype(BF16)
    q_all = _dot(cq, wuq_ref[...])
    ckv = _rms_rows(proj[:, o1:o2], ckvg_ref[...]).astype(BF16)
    kn_all = _dot(ckv, wuk_ref[...])
    vc_ref[...] = _dot(ckv, wuv_ref[...]).astype(BF16)
    kpe = proj[:, o2:o2 + LANES]
    kpe_sin = proj[:, o2 + LANES:o3] * k_sin
    nh = MLA_HEADS * LANES
    for h in range(MLA_HEADS):
        blk = slice(h * LANES, (h + 1) * LANES)
        qb = q_all[:, blk]
        qp = q_all[:, nh + h * LANES:nh + (h + 1) * LANES]
        qr = _inv_rms_padded_block(qb, MLA_QK) * (LOG2E * MLA_QK ** -0.5)
        qc_ref[:, blk] = ((qb * q_cos + qp * q_sin) * qr).astype(BF16)
        kb = kn_all[:, blk] + kpe
        kc_ref[:, blk] = ((kb * k_cos + kpe_sin) * _inv_rms_padded_block(kb, MLA_QK)).astype(BF16)
    acos, asin = acos_ref[...], asin_ref[...]
    q_cos, q_sin = gqg_ref[0:1] * acos, gqg_ref[1:2] * asin
    for j in range(nq // LANES):
        blk = slice(j * LANES, (j + 1) * LANES)
        qb = proj[:, o3 + j * LANES:o3 + (j + 1) * LANES]
        qp = proj[:, o3 + nq + j * LANES:o3 + nq + (j + 1) * LANES]
        qr = _inv_rms_half_blocks(qb) * (LOG2E * GQA_DIM ** -0.5)
        qd_ref[:, blk] = ((qb * q_cos + qp * q_sin) * qr).astype(BF16)
    kb = proj[:, o4:o4 + LANES]
    kp = proj[:, o4 + LANES:o5]
    kd_ref[...] = ((kb * (gkg_ref[0:1] * acos) + kp * (gkg_ref[1:2] * asin))
                   * _inv_rms_half_blocks(kb)).astype(BF16)
    vd_ref[...] = proj[:, o5:].astype(BF16)


def _pad_heads(w, heads, width):
    r = w.shape[0]
    w = w.reshape(r, heads, width)
    return jnp.pad(w, ((0, 0), (0, 0), (0, LANES - width))).reshape(r, heads * LANES)


def _rope_partner_cols(w, heads):
    r = w.shape[0]
    w = w.reshape(r, heads, MLA_QK)
    rope = _swap_pairs(w[:, :, MLA_NOPE:], MLA_ROPE)
    return jnp.concatenate([jnp.zeros_like(w[:, :, :MLA_NOPE]), rope], axis=2).reshape(r, heads * MLA_QK)


def _odd_proj_call(x, seq, gain, w_in, cq_norm, ckv_norm, w_uq, w_ukv, mq_norm, mk_norm,
                   gq_norm, gk_norm, mla_tables, axial_tables):
    t = x.shape[0]
    tm = PROJ_TM
    row = lambda i: (i, 0)
    pos = lambda i: (i % (seq // tm), 0)
    o1 = MLA_Q_RANK
    o2 = o1 + MLA_KV_RANK
    o3 = o2 + MLA_ROPE
    o4 = o3 + GQA_Q_HEADS * GQA_DIM
    o5 = o4 + GQA_KV_HEADS * GQA_DIM
    place_rope = lambda c: jnp.pad(c, ((0, 0), (MLA_NOPE, LANES - MLA_QK)))
    kpe_cols = w_in[:, o2:o3]
    gq_cols = w_in[:, o3:o4].reshape(D_MODEL, GQA_KV_HEADS, GQA_GROUP, GQA_DIM)
    gq_cols = gq_cols.transpose(0, 2, 1, 3).reshape(D_MODEL, GQA_Q_HEADS * GQA_DIM)
    gk_cols = w_in[:, o4:o5]
    half = GQA_DIM // 2
    win = jnp.concatenate(
        [w_in[:, :o2], place_rope(kpe_cols), place_rope(_swap_pairs(kpe_cols, MLA_ROPE)),
         gq_cols, _swap_pairs(gq_cols, half), gk_cols, _swap_pairs(gk_cols, half), w_in[:, o5:]],
        axis=1).astype(BF16)
    wuq = jnp.concatenate([_pad_heads(w_uq, MLA_HEADS, MLA_QK),
                           _pad_heads(_rope_partner_cols(w_uq, MLA_HEADS), MLA_HEADS, MLA_QK)],
                          axis=1).astype(BF16)
    w_ukv = w_ukv.reshape(MLA_KV_RANK, MLA_HEADS, MLA_NOPE + MLA_V)
    wuk = _pad_heads(w_ukv[:, :, :MLA_NOPE].reshape(MLA_KV_RANK, -1), MLA_HEADS, MLA_NOPE).astype(BF16)
    wuv = w_ukv[:, :, MLA_NOPE:].reshape(MLA_KV_RANK, MLA_HEADS * MLA_V).astype(BF16)

    def mla_gain_pair(g):
        partner = jnp.concatenate([g[:MLA_NOPE], _swap_pairs(g[MLA_NOPE:], MLA_ROPE)])
        return jnp.pad(jnp.stack([g, partner]), ((0, 0), (0, LANES - MLA_QK)))

    args = [x, gain.reshape(1, D_MODEL), win, cq_norm.reshape(1, -1), ckv_norm.reshape(1, -1),
            wuq, wuk, wuv, mla_gain_pair(mq_norm), mla_gain_pair(mk_norm),
            _gain_pair(gq_norm, half), _gain_pair(gk_norm, half), *mla_tables, *axial_tables]
    in_specs = ([pl.BlockSpec((tm, D_MODEL), row)]
                + [_const_spec(a.shape) for a in args[1:12]]
                + [pl.BlockSpec((tm, LANES), pos)] * 4)
    widths = [MLA_HEADS * LANES, MLA_HEADS * LANES, MLA_HEADS * MLA_V,
              GQA_Q_HEADS * GQA_DIM, LANES, LANES]
    return pl.pallas_call(
        _odd_proj_kernel, grid=(t // tm,), in_specs=in_specs,
        out_specs=[pl.BlockSpec((tm, n), row) for n in widths],
        out_shape=[jax.ShapeDtypeStruct((t, n), BF16) for n in widths],
        compiler_params=_params(("parallel",)), name="odd_proj",
    )(*args)


def _attention_kernel(*refs, mode, lam_init, stabilised):
    if mode == "diff":
        q_ref, k_ref, v_ref, lam_ref, sub_ref, o_ref = refs[:6]
    else:
        q_ref, k_ref, v_ref, o_ref = refs[:4]
    qt_ref, tile_buf, stat_ref, acc_ref = refs[-4:]
    tq = q_ref.shape[0]
    seq = k_ref.shape[0]
    tk = ATT_TK
    cw = ATT_CW
    n = seq // tk
    q = q_ref[...].astype(F32)
    if mode == "mla":
        qt_ref[:, :tq] = q[:, :LANES].T.astype(BF16)
        qt_ref[:, tq:] = q[:, LANES:].T.astype(BF16)
    else:
        qt = q.T
        first = lax.broadcasted_iota(jnp.int32, qt.shape, 0) < (LANES // 2)
        qt_ref[:, :tq] = jnp.where(first, qt, 0.0).astype(BF16)
        qt_ref[:, tq:] = jnp.where(first, 0.0, qt).astype(BF16)

    chunks = [slice(c * cw, (c + 1) * cw) for c in range(2 * tq // cw)]

    def tile_rows(t):
        return pl.ds(pl.multiple_of(t * tk, tk), tk)

    def ahead(kt, slot, cols):
        kc = kt
        if mode == "mla":
            kc = kt[:, :LANES] if cols.start < tq else kt[:, LANES:]
        s = _dot(kc, qt_ref[:, cols])
        if stabilised:
            tile_buf[slot, :, cols] = s
        else:
            p = jnp.exp2(s)
            stat_ref[:, cols] += jnp.sum(p.reshape(tk // 8, 8, p.shape[1]), axis=0)
            tile_buf[slot, :, cols] = p.astype(BF16)

    def consume(vt, slot, cols):
        if stabilised:
            s = tile_buf[slot, :, cols]
            m_old = stat_ref[:, cols]
            m_new = jnp.maximum(m_old, jnp.max(s, axis=0, keepdims=True))
            alpha = jnp.exp2(m_old - m_new)
            p = jnp.exp2(s - m_new).astype(BF16)
            stat_ref[:, cols] = m_new
            acc_ref[:, cols] = alpha * acc_ref[:, cols] + _dot(vt, p)
        else:
            acc_ref[:, cols] += _dot(vt, tile_buf[slot, :, cols])

    def value_rows(t):
        vt = v_ref[tile_rows(t), :].T
        parts = (vt, vt) if mode == "diff" else (vt[:LANES // 2], vt[LANES // 2:])
        if not stabilised:
            return parts
        ones = jnp.ones((SUM_ROWS, tk), BF16)
        return tuple(jnp.concatenate([part, ones], axis=0) for part in parts)

    def half_step(t, t_next, cur, nxt):
        kt = k_ref[tile_rows(t_next), :]
        vt = value_rows(t)
        for cols in chunks:
            ahead(kt, nxt, cols)
            consume(vt[0] if cols.start < tq else vt[1], cur, cols)

    stat_ref[...] = jnp.full(stat_ref.shape, -jnp.inf if stabilised else 0.0, F32)
    acc_ref[...] = jnp.zeros(acc_ref.shape, F32)
    kt0 = k_ref[tile_rows(0), :]
    for cols in chunks:
        ahead(kt0, 0, cols)

    def body(i, carry):
        for u in range(ATT_PAIRS):
            t = 2 * (i * ATT_PAIRS + u)
            half_step(t, t + 1, 0, 1)
            half_step(t + 1, t + 2, 1, 0)
        return carry

    lax.fori_loop(0, (n - 2) // (2 * ATT_PAIRS), body, 0)
    half_step(n - 2, n - 1, 0, 1)
    vt_last = value_rows(n - 1)
    for cols in chunks:
        consume(vt_last[0] if cols.start < tq else vt_last[1], 1, cols)

    if stabilised:
        nv = acc_ref.shape[0] - SUM_ROWS
        o = acc_ref[:nv, :] / acc_ref[nv:nv + 1, :]
    else:
        o = acc_ref[...] / jnp.sum(stat_ref[...], axis=0, keepdims=True)
    if mode == "diff":
        lv = lam_ref[...]
        lam = (jnp.exp(jnp.sum(lv[0:1] * lv[1:2], axis=-1, keepdims=True))
               - jnp.exp(jnp.sum(lv[2:3] * lv[3:4], axis=-1, keepdims=True)) + lam_init)
        ot = o[:, :tq] - lam * o[:, tq:]
        ms = jnp.mean(ot * ot, axis=0, keepdims=True)
        ot = ot * lax.rsqrt(ms + EPS) * (sub_ref[...] * (1.0 - lam_init))
        o_ref[...] = ot.T.astype(o_ref.dtype)
    else:
        ot = jnp.concatenate([o[:, :tq], o[:, tq:]], axis=0).T
        o_ref[...] = ot.astype(o_ref.dtype)


def _attention_call(q, k, v, mode, score_bound, extra=(), lam_init=0.0):
    b, seq, _ = q.shape
    qw = 2 * LANES if mode == "mla" else LANES
    nblk = q.shape[2] // qw
    tq, tk = ATT_TQ, ATT_TK
    assert seq % tk == 0 and (seq // tk - 2) % (2 * ATT_PAIRS) == 0 and seq % tq == 0 and tq % ATT_CW == 0
    shared = mode == "gqa"
    value_rows = LANES if mode == "diff" else LANES // 2
    kv_idx = (lambda bi, j, i: (bi, 0, 0)) if shared else (lambda bi, j, i: (bi, 0, j))
    in_specs = [pl.BlockSpec((None, tq, qw), lambda bi, j, i: (bi, i, j)),
                pl.BlockSpec((None, seq, qw), kv_idx),
                pl.BlockSpec((None, seq, LANES), kv_idx)]
    in_specs += [_const_spec(e.shape) for e in extra]

    def call(stabilised):
        if stabilised:
            scratch = [pltpu.VMEM((2, tk, 2 * tq), F32), pltpu.VMEM((1, 2 * tq), F32),
                       pltpu.VMEM((value_rows + SUM_ROWS, 2 * tq), F32)]
        else:
            scratch = [pltpu.VMEM((2, tk, 2 * tq), BF16), pltpu.VMEM((8, 2 * tq), F32),
                       pltpu.VMEM((value_rows, 2 * tq), F32)]
        return pl.pallas_call(
            functools.partial(_attention_kernel, mode=mode, lam_init=lam_init, stabilised=stabilised),
            grid=(b, nblk, seq // tq), in_specs=in_specs,
            out_specs=pl.BlockSpec((None, tq, LANES), lambda bi, j, i: (bi, i, j)),
            out_shape=jax.ShapeDtypeStruct((b, seq, nblk * LANES), BF16),
            scratch_shapes=[pltpu.VMEM((LANES, 2 * tq), BF16)] + scratch,
            compiler_params=_params(("parallel", "parallel", "arbitrary")),
            name="attn_" + mode + ("_stab" if stabilised else ""),
        )

    return lax.cond(score_bound <= SAFE_SCORE_RANGE, call(False), call(True), q, k, v, *extra)


def _score_bound(q_gain, k_gain, dim):
    return (jnp.max(jnp.abs(q_gain)) * jnp.max(jnp.abs(k_gain))
            * (dim * dim ** -0.5 * LOG2E * ROUNDING_SLACK))


def _rope_tables(seq):
    assert MLA_ROPE == GQA_DIM // 2 and seq % GRID_W == 0
    pos = jnp.arange(seq, dtype=jnp.int32)
    zeros = lambda n: jnp.zeros((seq, n), F32)

    def angles(dim):
        inv = ROPE_THETA ** (-jnp.arange(0, dim, 2, dtype=F32) / dim)
        ang = pos.astype(F32)[:, None] * inv[None, :]
        return lax.optimization_barrier((jnp.cos(ang), jnp.sin(ang)))

    c, s = angles(DIFF_D)
    full = (jnp.tile(jnp.concatenate([c, c], 1), (1, 2)), jnp.tile(jnp.concatenate([-s, s], 1), (1, 2)))
    c, s = angles(MLA_ROPE)
    tail = zeros(LANES - MLA_QK)
    mla = (jnp.concatenate([jnp.ones((seq, MLA_NOPE), F32), c, c, tail], 1),
           jnp.concatenate([zeros(MLA_NOPE), -s, s, tail], 1))
    rows = seq // GRID_W
    cr, sr = jnp.repeat(c[:rows], GRID_W, axis=0), jnp.repeat(s[:rows], GRID_W, axis=0)
    cc, sc = jnp.tile(c[:GRID_W], (rows, 1)), jnp.tile(s[:GRID_W], (rows, 1))
    axial = (jnp.tile(jnp.concatenate([cr, cr, cc, cc], 1), (1, 2)),
             jnp.tile(jnp.concatenate([-sr, sr, -sc, sc], 1), (1, 2)))
    return full, mla, axial


def kernel(x, ffn1_norm, ffn1_w_gu, ffn1_w_down, ffn2_norm, ffn2_w_gu, ffn2_w_down, ev_norm, ev_w_in, ev_sgu_norm, ev_w_s, ev_b_s, ev_q_norm, ev_k_norm, ev_lam_q1, ev_lam_k1, ev_lam_q2, ev_lam_k2, ev_sub_norm, ev_w_out, od_norm, od_w_in, od_cq_norm, od_ckv_norm, od_w_uq, od_w_ukv, od_mla_q_norm, od_mla_k_norm, od_gqa_q_norm, od_gqa_k_norm, od_w_out):
    b, seq, d = x.shape
    t = b * seq
    full_tab, mla_tab, axial_tab = _rope_tables(seq)
    x2 = x.reshape(t, d)
    ffn1 = (ffn1_norm[:, None, :], ffn1_w_gu.astype(BF16), ffn1_w_down.astype(BF16))
    ffn2 = (ffn2_norm[:, None, :], ffn2_w_gu.astype(BF16), ffn2_w_down.astype(BF16))

    x2 = _ffn_call(x2, 0, *ffn1)
    out_a, q, k, v = _even_proj_call(x2, seq, ev_norm[0], ev_w_in[0], ev_sgu_norm[0], ev_w_s[0],
                                     ev_b_s[0], ev_q_norm[0], ev_k_norm[0], full_tab)
    lam_init = 0.8 - 0.6 * float(np.exp(-0.3 * 0))
    lam_vecs = jnp.stack([ev_lam_q1[0], ev_lam_k1[0], ev_lam_q2[0], ev_lam_k2[0]])
    shp = (b, seq, GMLP_WIDTH)
    out_b = _attention_call(q.reshape(shp), k.reshape(shp), v.reshape(shp), "diff",
                            _score_bound(ev_q_norm[0], ev_k_norm[0], DIFF_D), extra=(lam_vecs, ev_sub_norm[0].reshape(DIFF_V, 1)), lam_init=lam_init)
    w_out = ev_w_out[0].astype(BF16)
    x2 = _ffn_call(x2, 0, *ffn2,
                   mix=(out_a, out_b.reshape(t, -1), w_out[:GMLP_WIDTH], w_out[GMLP_WIDTH:]))

    x2 = _ffn_call(x2, 1, *ffn1)
    qc, kc, vc, qd, kd, vd = _odd_proj_call(
        x2, seq, od_norm[0], od_w_in[0], od_cq_norm[0], od_ckv_norm[0], od_w_uq[0], od_w_ukv[0],
        od_mla_q_norm[0], od_mla_k_norm[0], od_gqa_q_norm[0], od_gqa_k_norm[0], mla_tab, axial_tab)
    r3 = lambda a: a.reshape(b, seq, a.shape[1])
    out_c = _attention_call(r3(qc), r3(kc), r3(vc), "mla",
                            _score_bound(od_mla_q_norm[0], od_mla_k_norm[0], MLA_QK))
    out_d = _attention_call(r3(qd), r3(kd), r3(vd), "gqa",
                            _score_bound(od_gqa_q_norm[0], od_gqa_k_norm[0], GQA_DIM))
    w_out = od_w_out[0].astype(BF16)
    n_c = MLA_HEADS * MLA_V
    w_d = w_out[n_c:].reshape(GQA_KV_HEADS, GQA_GROUP, GQA_DIM, d).transpose(1, 0, 2, 3)
    x2 = _ffn_call(x2, 1, *ffn2,
                   mix=(out_c.reshape(t, -1), out_d.reshape(t, -1), w_out[:n_c],
                        w_d.reshape(GQA_Q_HEADS * GQA_DIM, d)))
    return x2.reshape(b, seq, d)
```

```python
import functools
import math

import numpy as np
import jax
import jax.numpy as jnp
from jax import lax
from jax.experimental import pallas as pl
from jax.experimental.pallas import tpu as pltpu

D_MODEL = 1024
D_FF = 2816
ROPE_THETA = 10000.0
GRID_W = 64
EPS = 1e-6
GMLP_GROUPS = 8
GMLP_GROUP_DIM = 64
GMLP_CHUNK = 128
GMLP_WIDTH = GMLP_GROUPS * GMLP_GROUP_DIM
DIFF_HEADS = 4
DIFF_D = 64
DIFF_V = 128
MLA_HEADS = 8
MLA_Q_RANK = 256
MLA_KV_RANK = 128
MLA_NOPE = 64
MLA_ROPE = 32
MLA_V = 64
MLA_QK = MLA_NOPE + MLA_ROPE
GQA_Q_HEADS = 8
GQA_KV_HEADS = 2
GQA_GROUP = GQA_Q_HEADS // GQA_KV_HEADS
GQA_DIM = 64

LANES = 128
VMEM_LIMIT_BYTES = 56 * 1024 * 1024

FFN_TM = 512
PROJ_TM = 512
ATT_TQ = 2048
ATT_TK = 256
ATT_CW = 512
ATT_PAIRS = 3
SUM_ROWS = 16
LOG2E = math.log2(math.e)
SAFE_SCORE_RANGE = 60.0
ROUNDING_SLACK = 1.0 + 2.0 ** -6

BF16 = jnp.bfloat16
F32 = jnp.float32


def _params(semantics):
    return pltpu.CompilerParams(dimension_semantics=semantics,
                                vmem_limit_bytes=VMEM_LIMIT_BYTES)


def _const_spec(shape):
    nd = len(shape)
    return pl.BlockSpec(shape, lambda *_: (0,) * nd, pipeline_mode=pl.Buffered(1))


def _rms_rows(x, gain):
    ms = jnp.mean(x * x, axis=-1, keepdims=True)
    return x * lax.rsqrt(ms + EPS) * gain


def _dot(a, b):
    return jnp.dot(a, b, preferred_element_type=F32)


def _lane_is_low(shape):
    return lax.broadcasted_iota(jnp.int32, shape, len(shape) - 1) < (LANES // 2)


def _inv_rms_half_blocks(x):
    low = _lane_is_low(x.shape)
    x2 = x * x
    x2_lo = jnp.where(low, x2, 0.0)
    x2_hi = x2 - x2_lo
    ms_lo = jnp.sum(x2_lo, axis=-1, keepdims=True) * (2.0 / LANES)
    ms_hi = jnp.sum(x2_hi, axis=-1, keepdims=True) * (2.0 / LANES)
    return jnp.where(low, lax.rsqrt(ms_lo + EPS), lax.rsqrt(ms_hi + EPS))


def _inv_rms_padded_block(x, width):
    ms = jnp.sum(x * x, axis=-1, keepdims=True) * (1.0 / width)
    return lax.rsqrt(ms + EPS)


def _swap_pairs(a, group):
    shape = a.shape
    a = a.reshape(shape[:-1] + (shape[-1] // group, 2, group // 2))
    return jnp.flip(a, axis=-2).reshape(shape)


def _ffn_body(x, g_ref, wgu_ref, wd_ref, o_ref):
    xn = _rms_rows(x, g_ref[...]).astype(BF16)
    h = _dot(xn, wgu_ref[...])
    gate = h[:, :D_FF]
    up = h[:, D_FF:]
    act = (gate / (1.0 + jnp.exp(-gate)) * up).astype(BF16)
    o_ref[...] = x + 0.5 * _dot(act, wd_ref[...])


def _ffn_kernel(x_ref, g_ref, wgu_ref, wd_ref, o_ref):
    _ffn_body(x_ref[...], g_ref, wgu_ref, wd_ref, o_ref)


def _mix_ffn_kernel(x_ref, a_ref, b_ref, wa_ref, wb_ref, g_ref, wgu_ref, wd_ref, o_ref):
    x = x_ref[...] + _dot(a_ref[...], wa_ref[...]) + _dot(b_ref[...], wb_ref[...])
    _ffn_body(x, g_ref, wgu_ref, wd_ref, o_ref)


def _layer_spec(shape, layer):
    nd = len(shape) - 1
    return pl.BlockSpec((None,) + tuple(shape[1:]), lambda *_: (layer,) + (0,) * nd,
                        pipeline_mode=pl.Buffered(1))


def _ffn_call(x, layer, gains, w_gu, w_down, mix=None):
    t = x.shape[0]
    tm = FFN_TM
    row = lambda i: (i, 0)
    x_spec = pl.BlockSpec((tm, D_MODEL), row)
    w_args = [gains, w_gu, w_down]
    w_specs = [_layer_spec(a.shape, layer) for a in w_args]
    if mix is None:
        kern, in_specs, args = _ffn_kernel, [x_spec] + w_specs, [x] + w_args
    else:
        a, b, wa, wb = mix
        kern = _mix_ffn_kernel
        in_specs = [x_spec, pl.BlockSpec((tm, a.shape[1]), row), pl.BlockSpec((tm, b.shape[1]), row),
                    _const_spec(wa.shape), _const_spec(wb.shape)] + w_specs
        args = [x, a, b, wa, wb] + w_args
    return pl.pallas_call(
        kern, grid=(t // tm,), in_specs=in_specs,
        out_specs=pl.BlockSpec((tm, D_MODEL), row),
        out_shape=jax.ShapeDtypeStruct((t, D_MODEL), F32),
        compiler_params=_params(("parallel",)),
        name="ffn" if mix is None else "mix_ffn",
    )(*args)


def _gelu_tanh(x):
    c = math.sqrt(2.0 / math.pi)
    return 0.5 * x * (1.0 + jnp.tanh(c * (x + 0.044715 * (x * x * x))))


def _even_proj_kernel(x_ref, g_ref, win_ref, sgu_ref, wpair_ref, bias_ref, qg_ref, kg_ref,
                      cos_ref, sin_ref, oa_ref, q_ref, k_ref, v_ref):
    xn = _rms_rows(x_ref[...], g_ref[...]).astype(BF16)
    proj = _dot(xn, win_ref[...])
    w = GMLP_WIDTH
    nblk = w // LANES
    tm = proj.shape[0]
    for j in range(nblk):
        u = _gelu_tanh(proj[:, j * LANES:(j + 1) * LANES])
        v = _gelu_tanh(proj[:, w + j * LANES:w + (j + 1) * LANES])
        vn = v * _inv_rms_half_blocks(v) * sgu_ref[:, j * LANES:(j + 1) * LANES]
        low = _lane_is_low(vn.shape)
        vn_lo = jnp.where(low, vn, 0.0).astype(BF16)
        vn_hi = jnp.where(low, 0.0, vn).astype(BF16)
        wp = wpair_ref[j]
        bias = bias_ref[:, j * LANES:(j + 1) * LANES]
        for c in range(tm // GMLP_CHUNK):
            rows = slice(c * GMLP_CHUNK, (c + 1) * GMLP_CHUNK)
            stacked = jnp.concatenate([vn_lo[rows], vn_hi[rows]], axis=0)
            mixed = _dot(wp, stacked) + bias
            oa_ref[rows, j * LANES:(j + 1) * LANES] = (u[rows] * mixed).astype(BF16)
    cos, sin = cos_ref[...], sin_ref[...]
    q_cos, q_sin = qg_ref[0:1] * cos, qg_ref[1:2] * sin
    k_cos, k_sin = kg_ref[0:1] * cos, kg_ref[1:2] * sin
    for j in range(nblk):
        blk = slice(j * LANES, (j + 1) * LANES)
        qb = proj[:, 2 * w + j * LANES:2 * w + (j + 1) * LANES]
        kb = proj[:, 3 * w + j * LANES:3 * w + (j + 1) * LANES]
        qp = proj[:, 5 * w + j * LANES:5 * w + (j + 1) * LANES]
        kp = proj[:, 6 * w + j * LANES:6 * w + (j + 1) * LANES]
        qr = _inv_rms_half_blocks(qb) * (LOG2E * DIFF_D ** -0.5)
        q_ref[:, blk] = ((qb * q_cos + qp * q_sin) * qr).astype(BF16)
        k_ref[:, blk] = ((kb * k_cos + kp * k_sin) * _inv_rms_half_blocks(kb)).astype(BF16)
    v_ref[...] = proj[:, 4 * w:5 * w].astype(BF16)


def _gain_pair(g, group):
    return jnp.stack([jnp.tile(g, 2), jnp.tile(_swap_pairs(g, group), 2)])


def _even_proj_call(x, seq, gain, w_in, sgu_norm, w_s, b_s, q_norm, k_norm, tables):
    t = x.shape[0]
    tm = PROJ_TM
    w = GMLP_WIDTH
    row = lambda i: (i, 0)
    pos = lambda i: (i % (seq // tm), 0)
    wpair = jnp.concatenate([w_s[0::2], w_s[1::2]], axis=2).astype(BF16)
    bias = jnp.repeat(b_s.T, GMLP_GROUP_DIM, axis=1)
    qk_cols = w_in[:, 2 * w:4 * w]
    win = jnp.concatenate([w_in, _swap_pairs(qk_cols, DIFF_D)], axis=1).astype(BF16)
    args = [x, gain.reshape(1, D_MODEL), win, sgu_norm.reshape(1, w), wpair, bias,
            _gain_pair(q_norm, DIFF_D), _gain_pair(k_norm, DIFF_D), *tables]
    in_specs = [pl.BlockSpec((tm, D_MODEL), row)] + [_const_spec(a.shape) for a in args[1:8]]
    in_specs += [pl.BlockSpec((tm, LANES), pos)] * 2
    out = jax.ShapeDtypeStruct((t, w), BF16)
    return pl.pallas_call(
        _even_proj_kernel, grid=(t // tm,), in_specs=in_specs,
        out_specs=[pl.BlockSpec((tm, w), row)] * 4, out_shape=[out] * 4,
        compiler_params=_params(("parallel",)), name="even_proj",
    )(*args)


def _odd_proj_kernel(x_ref, g_ref, win_ref, cqg_ref, ckvg_ref, wuq_ref, wuk_ref, wuv_ref,
                     mqg_ref, mkg_ref, gqg_ref, gkg_ref, mcos_ref, msin_ref, acos_ref, asin_ref,
                     qc_ref, kc_ref, vc_ref, qd_ref, kd_ref, vd_ref):
    xn = _rms_rows(x_ref[...], g_ref[...]).astype(BF16)
    proj = _dot(xn, win_ref[...])
    nq = GQA_Q_HEADS * GQA_DIM
    o1 = MLA_Q_RANK
    o2 = o1 + MLA_KV_RANK
    o3 = o2 + 2 * LANES
    o4 = o3 + 2 * nq
    o5 = o4 + 2 * LANES
    mcos, msin = mcos_ref[...], msin_ref[...]
    q_cos, q_sin = mqg_ref[0:1] * mcos, mqg_ref[1:2] * msin
    k_cos, k_sin = mkg_ref[0:1] * mcos, mkg_ref[1:2] * msin
    cq = _rms_rows(proj[:, :o1], cqg_ref[...]).astype(BF16)
    q_all = _dot(cq, wuq_ref[...])
    ckv = _rms_rows(proj[:, o1:o2], ckvg_ref[...]).astype(BF16)
    kn_all = _dot(ckv, wuk_ref[...])
    vc_ref[...] = _dot(ckv, wuv_ref[...]).astype(BF16)
    kpe = proj[:, o2:o2 + LANES]
    kpe_sin = proj[:, o2 + LANES:o3] * k_sin
    nh = MLA_HEADS * LANES
    for h in range(MLA_HEADS):
        blk = slice(h * LANES, (h + 1) * LANES)
        qb = q_all[:, blk]
        qp = q_all[:, nh + h * LANES:nh + (h + 1) * LANES]
        qr = _inv_rms_padded_block(qb, MLA_QK) * (LOG2E * MLA_QK ** -0.5)
        qc_ref[:, blk] = ((qb * q_cos + qp * q_sin) * qr).astype(BF16)
        kb = kn_all[:, blk] + kpe
        kc_ref[:, blk] = ((kb * k_cos + kpe_sin) * _inv_rms_padded_block(kb, MLA_QK)).astype(BF16)
    acos, asin = acos_ref[...], asin_ref[...]
    q_cos, q_sin = gqg_ref[0:1] * acos, gqg_ref[1:2] * asin
    for j in range(nq // LANES):
        blk = slice(j * LANES, (j + 1) * LANES)
        qb = proj[:, o3 + j * LANES:o3 + (j + 1) * LANES]
        qp = proj[:, o3 + nq + j * LANES:o3 + nq + (j + 1) * LANES]
        qr = _inv_rms_half_blocks(qb) * (LOG2E * GQA_DIM ** -0.5)
        qd_ref[:, blk] = ((qb * q_cos + qp * q_sin) * qr).astype(BF16)
    kb = proj[:, o4:o4 + LANES]
    kp = proj[:, o4 + LANES:o5]
    kd_ref[...] = ((kb * (gkg_ref[0:1] * acos) + kp * (gkg_ref[1:2] * asin))
                   * _inv_rms_half_blocks(kb)).astype(BF16)
    vd_ref[...] = proj[:, o5:].astype(BF16)


def _pad_heads(w, heads, width):
    r = w.shape[0]
    w = w.reshape(r, heads, width)
    return jnp.pad(w, ((0, 0), (0, 0), (0, LANES - width))).reshape(r, heads * LANES)


def _rope_partner_cols(w, heads):
    r = w.shape[0]
    w = w.reshape(r, heads, MLA_QK)
    rope = _swap_pairs(w[:, :, MLA_NOPE:], MLA_ROPE)
    return jnp.concatenate([jnp.zeros_like(w[:, :, :MLA_NOPE]), rope], axis=2).reshape(r, heads * MLA_QK)


def _odd_proj_call(x, seq, gain, w_in, cq_norm, ckv_norm, w_uq, w_ukv, mq_norm, mk_norm,
                   gq_norm, gk_norm, mla_tables, axial_tables):
    t = x.shape[0]
    tm = PROJ_TM
    row = lambda i: (i, 0)
    pos = lambda i: (i % (seq // tm), 0)
    o1 = MLA_Q_RANK
    o2 = o1 + MLA_KV_RANK
    o3 = o2 + MLA_ROPE
    o4 = o3 + GQA_Q_HEADS * GQA_DIM
    o5 = o4 + GQA_KV_HEADS * GQA_DIM
    place_rope = lambda c: jnp.pad(c, ((0, 0), (MLA_NOPE, LANES - MLA_QK)))
    kpe_cols = w_in[:, o2:o3]
    gq_cols = w_in[:, o3:o4].reshape(D_MODEL, GQA_KV_HEADS, GQA_GROUP, GQA_DIM)
    gq_cols = gq_cols.transpose(0, 2, 1, 3).reshape(D_MODEL, GQA_Q_HEADS * GQA_DIM)
    gk_cols = w_in[:, o4:o5]
    half = GQA_DIM // 2
    win = jnp.concatenate(
        [w_in[:, :o2], place_rope(kpe_cols), place_rope(_swap_pairs(kpe_cols, MLA_ROPE)),
         gq_cols, _swap_pairs(gq_cols, half), gk_cols, _swap_pairs(gk_cols, half), w_in[:, o5:]],
        axis=1).astype(BF16)
    wuq = jnp.concatenate([_pad_heads(w_uq, MLA_HEADS, MLA_QK),
                           _pad_heads(_rope_partner_cols(w_uq, MLA_HEADS), MLA_HEADS, MLA_QK)],
                          axis=1).astype(BF16)
    w_ukv = w_ukv.reshape(MLA_KV_RANK, MLA_HEADS, MLA_NOPE + MLA_V)
    wuk = _pad_heads(w_ukv[:, :, :MLA_NOPE].reshape(MLA_KV_RANK, -1), MLA_HEADS, MLA_NOPE).astype(BF16)
    wuv = w_ukv[:, :, MLA_NOPE:].reshape(MLA_KV_RANK, MLA_HEADS * MLA_V).astype(BF16)

    def mla_gain_pair(g):
        partner = jnp.concatenate([g[:MLA_NOPE], _swap_pairs(g[MLA_NOPE:], MLA_ROPE)])
        return jnp.pad(jnp.stack([g, partner]), ((0, 0), (0, LANES - MLA_QK)))

    args = [x, gain.reshape(1, D_MODEL), win, cq_norm.reshape(1, -1), ckv_norm.reshape(1, -1),
            wuq, wuk, wuv, mla_gain_pair(mq_norm), mla_gain_pair(mk_norm),
            _gain_pair(gq_norm, half), _gain_pair(gk_norm, half), *mla_tables, *axial_tables]
    in_specs = ([pl.BlockSpec((tm, D_MODEL), row)]
                + [_const_spec(a.shape) for a in args[1:12]]
                + [pl.BlockSpec((tm, LANES), pos)] * 4)
    widths = [MLA_HEADS * LANES, MLA_HEADS * LANES, MLA_HEADS * MLA_V,
              GQA_Q_HEADS * GQA_DIM, LANES, LANES]
    return pl.pallas_call(
        _odd_proj_kernel, grid=(t // tm,), in_specs=in_specs,
        out_specs=[pl.BlockSpec((tm, n), row) for n in widths],
        out_shape=[jax.ShapeDtypeStruct((t, n), BF16) for n in widths],
        compiler_params=_params(("parallel",)), name="odd_proj",
    )(*args)


def _attention_kernel(*refs, mode, lam_init, stabilised):
    if mode == "diff":
        q_ref, k_ref, v_ref, lam_ref, sub_ref, o_ref = refs[:6]
    else:
        q_ref, k_ref, v_ref, o_ref = refs[:4]
    qt_ref, tile_buf, stat_ref, acc_ref = refs[-4:]
    tq = q_ref.shape[0]
    seq = k_ref.shape[0]
    tk = ATT_TK
    cw = ATT_CW
    n = seq // tk
    q = q_ref[...].astype(F32)
    if mode == "mla":
        qt_ref[:, :tq] = q[:, :LANES].T.astype(BF16)
        qt_ref[:, tq:] = q[:, LANES:].T.astype(BF16)
    else:
        qt = q.T
        first = lax.broadcasted_iota(jnp.int32, qt.shape, 0) < (LANES // 2)
        qt_ref[:, :tq] = jnp.where(first, qt, 0.0).astype(BF16)
        qt_ref[:, tq:] = jnp.where(first, 0.0, qt).astype(BF16)

    chunks = [slice(c * cw, (c + 1) * cw) for c in range(2 * tq // cw)]

    def tile_rows(t):
        return pl.ds(pl.multiple_of(t * tk, tk), tk)

    def ahead(kt, slot, cols):
        kc = kt
        if mode == "mla":
            kc = kt[:, :LANES] if cols.start < tq else kt[:, LANES:]
        s = _dot(kc, qt_ref[:, cols])
        if stabilised:
            tile_buf[slot, :, cols] = s
        else:
            p = jnp.exp2(s)
            stat_ref[:, cols] += jnp.sum(p.reshape(tk // 8, 8, p.shape[1]), axis=0)
            tile_buf[slot, :, cols] = p.astype(BF16)

    def consume(vt, slot, cols):
        if stabilised:
            s = tile_buf[slot, :, cols]
            m_old = stat_ref[:, cols]
            m_new = jnp.maximum(m_old, jnp.max(s, axis=0, keepdims=True))
            alpha = jnp.exp2(m_old - m_new)
            p = jnp.exp2(s - m_new).astype(BF16)
            stat_ref[:, cols] = m_new
            acc_ref[:, cols] = alpha * acc_ref[:, cols] + _dot(vt, p)
        else:
            acc_ref[:, cols] += _dot(vt, tile_buf[slot, :, cols])

    def value_rows(t):
        vt = v_ref[tile_rows(t), :].T
        parts = (vt, vt) if mode == "diff" else (vt[:LANES // 2], vt[LANES // 2:])
        if not stabilised:
            return parts
        ones = jnp.ones((SUM_ROWS, tk), BF16)
        return tuple(jnp.concatenate([part, ones], axis=0) for part in parts)

    def half_step(t, t_next, cur, nxt):
        kt = k_ref[tile_rows(t_next), :]
        vt = value_rows(t)
        for cols in chunks:
            ahead(kt, nxt, cols)
            consume(vt[0] if cols.start < tq else vt[1], cur, cols)

    stat_ref[...] = jnp.full(stat_ref.shape, -jnp.inf if stabilised else 0.0, F32)
    acc_ref[...] = jnp.zeros(acc_ref.shape, F32)
    kt0 = k_ref[tile_rows(0), :]
    for cols in chunks:
        ahead(kt0, 0, cols)

    def body(i, carry):
        for u in range(ATT_PAIRS):
            t = 2 * (i * ATT_PAIRS + u)
            half_step(t, t + 1, 0, 1)
            half_step(t + 1, t + 2, 1, 0)
        return carry

    lax.fori_loop(0, (n - 2) // (2 * ATT_PAIRS), body, 0)
    half_step(n - 2, n - 1, 0, 1)
    vt_last = value_rows(n - 1)
    for cols in chunks:
        consume(vt_last[0] if cols.start < tq else vt_last[1], 1, cols)

    if stabilised:
        nv = acc_ref.shape[0] - SUM_ROWS
        o = acc_ref[:nv, :] / acc_ref[nv:nv + 1, :]
    else:
        o = acc_ref[...] / jnp.sum(stat_ref[...], axis=0, keepdims=True)
    if mode == "diff":
        lv = lam_ref[...]
        lam = (jnp.exp(jnp.sum(lv[0:1] * lv[1:2], axis=-1, keepdims=True))
               - jnp.exp(jnp.sum(lv[2:3] * lv[3:4], axis=-1, keepdims=True)) + lam_init)
        ot = o[:, :tq] - lam * o[:, tq:]
        ms = jnp.mean(ot * ot, axis=0, keepdims=True)
        ot = ot * lax.rsqrt(ms + EPS) * (sub_ref[...] * (1.0 - lam_init))
        o_ref[...] = ot.T.astype(o_ref.dtype)
    else:
        ot = jnp.concatenate([o[:, :tq], o[:, tq:]], axis=0).T
        o_ref[...] = ot.astype(o_ref.dtype)


def _attention_call(q, k, v, mode, score_bound, extra=(), lam_init=0.0):
    b, seq, _ = q.shape
    qw = 2 * LANES if mode == "mla" else LANES
    nblk = q.shape[2] // qw
    tq, tk = ATT_TQ, ATT_TK
    assert seq % tk == 0 and (seq // tk - 2) % (2 * ATT_PAIRS) == 0 and seq % tq == 0 and tq % ATT_CW == 0
    shared = mode == "gqa"
    value_rows = LANES if mode == "diff" else LANES // 2
    kv_idx = (lambda bi, j, i: (bi, 0, 0)) if shared else (lambda bi, j, i: (bi, 0, j))
    in_specs = [pl.BlockSpec((None, tq, qw), lambda bi, j, i: (bi, i, j)),
                pl.BlockSpec((None, seq, qw), kv_idx),
                pl.BlockSpec((None, seq, LANES), kv_idx)]
    in_specs += [_const_spec(e.shape) for e in extra]

    def call(stabilised):
        if stabilised:
            scratch = [pltpu.VMEM((2, tk, 2 * tq), F32), pltpu.VMEM((1, 2 * tq), F32),
                       pltpu.VMEM((value_rows + SUM_ROWS, 2 * tq), F32)]
        else:
            scratch = [pltpu.VMEM((2, tk, 2 * tq), BF16), pltpu.VMEM((8, 2 * tq), F32),
                       pltpu.VMEM((value_rows, 2 * tq), F32)]
        return pl.pallas_call(
            functools.partial(_attention_kernel, mode=mode, lam_init=lam_init, stabilised=stabilised),
            grid=(b, nblk, seq // tq), in_specs=in_specs,
            out_specs=pl.BlockSpec((None, tq, LANES), lambda bi, j, i: (bi, i, j)),
            out_shape=jax.ShapeDtypeStruct((b, seq, nblk * LANES), BF16),
            scratch_shapes=[pltpu.VMEM((LANES, 2 * tq), BF16)] + scratch,
            compiler_params=_params(("parallel", "parallel", "arbitrary")),
            name="attn_" + mode + ("_stab" if stabilised else ""),
        )

    return lax.cond(score_bound <= SAFE_SCORE_RANGE, call(False), call(True), q, k, v, *extra)


def _score_bound(q_gain, k_gain, dim):
    return (jnp.max(jnp.abs(q_gain)) * jnp.max(jnp.abs(k_gain))
            * (dim * dim ** -0.5 * LOG2E * ROUNDING_SLACK))


def _rope_tables(seq):
    assert MLA_ROPE == GQA_DIM // 2 and seq % GRID_W == 0
    pos = jnp.arange(seq, dtype=jnp.int32)
    zeros = lambda n: jnp.zeros((seq, n), F32)

    def angles(dim):
        inv = ROPE_THETA ** (-jnp.arange(0, dim, 2, dtype=F32) / dim)
        ang = pos.astype(F32)[:, None] * inv[None, :]
        return lax.optimization_barrier((jnp.cos(ang), jnp.sin(ang)))

    c, s = angles(DIFF_D)
    full = (jnp.tile(jnp.concatenate([c, c], 1), (1, 2)), jnp.tile(jnp.concatenate([-s, s], 1), (1, 2)))
    c, s = angles(MLA_ROPE)
    tail = zeros(LANES - MLA_QK)
    mla = (jnp.concatenate([jnp.ones((seq, MLA_NOPE), F32), c, c, tail], 1),
           jnp.concatenate([zeros(MLA_NOPE), -s, s, tail], 1))
    rows = seq // GRID_W
    cr, sr = jnp.repeat(c[:rows], GRID_W, axis=0), jnp.repeat(s[:rows], GRID_W, axis=0)
    cc, sc = jnp.tile(c[:GRID_W], (rows, 1)), jnp.tile(s[:GRID_W], (rows, 1))
    axial = (jnp.tile(jnp.concatenate([cr, cr, cc, cc], 1), (1, 2)),
             jnp.tile(jnp.concatenate([-sr, sr, -sc, sc], 1), (1, 2)))
    return full, mla, axial


def kernel(x, ffn1_norm, ffn1_w_gu, ffn1_w_down, ffn2_norm, ffn2_w_gu, ffn2_w_down, ev_norm, ev_w_in, ev_sgu_norm, ev_w_s, ev_b_s, ev_q_norm, ev_k_norm, ev_lam_q1, ev_lam_k1, ev_lam_q2, ev_lam_k2, ev_sub_norm, ev_w_out, od_norm, od_w_in, od_cq_norm, od_ckv_norm, od_w_uq, od_w_ukv, od_mla_q_norm, od_mla_k_norm, od_gqa_q_norm, od_gqa_k_norm, od_w_out):
    b, seq, d = x.shape
    t = b * seq
    full_tab, mla_tab, axial_tab = _rope_tables(seq)
    x2 = x.reshape(t, d)
    ffn1 = (ffn1_norm[:, None, :], ffn1_w_gu.astype(BF16), ffn1_w_down.astype(BF16))
    ffn2 = (ffn2_norm[:, None, :], ffn2_w_gu.astype(BF16), ffn2_w_down.astype(BF16))

    x2 = _ffn_call(x2, 0, *ffn1)
    out_a, q, k, v = _even_proj_call(x2, seq, ev_norm[0], ev_w_in[0], ev_sgu_norm[0], ev_w_s[0],
                                     ev_b_s[0], ev_q_norm[0], ev_k_norm[0], full_tab)
    lam_init = 0.8 - 0.6 * float(np.exp(-0.3 * 0))
    lam_vecs = jnp.stack([ev_lam_q1[0], ev_lam_k1[0], ev_lam_q2[0], ev_lam_k2[0]])
    shp = (b, seq, GMLP_WIDTH)
    out_b = _attention_call(q.reshape(shp), k.reshape(shp), v.reshape(shp), "diff",
                            _score_bound(ev_q_norm[0], ev_k_norm[0], DIFF_D), extra=(lam_vecs, ev_sub_norm[0].reshape(DIFF_V, 1)), lam_init=lam_init)
    w_out = ev_w_out[0].astype(BF16)
    x2 = _ffn_call(x2, 0, *ffn2,
                   mix=(out_a, out_b.reshape(t, -1), w_out[:GMLP_WIDTH], w_out[GMLP_WIDTH:]))

    x2 = _ffn_call(x2, 1, *ffn1)
    qc, kc, vc, qd, kd, vd = _odd_proj_call(
        x2, seq, od_norm[0], od_w_in[0], od_cq_norm[0], od_ckv_norm[0], od_w_uq[0], od_w_ukv[0],
        od_mla_q_norm[0], od_mla_k_norm[0], od_gqa_q_norm[0], od_gqa_k_norm[0], mla_tab, axial_tab)
    r3 = lambda a: a.reshape(b, seq, a.shape[1])
    out_c = _attention_call(r3(qc), r3(kc), r3(vc), "mla",
                            _score_bound(od_mla_q_norm[0], od_mla_k_norm[0], MLA_QK))
    out_d = _attention_call(r3(qd), r3(kd), r3(vd), "gqa",
                            _score_bound(od_gqa_q_norm[0], od_gqa_k_norm[0], GQA_DIM))
    w_out = od_w_out[0].astype(BF16)
    n_c = MLA_HEADS * MLA_V
    w_d = w_out[n_c:].reshape(GQA_KV_HEADS, GQA_GROUP, GQA_DIM, d).transpose(1, 0, 2, 3)
    x2 = _ffn_call(x2, 1, *ffn2,
                   mix=(out_c.reshape(t, -1), out_d.reshape(t, -1), w_out[:n_c],
                        w_d.reshape(GQA_Q_HEADS * GQA_DIM, d)))
    return x2.reshape(b, seq, d)
```

```python
import functools
import math

import numpy as np
import jax
import jax.numpy as jnp
from jax import lax
from jax.experimental import pallas as pl
from jax.experimental.pallas import tpu as pltpu

D_MODEL = 1024
D_FF = 2816
ROPE_THETA = 10000.0
GRID_W = 64
EPS = 1e-6
GMLP_GROUPS = 8
GMLP_GROUP_DIM = 64
GMLP_CHUNK = 128
GMLP_WIDTH = GMLP_GROUPS * GMLP_GROUP_DIM
DIFF_HEADS = 4
DIFF_D = 64
DIFF_V = 128
MLA_HEADS = 8
MLA_Q_RANK = 256
MLA_KV_RANK = 128
MLA_NOPE = 64
MLA_ROPE = 32
MLA_V = 64
MLA_QK = MLA_NOPE + MLA_ROPE
GQA_Q_HEADS = 8
GQA_KV_HEADS = 2
GQA_GROUP = GQA_Q_HEADS // GQA_KV_HEADS
GQA_DIM = 64

LANES = 128
VMEM_LIMIT_BYTES = 56 * 1024 * 1024

FFN_TM = 1024
FFN_SUB = 256
PROJ_TM = 512
ATT_TQ = 2048
ATT_TK = 256
ATT_CW = 512
ATT_PAIRS = 3
SUM_ROWS = 16
LOG2E = math.log2(math.e)
SAFE_SCORE_RANGE = 60.0
ROUNDING_SLACK = 1.0 + 2.0 ** -6

BF16 = jnp.bfloat16
F32 = jnp.float32


def _params(semantics):
    return pltpu.CompilerParams(dimension_semantics=semantics,
                                vmem_limit_bytes=VMEM_LIMIT_BYTES)


def _const_spec(shape):
    nd = len(shape)
    return pl.BlockSpec(shape, lambda *_: (0,) * nd, pipeline_mode=pl.Buffered(1))


def _rms_rows(x, gain):
    ms = jnp.mean(x * x, axis=-1, keepdims=True)
    return x * lax.rsqrt(ms + EPS) * gain


def _dot(a, b):
    return jnp.dot(a, b, preferred_element_type=F32)


def _lane_is_low(shape):
    return lax.broadcasted_iota(jnp.int32, shape, len(shape) - 1) < (LANES // 2)


def _inv_rms_half_blocks(x):
    low = _lane_is_low(x.shape)
    x2 = x * x
    x2_lo = jnp.where(low, x2, 0.0)
    x2_hi = x2 - x2_lo
    ms_lo = jnp.sum(x2_lo, axis=-1, keepdims=True) * (2.0 / LANES)
    ms_hi = jnp.sum(x2_hi, axis=-1, keepdims=True) * (2.0 / LANES)
    return jnp.where(low, lax.rsqrt(ms_lo + EPS), lax.rsqrt(ms_hi + EPS))


def _inv_rms_padded_block(x, width):
    ms = jnp.sum(x * x, axis=-1, keepdims=True) * (1.0 / width)
    return lax.rsqrt(ms + EPS)


def _swap_pairs(a, group):
    shape = a.shape
    a = a.reshape(shape[:-1] + (shape[-1] // group, 2, group // 2))
    return jnp.flip(a, axis=-2).reshape(shape)


def _ffn_body(x, g_ref, wgu_ref, wd_ref, o_ref):
    for r in range(x.shape[0] // FFN_SUB):
        rows = slice(r * FFN_SUB, (r + 1) * FFN_SUB)
        xr = x[rows]
        xn = _rms_rows(xr, g_ref[...]).astype(BF16)
        h = _dot(xn, wgu_ref[...])
        gate = h[:, :D_FF]
        up = h[:, D_FF:]
        act = (gate / (1.0 + jnp.exp(-gate)) * up).astype(BF16)
        o_ref[rows, :] = xr + 0.5 * _dot(act, wd_ref[...])


def _ffn_kernel(x_ref, g_ref, wgu_ref, wd_ref, o_ref):
    _ffn_body(x_ref[...], g_ref, wgu_ref, wd_ref, o_ref)


def _mix_ffn_kernel(x_ref, a_ref, b_ref, wa_ref, wb_ref, g_ref, wgu_ref, wd_ref, o_ref):
    x = x_ref[...] + _dot(a_ref[...], wa_ref[...]) + _dot(b_ref[...], wb_ref[...])
    _ffn_body(x, g_ref, wgu_ref, wd_ref, o_ref)


def _layer_spec(shape, layer):
    nd = len(shape) - 1
    return pl.BlockSpec((None,) + tuple(shape[1:]), lambda *_: (layer,) + (0,) * nd,
                        pipeline_mode=pl.Buffered(1))


def _ffn_call(x, layer, gains, w_gu, w_down, mix=None):
    t = x.shape[0]
    tm = FFN_TM
    row = lambda i: (i, 0)
    x_spec = pl.BlockSpec((tm, D_MODEL), row)
    w_args = [gains, w_gu, w_down]
    w_specs = [_layer_spec(a.shape, layer) for a in w_args]
    if mix is None:
        kern, in_specs, args = _ffn_kernel, [x_spec] + w_specs, [x] + w_args
    else:
        a, b, wa, wb = mix
        kern = _mix_ffn_kernel
        in_specs = [x_spec, pl.BlockSpec((tm, a.shape[1]), row), pl.BlockSpec((tm, b.shape[1]), row),
                    _const_spec(wa.shape), _const_spec(wb.shape)] + w_specs
        args = [x, a, b, wa, wb] + w_args
    return pl.pallas_call(
        kern, grid=(t // tm,), in_specs=in_specs,
        out_specs=pl.BlockSpec((tm, D_MODEL), row),
        out_shape=jax.ShapeDtypeStruct((t, D_MODEL), F32),
        compiler_params=_params(("parallel",)),
        name="ffn" if mix is None else "mix_ffn",
    )(*args)


def _gelu_tanh(x):
    c = math.sqrt(2.0 / math.pi)
    return 0.5 * x * (1.0 + jnp.tanh(c * (x + 0.044715 * (x * x * x))))


def _even_proj_kernel(x_ref, g_ref, win_ref, sgu_ref, wpair_ref, bias_ref, qg_ref, kg_ref,
                      cos_ref, sin_ref, oa_ref, q_ref, k_ref, v_ref):
    xn = _rms_rows(x_ref[...], g_ref[...]).astype(BF16)
    proj = _dot(xn, win_ref[...])
    w = GMLP_WIDTH
    nblk = w // LANES
    tm = proj.shape[0]
    for j in range(nblk):
        u = _gelu_tanh(proj[:, j * LANES:(j + 1) * LANES])
        v = _gelu_tanh(proj[:, w + j * LANES:w + (j + 1) * LANES])
        vn = v * _inv_rms_half_blocks(v) * sgu_ref[:, j * LANES:(j + 1) * LANES]
        low = _lane_is_low(vn.shape)
        vn_lo = jnp.where(low, vn, 0.0).astype(BF16)
        vn_hi = jnp.where(low, 0.0, vn).astype(BF16)
        wp = wpair_ref[j]
        bias = bias_ref[:, j * LANES:(j + 1) * LANES]
        for c in range(tm // GMLP_CHUNK):
            rows = slice(c * GMLP_CHUNK, (c + 1) * GMLP_CHUNK)
            stacked = jnp.concatenate([vn_lo[rows], vn_hi[rows]], axis=0)
            mixed = _dot(wp, stacked) + bias
            oa_ref[rows, j * LANES:(j + 1) * LANES] = (u[rows] * mixed).astype(BF16)
    cos, sin = cos_ref[...], sin_ref[...]
    q_cos, q_sin = qg_ref[0:1] * cos, qg_ref[1:2] * sin
    k_cos, k_sin = kg_ref[0:1] * cos, kg_ref[1:2] * sin
    for j in range(nblk):
        blk = slice(j * LANES, (j + 1) * LANES)
        qb = proj[:, 2 * w + j * LANES:2 * w + (j + 1) * LANES]
        kb = proj[:, 3 * w + j * LANES:3 * w + (j + 1) * LANES]
        qp = proj[:, 5 * w + j * LANES:5 * w + (j + 1) * LANES]
        kp = proj[:, 6 * w + j * LANES:6 * w + (j + 1) * LANES]
        qr = _inv_rms_half_blocks(qb) * (LOG2E * DIFF_D ** -0.5)
        q_ref[:, blk] = ((qb * q_cos + qp * q_sin) * qr).astype(BF16)
        k_ref[:, blk] = ((kb * k_cos + kp * k_sin) * _inv_rms_half_blocks(kb)).astype(BF16)
    v_ref[...] = proj[:, 4 * w:5 * w].astype(BF16)


def _gain_pair(g, group):
    return jnp.stack([jnp.tile(g, 2), jnp.tile(_swap_pairs(g, group), 2)])


def _even_proj_call(x, seq, gain, w_in, sgu_norm, w_s, b_s, q_norm, k_norm, tables):
    t = x.shape[0]
    tm = PROJ_TM
    w = GMLP_WIDTH
    row = lambda i: (i, 0)
    pos = lambda i: (i % (seq // tm), 0)
    wpair = jnp.concatenate([w_s[0::2], w_s[1::2]], axis=2).astype(BF16)
    bias = jnp.repeat(b_s.T, GMLP_GROUP_DIM, axis=1)
    qk_cols = w_in[:, 2 * w:4 * w]
    win = jnp.concatenate([w_in, _swap_pairs(qk_cols, DIFF_D)], axis=1).astype(BF16)
    args = [x, gain.reshape(1, D_MODEL), win, sgu_norm.reshape(1, w), wpair, bias,
            _gain_pair(q_norm, DIFF_D), _gain_pair(k_norm, DIFF_D), *tables]
    in_specs = [pl.BlockSpec((tm, D_MODEL), row)] + [_const_spec(a.shape) for a in args[1:8]]
    in_specs += [pl.BlockSpec((tm, LANES), pos)] * 2
    out = jax.ShapeDtypeStruct((t, w), BF16)
    return pl.pallas_call(
        _even_proj_kernel, grid=(t // tm,), in_specs=in_specs,
        out_specs=[pl.BlockSpec((tm, w), row)] * 4, out_shape=[out] * 4,
        compiler_params=_params(("parallel",)), name="even_proj",
    )(*args)


def _odd_proj_kernel(x_ref, g_ref, win_ref, cqg_ref, ckvg_ref, wuq_ref, wuk_ref, wuv_ref,
                     mqg_ref, mkg_ref, gqg_ref, gkg_ref, mcos_ref, msin_ref, acos_ref, asin_ref,
                     qc_ref, kc_ref, vc_ref, qd_ref, kd_ref, vd_ref):
    xn = _rms_rows(x_ref[...], g_ref[...]).astype(BF16)
    proj = _dot(xn, win_ref[...])
    nq = GQA_Q_HEADS * GQA_DIM
    o1 = MLA_Q_RANK
    o2 = o1 + MLA_KV_RANK
    o3 = o2 + 2 * LANES
    o4 = o3 + 2 * nq
    o5 = o4 + 2 * LANES
    mcos, msin = mcos_ref[...], msin_ref[...]
    q_cos, q_sin = mqg_ref[0:1] * mcos, mqg_ref[1:2] * msin
    k_cos, k_sin = mkg_ref[0:1] * mcos, mkg_ref[1:2] * msin
    cq = _rms_rows(proj[:, :o1], cqg_ref[...]).astype(BF16)
    q_all = _dot(cq, wuq_ref[...])
    ckv = _rms_rows(proj[:, o1:o2], ckvg_ref[...]).astype(BF16)
    kn_all = _dot(ckv, wuk_ref[...])
    vc_ref[...] = _dot(ckv, wuv_ref[...]).astype(BF16)
    kpe = proj[:, o2:o2 + LANES]
    kpe_sin = proj[:, o2 + LANES:o3] * k_sin
    nh = MLA_HEADS * LANES
    for h in range(MLA_HEADS):
        blk = slice(h * LANES, (h + 1) * LANES)
        qb = q_all[:, blk]
        qp = q_all[:, nh + h * LANES:nh + (h + 1) * LANES]
        qr = _inv_rms_padded_block(qb, MLA_QK) * (LOG2E * MLA_QK ** -0.5)
        qc_ref[:, blk] = ((qb * q_cos + qp * q_sin) * qr).astype(BF16)
        kb = kn_all[:, blk] + kpe
        kc_ref[:, blk] = ((kb * k_cos + kpe_sin) * _inv_rms_padded_block(kb, MLA_QK)).astype(BF16)
    acos, asin = acos_ref[...], asin_ref[...]
    q_cos, q_sin = gqg_ref[0:1] * acos, gqg_ref[1:2] * asin
    for j in range(nq // LANES):
        blk = slice(j * LANES, (j + 1) * LANES)
        qb = proj[:, o3 + j * LANES:o3 + (j + 1) * LANES]
        qp = proj[:, o3 + nq + j * LANES:o3 + nq + (j + 1) * LANES]
        qr = _inv_rms_half_blocks(qb) * (LOG2E * GQA_DIM ** -0.5)
        qd_ref[:, blk] = ((qb * q_cos + qp * q_sin) * qr).astype(BF16)
    kb = proj[:, o4:o4 + LANES]
    kp = proj[:, o4 + LANES:o5]
    kd_ref[...] = ((kb * (gkg_ref[0:1] * acos) + kp * (gkg_ref[1:2] * asin))
                   * _inv_rms_half_blocks(kb)).astype(BF16)
    vd_ref[...] = proj[:, o5:].astype(BF16)


def _pad_heads(w, heads, width):
    r = w.shape[0]
    w = w.reshape(r, heads, width)
    return jnp.pad(w, ((0, 0), (0, 0), (0, LANES - width))).reshape(r, heads * LANES)


def _rope_partner_cols(w, heads):
    r = w.shape[0]
    w = w.reshape(r, heads, MLA_QK)
    rope = _swap_pairs(w[:, :, MLA_NOPE:], MLA_ROPE)
    return jnp.concatenate([jnp.zeros_like(w[:, :, :MLA_NOPE]), rope], axis=2).reshape(r, heads * MLA_QK)


def _odd_proj_call(x, seq, gain, w_in, cq_norm, ckv_norm, w_uq, w_ukv, mq_norm, mk_norm,
                   gq_norm, gk_norm, mla_tables, axial_tables):
    t = x.shape[0]
    tm = PROJ_TM
    row = lambda i: (i, 0)
    pos = lambda i: (i % (seq // tm), 0)
    o1 = MLA_Q_RANK
    o2 = o1 + MLA_KV_RANK
    o3 = o2 + MLA_ROPE
    o4 = o3 + GQA_Q_HEADS * GQA_DIM
    o5 = o4 + GQA_KV_HEADS * GQA_DIM
    place_rope = lambda c: jnp.pad(c, ((0, 0), (MLA_NOPE, LANES - MLA_QK)))
    kpe_cols = w_in[:, o2:o3]
    gq_cols = w_in[:, o3:o4].reshape(D_MODEL, GQA_KV_HEADS, GQA_GROUP, GQA_DIM)
    gq_cols = gq_cols.transpose(0, 2, 1, 3).reshape(D_MODEL, GQA_Q_HEADS * GQA_DIM)
    gk_cols = w_in[:, o4:o5]
    half = GQA_DIM // 2
    win = jnp.concatenate(
        [w_in[:, :o2], place_rope(kpe_cols), place_rope(_swap_pairs(kpe_cols, MLA_ROPE)),
         gq_cols, _swap_pairs(gq_cols, half), gk_cols, _swap_pairs(gk_cols, half), w_in[:, o5:]],
        axis=1).astype(BF16)
    wuq = jnp.concatenate([_pad_heads(w_uq, MLA_HEADS, MLA_QK),
                           _pad_heads(_rope_partner_cols(w_uq, MLA_HEADS), MLA_HEADS, MLA_QK)],
                          axis=1).astype(BF16)
    w_ukv = w_ukv.reshape(MLA_KV_RANK, MLA_HEADS, MLA_NOPE + MLA_V)
    wuk = _pad_heads(w_ukv[:, :, :MLA_NOPE].reshape(MLA_KV_RANK, -1), MLA_HEADS, MLA_NOPE).astype(BF16)
    wuv = w_ukv[:, :, MLA_NOPE:].reshape(MLA_KV_RANK, MLA_HEADS * MLA_V).astype(BF16)

    def mla_gain_pair(g):
        partner = jnp.concatenate([g[:MLA_NOPE], _swap_pairs(g[MLA_NOPE:], MLA_ROPE)])
        return jnp.pad(jnp.stack([g, partner]), ((0, 0), (0, LANES - MLA_QK)))

    args = [x, gain.reshape(1, D_MODEL), win, cq_norm.reshape(1, -1), ckv_norm.reshape(1, -1),
            wuq, wuk, wuv, mla_gain_pair(mq_norm), mla_gain_pair(mk_norm),
            _gain_pair(gq_norm, half), _gain_pair(gk_norm, half), *mla_tables, *axial_tables]
    in_specs = ([pl.BlockSpec((tm, D_MODEL), row)]
                + [_const_spec(a.shape) for a in args[1:12]]
                + [pl.BlockSpec((tm, LANES), pos)] * 4)
    widths = [MLA_HEADS * LANES, MLA_HEADS * LANES, MLA_HEADS * MLA_V,
              GQA_Q_HEADS * GQA_DIM, LANES, LANES]
    return pl.pallas_call(
        _odd_proj_kernel, grid=(t // tm,), in_specs=in_specs,
        out_specs=[pl.BlockSpec((tm, n), row) for n in widths],
        out_shape=[jax.ShapeDtypeStruct((t, n), BF16) for n in widths],
        compiler_params=_params(("parallel",)), name="odd_proj",
    )(*args)


def _attention_kernel(*refs, mode, lam_init, stabilised):
    if mode == "diff":
        q_ref, k_ref, v_ref, lam_ref, sub_ref, o_ref = refs[:6]
    else:
        q_ref, k_ref, v_ref, o_ref = refs[:4]
    qt_ref, tile_buf, stat_ref, acc_ref = refs[-4:]
    tq = q_ref.shape[0]
    seq = k_ref.shape[0]
    tk = ATT_TK
    cw = ATT_CW
    n = seq // tk
    q = q_ref[...].astype(F32)
    if mode == "mla":
        qt_ref[:, :tq] = q[:, :LANES].T.astype(BF16)
        qt_ref[:, tq:] = q[:, LANES:].T.astype(BF16)
    else:
        qt = q.T
        first = lax.broadcasted_iota(jnp.int32, qt.shape, 0) < (LANES // 2)
        qt_ref[:, :tq] = jnp.where(first, qt, 0.0).astype(BF16)
        qt_ref[:, tq:] = jnp.where(first, 0.0, qt).astype(BF16)

    chunks = [slice(c * cw, (c + 1) * cw) for c in range(2 * tq // cw)]

    def tile_rows(t):
        return pl.ds(pl.multiple_of(t * tk, tk), tk)

    def ahead(kt, slot, cols):
        kc = kt
        if mode == "mla":
            kc = kt[:, :LANES] if cols.start < tq else kt[:, LANES:]
        s = _dot(kc, qt_ref[:, cols])
        if stabilised:
            tile_buf[slot, :, cols] = s
        else:
            p = jnp.exp2(s)
            stat_ref[:, cols] += jnp.sum(p.reshape(tk // 8, 8, p.shape[1]), axis=0)
            tile_buf[slot, :, cols] = p.astype(BF16)

    def consume(vt, slot, cols):
        if stabilised:
            s = tile_buf[slot, :, cols]
            m_old = stat_ref[:, cols]
            m_new = jnp.maximum(m_old, jnp.max(s, axis=0, keepdims=True))
            alpha = jnp.exp2(m_old - m_new)
            p = jnp.exp2(s - m_new).astype(BF16)
            stat_ref[:, cols] = m_new
            acc_ref[:, cols] = alpha * acc_ref[:, cols] + _dot(vt, p)
        else:
            acc_ref[:, cols] += _dot(vt, tile_buf[slot, :, cols])

    def value_rows(t):
        vt = v_ref[tile_rows(t), :].T
        parts = (vt, vt) if mode == "diff" else (vt[:LANES // 2], vt[LANES // 2:])
        if not stabilised:
            return parts
        ones = jnp.ones((SUM_ROWS, tk), BF16)
        return tuple(jnp.concatenate([part, ones], axis=0) for part in parts)

    def half_step(t, t_next, cur, nxt):
        kt = k_ref[tile_rows(t_next), :]
        vt = value_rows(t)
        for cols in chunks:
            ahead(kt, nxt, cols)
            consume(vt[0] if cols.start < tq else vt[1], cur, cols)

    stat_ref[...] = jnp.full(stat_ref.shape, -jnp.inf if stabilised else 0.0, F32)
    acc_ref[...] = jnp.zeros(acc_ref.shape, F32)
    kt0 = k_ref[tile_rows(0), :]
    for cols in chunks:
        ahead(kt0, 0, cols)

    def body(i, carry):
        for u in range(ATT_PAIRS):
            t = 2 * (i * ATT_PAIRS + u)
            half_step(t, t + 1, 0, 1)
            half_step(t + 1, t + 2, 1, 0)
        return carry

    lax.fori_loop(0, (n - 2) // (2 * ATT_PAIRS), body, 0)
    half_step(n - 2, n - 1, 0, 1)
    vt_last = value_rows(n - 1)
    for cols in chunks:
        consume(vt_last[0] if cols.start < tq else vt_last[1], 1, cols)

    if stabilised:
        nv = acc_ref.shape[0] - SUM_ROWS
        o = acc_ref[:nv, :] / acc_ref[nv:nv + 1, :]
    else:
        o = acc_ref[...] / jnp.sum(stat_ref[...], axis=0, keepdims=True)
    if mode == "diff":
        lv = lam_ref[...]
        lam = (jnp.exp(jnp.sum(lv[0:1] * lv[1:2], axis=-1, keepdims=True))
               - jnp.exp(jnp.sum(lv[2:3] * lv[3:4], axis=-1, keepdims=True)) + lam_init)
        ot = o[:, :tq] - lam * o[:, tq:]
        ms = jnp.mean(ot * ot, axis=0, keepdims=True)
        ot = ot * lax.rsqrt(ms + EPS) * (sub_ref[...] * (1.0 - lam_init))
        o_ref[...] = ot.T.astype(o_ref.dtype)
    else:
        ot = jnp.concatenate([o[:, :tq], o[:, tq:]], axis=0).T
        o_ref[...] = ot.astype(o_ref.dtype)


def _attention_call(q, k, v, mode, score_bound, extra=(), lam_init=0.0):
    b, seq, _ = q.shape
    qw = 2 * LANES if mode == "mla" else LANES
    nblk = q.shape[2] // qw
    tq, tk = ATT_TQ, ATT_TK
    assert seq % tk == 0 and (seq // tk - 2) % (2 * ATT_PAIRS) == 0 and seq % tq == 0 and tq % ATT_CW == 0
    shared = mode == "gqa"
    value_rows = LANES if mode == "diff" else LANES // 2
    kv_idx = (lambda bi, j, i: (bi, 0, 0)) if shared else (lambda bi, j, i: (bi, 0, j))
    in_specs = [pl.BlockSpec((None, tq, qw), lambda bi, j, i: (bi, i, j)),
                pl.BlockSpec((None, seq, qw), kv_idx),
                pl.BlockSpec((None, seq, LANES), kv_idx)]
    in_specs += [_const_spec(e.shape) for e in extra]

    def call(stabilised):
        if stabilised:
            scratch = [pltpu.VMEM((2, tk, 2 * tq), F32), pltpu.VMEM((1, 2 * tq), F32),
                       pltpu.VMEM((value_rows + SUM_ROWS, 2 * tq), F32)]
        else:
            scratch = [pltpu.VMEM((2, tk, 2 * tq), BF16), pltpu.VMEM((8, 2 * tq), F32),
                       pltpu.VMEM((value_rows, 2 * tq), F32)]
        return pl.pallas_call(
            functools.partial(_attention_kernel, mode=mode, lam_init=lam_init, stabilised=stabilised),
            grid=(b, nblk, seq // tq), in_specs=in_specs,
            out_specs=pl.BlockSpec((None, tq, LANES), lambda bi, j, i: (bi, i, j)),
            out_shape=jax.ShapeDtypeStruct((b, seq, nblk * LANES), BF16),
            scratch_shapes=[pltpu.VMEM((LANES, 2 * tq), BF16)] + scratch,
            compiler_params=_params(("parallel", "parallel", "arbitrary")),
            name="attn_" + mode + ("_stab" if stabilised else ""),
        )

    return lax.cond(score_bound <= SAFE_SCORE_RANGE, call(False), call(True), q, k, v, *extra)


def _score_bound(q_gain, k_gain, dim):
    return (jnp.max(jnp.abs(q_gain)) * jnp.max(jnp.abs(k_gain))
            * (dim * dim ** -0.5 * LOG2E * ROUNDING_SLACK))


def _rope_tables(seq):
    assert MLA_ROPE == GQA_DIM // 2 and seq % GRID_W == 0
    pos = jnp.arange(seq, dtype=jnp.int32)
    zeros = lambda n: jnp.zeros((seq, n), F32)

    def angles(dim):
        inv = ROPE_THETA ** (-jnp.arange(0, dim, 2, dtype=F32) / dim)
        ang = pos.astype(F32)[:, None] * inv[None, :]
        return lax.optimization_barrier((jnp.cos(ang), jnp.sin(ang)))

    c, s = angles(DIFF_D)
    full = (jnp.tile(jnp.concatenate([c, c], 1), (1, 2)), jnp.tile(jnp.concatenate([-s, s], 1), (1, 2)))
    c, s = angles(MLA_ROPE)
    tail = zeros(LANES - MLA_QK)
    mla = (jnp.concatenate([jnp.ones((seq, MLA_NOPE), F32), c, c, tail], 1),
           jnp.concatenate([zeros(MLA_NOPE), -s, s, tail], 1))
    rows = seq // GRID_W
    cr, sr = jnp.repeat(c[:rows], GRID_W, axis=0), jnp.repeat(s[:rows], GRID_W, axis=0)
    cc, sc = jnp.tile(c[:GRID_W], (rows, 1)), jnp.tile(s[:GRID_W], (rows, 1))
    axial = (jnp.tile(jnp.concatenate([cr, cr, cc, cc], 1), (1, 2)),
             jnp.tile(jnp.concatenate([-sr, sr, -sc, sc], 1), (1, 2)))
    return full, mla, axial


def kernel(x, ffn1_norm, ffn1_w_gu, ffn1_w_down, ffn2_norm, ffn2_w_gu, ffn2_w_down, ev_norm, ev_w_in, ev_sgu_norm, ev_w_s, ev_b_s, ev_q_norm, ev_k_norm, ev_lam_q1, ev_lam_k1, ev_lam_q2, ev_lam_k2, ev_sub_norm, ev_w_out, od_norm, od_w_in, od_cq_norm, od_ckv_norm, od_w_uq, od_w_ukv, od_mla_q_norm, od_mla_k_norm, od_gqa_q_norm, od_gqa_k_norm, od_w_out):
    b, seq, d = x.shape
    t = b * seq
    full_tab, mla_tab, axial_tab = _rope_tables(seq)
    x2 = x.reshape(t, d)
    ffn1 = (ffn1_norm[:, None, :], ffn1_w_gu.astype(BF16), ffn1_w_down.astype(BF16))
    ffn2 = (ffn2_norm[:, None, :], ffn2_w_gu.astype(BF16), ffn2_w_down.astype(BF16))

    x2 = _ffn_call(x2, 0, *ffn1)
    out_a, q, k, v = _even_proj_call(x2, seq, ev_norm[0], ev_w_in[0], ev_sgu_norm[0], ev_w_s[0],
                                     ev_b_s[0], ev_q_norm[0], ev_k_norm[0], full_tab)
    lam_init = 0.8 - 0.6 * float(np.exp(-0.3 * 0))
    lam_vecs = jnp.stack([ev_lam_q1[0], ev_lam_k1[0], ev_lam_q2[0], ev_lam_k2[0]])
    shp = (b, seq, GMLP_WIDTH)
    out_b = _attention_call(q.reshape(shp), k.reshape(shp), v.reshape(shp), "diff",
                            _score_bound(ev_q_norm[0], ev_k_norm[0], DIFF_D), extra=(lam_vecs, ev_sub_norm[0].reshape(DIFF_V, 1)), lam_init=lam_init)
    w_out = ev_w_out[0].astype(BF16)
    x2 = _ffn_call(x2, 0, *ffn2,
                   mix=(out_a, out_b.reshape(t, -1), w_out[:GMLP_WIDTH], w_out[GMLP_WIDTH:]))

    x2 = _ffn_call(x2, 1, *ffn1)
    qc, kc, vc, qd, kd, vd = _odd_proj_call(
        x2, seq, od_norm[0], od_w_in[0], od_cq_norm[0], od_ckv_norm[0], od_w_uq[0], od_w_ukv[0],
        od_mla_q_norm[0], od_mla_k_norm[0], od_gqa_q_norm[0], od_gqa_k_norm[0], mla_tab, axial_tab)
    r3 = lambda a: a.reshape(b, seq, a.shape[1])
    out_c = _attention_call(r3(qc), r3(kc), r3(vc), "mla",
                            _score_bound(od_mla_q_norm[0], od_mla_k_norm[0], MLA_QK))
    out_d = _attention_call(r3(qd), r3(kd), r3(vd), "gqa",
                            _score_bound(od_gqa_q_norm[0], od_gqa_k_norm[0], GQA_DIM))
    w_out = od_w_out[0].astype(BF16)
    n_c = MLA_HEADS * MLA_V
    w_d = w_out[n_c:].reshape(GQA_KV_HEADS, GQA_GROUP, GQA_DIM, d).transpose(1, 0, 2, 3)
    x2 = _ffn_call(x2, 1, *ffn2,
                   mix=(out_c.reshape(t, -1), out_d.reshape(t, -1), w_out[:n_c],
                        w_d.reshape(GQA_Q_HEADS * GQA_DIM, d)))
    return x2.reshape(b, seq, d)
```

```python
import functools
import math

import numpy as np
import jax
import jax.numpy as jnp
from jax import lax
from jax.experimental import pallas as pl
from jax.experimental.pallas import tpu as pltpu

D_MODEL = 1024
D_FF = 2816
ROPE_THETA = 10000.0
GRID_W = 64
EPS = 1e-6
GMLP_GROUPS = 8
GMLP_GROUP_DIM = 64
GMLP_CHUNK = 128
GMLP_WIDTH = GMLP_GROUPS * GMLP_GROUP_DIM
DIFF_HEADS = 4
DIFF_D = 64
DIFF_V = 128
MLA_HEADS = 8
MLA_Q_RANK = 256
MLA_KV_RANK = 128
MLA_NOPE = 64
MLA_ROPE = 32
MLA_V = 64
MLA_QK = MLA_NOPE + MLA_ROPE
GQA_Q_HEADS = 8
GQA_KV_HEADS = 2
GQA_GROUP = GQA_Q_HEADS // GQA_KV_HEADS
GQA_DIM = 64

LANES = 128
VMEM_LIMIT_BYTES = 56 * 1024 * 1024

FFN_TM = 1024
FFN_SUB = 256
FFN_CAST_CHUNKS = 11
PROJ_TM = 512
ATT_TQ = 2048
ATT_TK = 256
ATT_CW = 512
ATT_PAIRS = 3
SUM_ROWS = 16
LOG2E = math.log2(math.e)
SAFE_SCORE_RANGE = 60.0
ROUNDING_SLACK = 1.0 + 2.0 ** -6

BF16 = jnp.bfloat16
F32 = jnp.float32


def _params(semantics):
    return pltpu.CompilerParams(dimension_semantics=semantics,
                                vmem_limit_bytes=VMEM_LIMIT_BYTES)


def _const_spec(shape):
    nd = len(shape)
    return pl.BlockSpec(shape, lambda *_: (0,) * nd, pipeline_mode=pl.Buffered(1))


def _rms_rows(x, gain):
    ms = jnp.mean(x * x, axis=-1, keepdims=True)
    return x * lax.rsqrt(ms + EPS) * gain


def _dot(a, b):
    return jnp.dot(a, b, preferred_element_type=F32)


def _lane_is_low(shape):
    return lax.broadcasted_iota(jnp.int32, shape, len(shape) - 1) < (LANES // 2)


def _inv_rms_half_blocks(x):
    low = _lane_is_low(x.shape)
    x2 = x * x
    x2_lo = jnp.where(low, x2, 0.0)
    x2_hi = x2 - x2_lo
    ms_lo = jnp.sum(x2_lo, axis=-1, keepdims=True) * (2.0 / LANES)
    ms_hi = jnp.sum(x2_hi, axis=-1, keepdims=True) * (2.0 / LANES)
    return jnp.where(low, lax.rsqrt(ms_lo + EPS), lax.rsqrt(ms_hi + EPS))


def _inv_rms_padded_block(x, width):
    ms = jnp.sum(x * x, axis=-1, keepdims=True) * (1.0 / width)
    return lax.rsqrt(ms + EPS)


def _swap_pairs(a, group):
    shape = a.shape
    a = a.reshape(shape[:-1] + (shape[-1] // group, 2, group // 2))
    return jnp.flip(a, axis=-2).reshape(shape)


def _ffn_body(x, g_ref, wgu_ref, wd_ref, o_ref):
    for r in range(x.shape[0] // FFN_SUB):
        rows = slice(r * FFN_SUB, (r + 1) * FFN_SUB)
        xr = x[rows]
        xn = _rms_rows(xr, g_ref[...]).astype(BF16)
        h = _dot(xn, wgu_ref[...])
        gate = h[:, :D_FF]
        up = h[:, D_FF:]
        act = (gate / (1.0 + jnp.exp(-gate)) * up).astype(BF16)
        o_ref[rows, :] = xr + 0.5 * _dot(act, wd_ref[...])


def _load_ffn_weights(layer, wgu_hbm, wd_hbm, wgu_ref, wd_ref, gu_stage, d_stage, sem):
    gu_cols = 2 * D_FF // FFN_CAST_CHUNKS
    d_rows = D_FF // FFN_CAST_CHUNKS

    def gu_copy(c, slot):
        return pltpu.make_async_copy(wgu_hbm.at[layer, :, pl.ds(c * gu_cols, gu_cols)],
                                     gu_stage.at[slot], sem.at[0, slot])

    def d_copy(c, slot):
        return pltpu.make_async_copy(wd_hbm.at[layer, pl.ds(c * d_rows, d_rows), :],
                                     d_stage.at[slot], sem.at[1, slot])

    gu_copy(0, 0).start()
    d_copy(0, 0).start()
    for c in range(FFN_CAST_CHUNKS):
        slot = c % 2
        if c + 1 < FFN_CAST_CHUNKS:
            gu_copy(c + 1, 1 - slot).start()
            d_copy(c + 1, 1 - slot).start()
        gu_copy(c, slot).wait()
        wgu_ref[:, c * gu_cols:(c + 1) * gu_cols] = gu_stage[slot].astype(BF16)
        d_copy(c, slot).wait()
        wd_ref[c * d_rows:(c + 1) * d_rows, :] = d_stage[slot].astype(BF16)


def _ffn_kernel(x_ref, g_ref, wgu_hbm, wd_hbm, o_ref, wgu_ref, wd_ref, gu_stage, d_stage, sem, *, layer):
    @pl.when(pl.program_id(0) == 0)
    def _():
        _load_ffn_weights(layer, wgu_hbm, wd_hbm, wgu_ref, wd_ref, gu_stage, d_stage, sem)

    _ffn_body(x_ref[...], g_ref, wgu_ref, wd_ref, o_ref)


def _mix_ffn_kernel(x_ref, a_ref, b_ref, wa_ref, wb_ref, g_ref, wgu_hbm, wd_hbm, o_ref,
                    wgu_ref, wd_ref, gu_stage, d_stage, sem, *, layer):
    @pl.when(pl.program_id(0) == 0)
    def _():
        _load_ffn_weights(layer, wgu_hbm, wd_hbm, wgu_ref, wd_ref, gu_stage, d_stage, sem)

    x = x_ref[...] + _dot(a_ref[...], wa_ref[...]) + _dot(b_ref[...], wb_ref[...])
    _ffn_body(x, g_ref, wgu_ref, wd_ref, o_ref)


def _layer_spec(shape, layer):
    nd = len(shape) - 1
    return pl.BlockSpec((None,) + tuple(shape[1:]), lambda *_: (layer,) + (0,) * nd,
                        pipeline_mode=pl.Buffered(1))


def _ffn_call(x, layer, gains, w_gu, w_down, mix=None):
    t = x.shape[0]
    tm = FFN_TM
    row = lambda i: (i, 0)
    x_spec = pl.BlockSpec((tm, D_MODEL), row)
    hbm = pl.BlockSpec(memory_space=pl.ANY)
    w_args = [gains, w_gu, w_down]
    w_specs = [_layer_spec(gains.shape, layer), hbm, hbm]
    if mix is None:
        kern, in_specs, args = _ffn_kernel, [x_spec] + w_specs, [x] + w_args
    else:
        a, b, wa, wb = mix
        kern = _mix_ffn_kernel
        in_specs = [x_spec, pl.BlockSpec((tm, a.shape[1]), row), pl.BlockSpec((tm, b.shape[1]), row),
                    _const_spec(wa.shape), _const_spec(wb.shape)] + w_specs
        args = [x, a, b, wa, wb] + w_args
    return pl.pallas_call(
        functools.partial(kern, layer=layer), grid=(t // tm,), in_specs=in_specs,
        out_specs=pl.BlockSpec((tm, D_MODEL), row),
        out_shape=jax.ShapeDtypeStruct((t, D_MODEL), F32),
        scratch_shapes=[pltpu.VMEM((D_MODEL, 2 * D_FF), BF16), pltpu.VMEM((D_FF, D_MODEL), BF16),
                        pltpu.VMEM((2, D_MODEL, 2 * D_FF // FFN_CAST_CHUNKS), F32),
                        pltpu.VMEM((2, D_FF // FFN_CAST_CHUNKS, D_MODEL), F32),
                        pltpu.SemaphoreType.DMA((2, 2))],
        compiler_params=_params(("arbitrary",)),
        name="ffn" if mix is None else "mix_ffn",
    )(*args)


def _gelu_tanh(x):
    c = math.sqrt(2.0 / math.pi)
    return 0.5 * x * (1.0 + jnp.tanh(c * (x + 0.044715 * (x * x * x))))


def _even_proj_kernel(x_ref, g_ref, win_ref, sgu_ref, wpair_ref, bias_ref, qg_ref, kg_ref,
                      cos_ref, sin_ref, oa_ref, q_ref, k_ref, v_ref):
    xn = _rms_rows(x_ref[...], g_ref[...]).astype(BF16)
    proj = _dot(xn, win_ref[...])
    w = GMLP_WIDTH
    nblk = w // LANES
    tm = proj.shape[0]
    for j in range(nblk):
        u = _gelu_tanh(proj[:, j * LANES:(j + 1) * LANES])
        v = _gelu_tanh(proj[:, w + j * LANES:w + (j + 1) * LANES])
        vn = v * _inv_rms_half_blocks(v) * sgu_ref[:, j * LANES:(j + 1) * LANES]
        low = _lane_is_low(vn.shape)
        vn_lo = jnp.where(low, vn, 0.0).astype(BF16)
        vn_hi = jnp.where(low, 0.0, vn).astype(BF16)
        wp = wpair_ref[j]
        bias = bias_ref[:, j * LANES:(j + 1) * LANES]
        for c in range(tm // GMLP_CHUNK):
            rows = slice(c * GMLP_CHUNK, (c + 1) * GMLP_CHUNK)
            stacked = jnp.concatenate([vn_lo[rows], vn_hi[rows]], axis=0)
            mixed = _dot(wp, stacked) + bias
            oa_ref[rows, j * LANES:(j + 1) * LANES] = (u[rows] * mixed).astype(BF16)
    cos, sin = cos_ref[...], sin_ref[...]
    q_cos, q_sin = qg_ref[0:1] * cos, qg_ref[1:2] * sin
    k_cos, k_sin = kg_ref[0:1] * cos, kg_ref[1:2] * sin
    for j in range(nblk):
        blk = slice(j * LANES, (j + 1) * LANES)
        qb = proj[:, 2 * w + j * LANES:2 * w + (j + 1) * LANES]
        kb = proj[:, 3 * w + j * LANES:3 * w + (j + 1) * LANES]
        qp = proj[:, 5 * w + j * LANES:5 * w + (j + 1) * LANES]
        kp = proj[:, 6 * w + j * LANES:6 * w + (j + 1) * LANES]
        qr = _inv_rms_half_blocks(qb) * (LOG2E * DIFF_D ** -0.5)
        q_ref[:, blk] = ((qb * q_cos + qp * q_sin) * qr).astype(BF16)
        k_ref[:, blk] = ((kb * k_cos + kp * k_sin) * _inv_rms_half_blocks(kb)).astype(BF16)
    v_ref[...] = proj[:, 4 * w:5 * w].astype(BF16)


def _gain_pair(g, group):
    return jnp.stack([jnp.tile(g, 2), jnp.tile(_swap_pairs(g, group), 2)])


def _even_proj_call(x, seq, gain, w_in, sgu_norm, w_s, b_s, q_norm, k_norm, tables):
    t = x.shape[0]
    tm = PROJ_TM
    w = GMLP_WIDTH
    row = lambda i: (i, 0)
    pos = lambda i: (i % (seq // tm), 0)
    wpair = jnp.concatenate([w_s[0::2], w_s[1::2]], axis=2).astype(BF16)
    bias = jnp.repeat(b_s.T, GMLP_GROUP_DIM, axis=1)
    qk_cols = w_in[:, 2 * w:4 * w]
    win = jnp.concatenate([w_in, _swap_pairs(qk_cols, DIFF_D)], axis=1).astype(BF16)
    args = [x, gain.reshape(1, D_MODEL), win, sgu_norm.reshape(1, w), wpair, bias,
            _gain_pair(q_norm, DIFF_D), _gain_pair(k_norm, DIFF_D), *tables]
    in_specs = [pl.BlockSpec((tm, D_MODEL), row)] + [_const_spec(a.shape) for a in args[1:8]]
    in_specs += [pl.BlockSpec((tm, LANES), pos)] * 2
    out = jax.ShapeDtypeStruct((t, w), BF16)
    return pl.pallas_call(
        _even_proj_kernel, grid=(t // tm,), in_specs=in_specs,
        out_specs=[pl.BlockSpec((tm, w), row)] * 4, out_shape=[out] * 4,
        compiler_params=_params(("parallel",)), name="even_proj",
    )(*args)


def _odd_proj_kernel(x_ref, g_ref, win_ref, cqg_ref, ckvg_ref, wuq_ref, wuk_ref, wuv_ref,
                     mqg_ref, mkg_ref, gqg_ref, gkg_ref, mcos_ref, msin_ref, acos_ref, asin_ref,
                     qc_ref, kc_ref, vc_ref, qd_ref, kd_ref, vd_ref):
    xn = _rms_rows(x_ref[...], g_ref[...]).astype(BF16)
    proj = _dot(xn, win_ref[...])
    nq = GQA_Q_HEADS * GQA_DIM
    o1 = MLA_Q_RANK
    o2 = o1 + MLA_KV_RANK
    o3 = o2 + 2 * LANES
    o4 = o3 + 2 * nq
    o5 = o4 + 2 * LANES
    mcos, msin = mcos_ref[...], msin_ref[...]
    q_cos, q_sin = mqg_ref[0:1] * mcos, mqg_ref[1:2] * msin
    k_cos, k_sin = mkg_ref[0:1] * mcos, mkg_ref[1:2] * msin
    cq = _rms_rows(proj[:, :o1], cqg_ref[...]).astype(BF16)
    q_all = _dot(cq, wuq_ref[...])
    ckv = _rms_rows(proj[:, o1:o2], ckvg_ref[...]).astype(BF16)
    kn_all = _dot(ckv, wuk_ref[...])
    vc_ref[...] = _dot(ckv, wuv_ref[...]).astype(BF16)
    kpe = proj[:, o2:o2 + LANES]
    kpe_sin = proj[:, o2 + LANES:o3] * k_sin
    nh = MLA_HEADS * LANES
    for h in range(MLA_HEADS):
        blk = slice(h * LANES, (h + 1) * LANES)
        qb = q_all[:, blk]
        qp = q_all[:, nh + h * LANES:nh + (h + 1) * LANES]
        qr = _inv_rms_padded_block(qb, MLA_QK) * (LOG2E * MLA_QK ** -0.5)
        qc_ref[:, blk] = ((qb * q_cos + qp * q_sin) * qr).astype(BF16)
        kb = kn_all[:, blk] + kpe
        kc_ref[:, blk] = ((kb * k_cos + kpe_sin) * _inv_rms_padded_block(kb, MLA_QK)).astype(BF16)
    acos, asin = acos_ref[...], asin_ref[...]
    q_cos, q_sin = gqg_ref[0:1] * acos, gqg_ref[1:2] * asin
    for j in range(nq // LANES):
        blk = slice(j * LANES, (j + 1) * LANES)
        qb = proj[:, o3 + j * LANES:o3 + (j + 1) * LANES]
        qp = proj[:, o3 + nq + j * LANES:o3 + nq + (j + 1) * LANES]
        qr = _inv_rms_half_blocks(qb) * (LOG2E * GQA_DIM ** -0.5)
        qd_ref[:, blk] = ((qb * q_cos + qp * q_sin) * qr).astype(BF16)
    kb = proj[:, o4:o4 + LANES]
    kp = proj[:, o4 + LANES:o5]
    kd_ref[...] = ((kb * (gkg_ref[0:1] * acos) + kp * (gkg_ref[1:2] * asin))
                   * _inv_rms_half_blocks(kb)).astype(BF16)
    vd_ref[...] = proj[:, o5:].astype(BF16)


def _pad_heads(w, heads, width):
    r = w.shape[0]
    w = w.reshape(r, heads, width)
    return jnp.pad(w, ((0, 0), (0, 0), (0, LANES - width))).reshape(r, heads * LANES)


def _rope_partner_cols(w, heads):
    r = w.shape[0]
    w = w.reshape(r, heads, MLA_QK)
    rope = _swap_pairs(w[:, :, MLA_NOPE:], MLA_ROPE)
    return jnp.concatenate([jnp.zeros_like(w[:, :, :MLA_NOPE]), rope], axis=2).reshape(r, heads * MLA_QK)


def _odd_proj_call(x, seq, gain, w_in, cq_norm, ckv_norm, w_uq, w_ukv, mq_norm, mk_norm,
                   gq_norm, gk_norm, mla_tables, axial_tables):
    t = x.shape[0]
    tm = PROJ_TM
    row = lambda i: (i, 0)
    pos = lambda i: (i % (seq // tm), 0)
    o1 = MLA_Q_RANK
    o2 = o1 + MLA_KV_RANK
    o3 = o2 + MLA_ROPE
    o4 = o3 + GQA_Q_HEADS * GQA_DIM
    o5 = o4 + GQA_KV_HEADS * GQA_DIM
    place_rope = lambda c: jnp.pad(c, ((0, 0), (MLA_NOPE, LANES - MLA_QK)))
    kpe_cols = w_in[:, o2:o3]
    gq_cols = w_in[:, o3:o4].reshape(D_MODEL, GQA_KV_HEADS, GQA_GROUP, GQA_DIM)
    gq_cols = gq_cols.transpose(0, 2, 1, 3).reshape(D_MODEL, GQA_Q_HEADS * GQA_DIM)
    gk_cols = w_in[:, o4:o5]
    half = GQA_DIM // 2
    win = jnp.concatenate(
        [w_in[:, :o2], place_rope(kpe_cols), place_rope(_swap_pairs(kpe_cols, MLA_ROPE)),
         gq_cols, _swap_pairs(gq_cols, half), gk_cols, _swap_pairs(gk_cols, half), w_in[:, o5:]],
        axis=1).astype(BF16)
    wuq = jnp.concatenate([_pad_heads(w_uq, MLA_HEADS, MLA_QK),
                           _pad_heads(_rope_partner_cols(w_uq, MLA_HEADS), MLA_HEADS, MLA_QK)],
                          axis=1).astype(BF16)
    w_ukv = w_ukv.reshape(MLA_KV_RANK, MLA_HEADS, MLA_NOPE + MLA_V)
    wuk = _pad_heads(w_ukv[:, :, :MLA_NOPE].reshape(MLA_KV_RANK, -1), MLA_HEADS, MLA_NOPE).astype(BF16)
    wuv = w_ukv[:, :, MLA_NOPE:].reshape(MLA_KV_RANK, MLA_HEADS * MLA_V).astype(BF16)

    def mla_gain_pair(g):
        partner = jnp.concatenate([g[:MLA_NOPE], _swap_pairs(g[MLA_NOPE:], MLA_ROPE)])
        return jnp.pad(jnp.stack([g, partner]), ((0, 0), (0, LANES - MLA_QK)))

    args = [x, gain.reshape(1, D_MODEL), win, cq_norm.reshape(1, -1), ckv_norm.reshape(1, -1),
            wuq, wuk, wuv, mla_gain_pair(mq_norm), mla_gain_pair(mk_norm),
            _gain_pair(gq_norm, half), _gain_pair(gk_norm, half), *mla_tables, *axial_tables]
    in_specs = ([pl.BlockSpec((tm, D_MODEL), row)]
                + [_const_spec(a.shape) for a in args[1:12]]
                + [pl.BlockSpec((tm, LANES), pos)] * 4)
    widths = [MLA_HEADS * LANES, MLA_HEADS * LANES, MLA_HEADS * MLA_V,
              GQA_Q_HEADS * GQA_DIM, LANES, LANES]
    return pl.pallas_call(
        _odd_proj_kernel, grid=(t // tm,), in_specs=in_specs,
        out_specs=[pl.BlockSpec((tm, n), row) for n in widths],
        out_shape=[jax.ShapeDtypeStruct((t, n), BF16) for n in widths],
        compiler_params=_params(("parallel",)), name="odd_proj",
    )(*args)


def _attention_kernel(*refs, mode, lam_init, stabilised):
    if mode == "diff":
        q_ref, k_ref, v_ref, lam_ref, sub_ref, o_ref = refs[:6]
    else:
        q_ref, k_ref, v_ref, o_ref = refs[:4]
    qt_ref, tile_buf, stat_ref, acc_ref = refs[-4:]
    tq = q_ref.shape[0]
    seq = k_ref.shape[0]
    tk = ATT_TK
    cw = ATT_CW
    n = seq // tk
    q = q_ref[...].astype(F32)
    if mode == "mla":
        qt_ref[:, :tq] = q[:, :LANES].T.astype(BF16)
        qt_ref[:, tq:] = q[:, LANES:].T.astype(BF16)
    else:
        qt = q.T
        first = lax.broadcasted_iota(jnp.int32, qt.shape, 0) < (LANES // 2)
        qt_ref[:, :tq] = jnp.where(first, qt, 0.0).astype(BF16)
        qt_ref[:, tq:] = jnp.where(first, 0.0, qt).astype(BF16)

    chunks = [slice(c * cw, (c + 1) * cw) for c in range(2 * tq // cw)]

    def tile_rows(t):
        return pl.ds(pl.multiple_of(t * tk, tk), tk)

    def ahead(kt, slot, cols):
        kc = kt
        if mode == "mla":
            kc = kt[:, :LANES] if cols.start < tq else kt[:, LANES:]
        s = _dot(kc, qt_ref[:, cols])
        if stabilised:
            tile_buf[slot, :, cols] = s
        else:
            p = jnp.exp2(s)
            stat_ref[:, cols] += jnp.sum(p.reshape(tk // 8, 8, p.shape[1]), axis=0)
            tile_buf[slot, :, cols] = p.astype(BF16)

    def consume(vt, slot, cols):
        if stabilised:
            s = tile_buf[slot, :, cols]
            m_old = stat_ref[:, cols]
            m_new = jnp.maximum(m_old, jnp.max(s, axis=0, keepdims=True))
            alpha = jnp.exp2(m_old - m_new)
            p = jnp.exp2(s - m_new).astype(BF16)
            stat_ref[:, cols] = m_new
            acc_ref[:, cols] = alpha * acc_ref[:, cols] + _dot(vt, p)
        else:
            acc_ref[:, cols] += _dot(vt, tile_buf[slot, :, cols])

    def value_rows(t):
        vt = v_ref[tile_rows(t), :].T
        parts = (vt, vt) if mode == "diff" else (vt[:LANES // 2], vt[LANES // 2:])
        if not stabilised:
            return parts
        ones = jnp.ones((SUM_ROWS, tk), BF16)
        return tuple(jnp.concatenate([part, ones], axis=0) for part in parts)

    def half_step(t, t_next, cur, nxt):
        kt = k_ref[tile_rows(t_next), :]
        vt = value_rows(t)
        for cols in chunks:
            ahead(kt, nxt, cols)
            consume(vt[0] if cols.start < tq else vt[1], cur, cols)

    stat_ref[...] = jnp.full(stat_ref.shape, -jnp.inf if stabilised else 0.0, F32)
    acc_ref[...] = jnp.zeros(acc_ref.shape, F32)
    kt0 = k_ref[tile_rows(0), :]
    for cols in chunks:
        ahead(kt0, 0, cols)

    def body(i, carry):
        for u in range(ATT_PAIRS):
            t = 2 * (i * ATT_PAIRS + u)
            half_step(t, t + 1, 0, 1)
            half_step(t + 1, t + 2, 1, 0)
        return carry

    lax.fori_loop(0, (n - 2) // (2 * ATT_PAIRS), body, 0)
    half_step(n - 2, n - 1, 0, 1)
    vt_last = value_rows(n - 1)
    for cols in chunks:
        consume(vt_last[0] if cols.start < tq else vt_last[1], 1, cols)

    if stabilised:
        nv = acc_ref.shape[0] - SUM_ROWS
        o = acc_ref[:nv, :] / acc_ref[nv:nv + 1, :]
    else:
        o = acc_ref[...] / jnp.sum(stat_ref[...], axis=0, keepdims=True)
    if mode == "diff":
        lv = lam_ref[...]
        lam = (jnp.exp(jnp.sum(lv[0:1] * lv[1:2], axis=-1, keepdims=True))
               - jnp.exp(jnp.sum(lv[2:3] * lv[3:4], axis=-1, keepdims=True)) + lam_init)
        ot = o[:, :tq] - lam * o[:, tq:]
        ms = jnp.mean(ot * ot, axis=0, keepdims=True)
        ot = ot * lax.rsqrt(ms + EPS) * (sub_ref[...] * (1.0 - lam_init))
        o_ref[...] = ot.T.astype(o_ref.dtype)
    else:
        ot = jnp.concatenate([o[:, :tq], o[:, tq:]], axis=0).T
        o_ref[...] = ot.astype(o_ref.dtype)


def _attention_call(q, k, v, mode, score_bound, extra=(), lam_init=0.0):
    b, seq, _ = q.shape
    qw = 2 * LANES if mode == "mla" else LANES
    nblk = q.shape[2] // qw
    tq, tk = ATT_TQ, ATT_TK
    assert seq % tk == 0 and (seq // tk - 2) % (2 * ATT_PAIRS) == 0 and seq % tq == 0 and tq % ATT_CW == 0
    shared = mode == "gqa"
    value_rows = LANES if mode == "diff" else LANES // 2
    kv_idx = (lambda bi, j, i: (bi, 0, 0)) if shared else (lambda bi, j, i: (bi, 0, j))
    in_specs = [pl.BlockSpec((None, tq, qw), lambda bi, j, i: (bi, i, j)),
                pl.BlockSpec((None, seq, qw), kv_idx),
                pl.BlockSpec((None, seq, LANES), kv_idx)]
    in_specs += [_const_spec(e.shape) for e in extra]

    def call(stabilised):
        if stabilised:
            scratch = [pltpu.VMEM((2, tk, 2 * tq), F32), pltpu.VMEM((1, 2 * tq), F32),
                       pltpu.VMEM((value_rows + SUM_ROWS, 2 * tq), F32)]
        else:
            scratch = [pltpu.VMEM((2, tk, 2 * tq), BF16), pltpu.VMEM((8, 2 * tq), F32),
                       pltpu.VMEM((value_rows, 2 * tq), F32)]
        return pl.pallas_call(
            functools.partial(_attention_kernel, mode=mode, lam_init=lam_init, stabilised=stabilised),
            grid=(b, nblk, seq // tq), in_specs=in_specs,
            out_specs=pl.BlockSpec((None, tq, LANES), lambda bi, j, i: (bi, i, j)),
            out_shape=jax.ShapeDtypeStruct((b, seq, nblk * LANES), BF16),
            scratch_shapes=[pltpu.VMEM((LANES, 2 * tq), BF16)] + scratch,
            compiler_params=_params(("parallel", "parallel", "arbitrary")),
            name="attn_" + mode + ("_stab" if stabilised else ""),
        )

    return lax.cond(score_bound <= SAFE_SCORE_RANGE, call(False), call(True), q, k, v, *extra)


def _score_bound(q_gain, k_gain, dim):
    return (jnp.max(jnp.abs(q_gain)) * jnp.max(jnp.abs(k_gain))
            * (dim * dim ** -0.5 * LOG2E * ROUNDING_SLACK))


def _rope_tables(seq):
    assert MLA_ROPE == GQA_DIM // 2 and seq % GRID_W == 0
    pos = jnp.arange(seq, dtype=jnp.int32)
    zeros = lambda n: jnp.zeros((seq, n), F32)

    def angles(dim):
        inv = ROPE_THETA ** (-jnp.arange(0, dim, 2, dtype=F32) / dim)
        ang = pos.astype(F32)[:, None] * inv[None, :]
        return lax.optimization_barrier((jnp.cos(ang), jnp.sin(ang)))

    c, s = angles(DIFF_D)
    full = (jnp.tile(jnp.concatenate([c, c], 1), (1, 2)), jnp.tile(jnp.concatenate([-s, s], 1), (1, 2)))
    c, s = angles(MLA_ROPE)
    tail = zeros(LANES - MLA_QK)
    mla = (jnp.concatenate([jnp.ones((seq, MLA_NOPE), F32), c, c, tail], 1),
           jnp.concatenate([zeros(MLA_NOPE), -s, s, tail], 1))
    rows = seq // GRID_W
    cr, sr = jnp.repeat(c[:rows], GRID_W, axis=0), jnp.repeat(s[:rows], GRID_W, axis=0)
    cc, sc = jnp.tile(c[:GRID_W], (rows, 1)), jnp.tile(s[:GRID_W], (rows, 1))
    axial = (jnp.tile(jnp.concatenate([cr, cr, cc, cc], 1), (1, 2)),
             jnp.tile(jnp.concatenate([-sr, sr, -sc, sc], 1), (1, 2)))
    return full, mla, axial


def kernel(x, ffn1_norm, ffn1_w_gu, ffn1_w_down, ffn2_norm, ffn2_w_gu, ffn2_w_down, ev_norm, ev_w_in, ev_sgu_norm, ev_w_s, ev_b_s, ev_q_norm, ev_k_norm, ev_lam_q1, ev_lam_k1, ev_lam_q2, ev_lam_k2, ev_sub_norm, ev_w_out, od_norm, od_w_in, od_cq_norm, od_ckv_norm, od_w_uq, od_w_ukv, od_mla_q_norm, od_mla_k_norm, od_gqa_q_norm, od_gqa_k_norm, od_w_out):
    b, seq, d = x.shape
    t = b * seq
    full_tab, mla_tab, axial_tab = _rope_tables(seq)
    x2 = x.reshape(t, d)
    ffn1 = (ffn1_norm[:, None, :], ffn1_w_gu, ffn1_w_down)
    ffn2 = (ffn2_norm[:, None, :], ffn2_w_gu, ffn2_w_down)

    x2 = _ffn_call(x2, 0, *ffn1)
    out_a, q, k, v = _even_proj_call(x2, seq, ev_norm[0], ev_w_in[0], ev_sgu_norm[0], ev_w_s[0],
                                     ev_b_s[0], ev_q_norm[0], ev_k_norm[0], full_tab)
    lam_init = 0.8 - 0.6 * float(np.exp(-0.3 * 0))
    lam_vecs = jnp.stack([ev_lam_q1[0], ev_lam_k1[0], ev_lam_q2[0], ev_lam_k2[0]])
    shp = (b, seq, GMLP_WIDTH)
    out_b = _attention_call(q.reshape(shp), k.reshape(shp), v.reshape(shp), "diff",
                            _score_bound(ev_q_norm[0], ev_k_norm[0], DIFF_D), extra=(lam_vecs, ev_sub_norm[0].reshape(DIFF_V, 1)), lam_init=lam_init)
    w_out = ev_w_out[0].astype(BF16)
    x2 = _ffn_call(x2, 0, *ffn2,
                   mix=(out_a, out_b.reshape(t, -1), w_out[:GMLP_WIDTH], w_out[GMLP_WIDTH:]))

    x2 = _ffn_call(x2, 1, *ffn1)
    qc, kc, vc, qd, kd, vd = _odd_proj_call(
        x2, seq, od_norm[0], od_w_in[0], od_cq_norm[0], od_ckv_norm[0], od_w_uq[0], od_w_ukv[0],
        od_mla_q_norm[0], od_mla_k_norm[0], od_gqa_q_norm[0], od_gqa_k_norm[0], mla_tab, axial_tab)
    r3 = lambda a: a.reshape(b, seq, a.shape[1])
    out_c = _attention_call(r3(qc), r3(kc), r3(vc), "mla",
                            _score_bound(od_mla_q_norm[0], od_mla_k_norm[0], MLA_QK))
    out_d = _attention_call(r3(qd), r3(kd), r3(vd), "gqa",
                            _score_bound(od_gqa_q_norm[0], od_gqa_k_norm[0], GQA_DIM))
    w_out = od_w_out[0].astype(BF16)
    n_c = MLA_HEADS * MLA_V
    w_d = w_out[n_c:].reshape(GQA_KV_HEADS, GQA_GROUP, GQA_DIM, d).transpose(1, 0, 2, 3)
    x2 = _ffn_call(x2, 1, *ffn2,
                   mix=(out_c.reshape(t, -1), out_d.reshape(t, -1), w_out[:n_c],
                        w_d.reshape(GQA_Q_HEADS * GQA_DIM, d)))
    return x2.reshape(b, seq, d)
```

```python
import functools
import math

import numpy as np
import jax
import jax.numpy as jnp
from jax import lax
from jax.experimental import pallas as pl
from jax.experimental.pallas import tpu as pltpu

D_MODEL = 1024
D_FF = 2816
ROPE_THETA = 10000.0
GRID_W = 64
EPS = 1e-6
GMLP_GROUPS = 8
GMLP_GROUP_DIM = 64
GMLP_CHUNK = 128
GMLP_WIDTH = GMLP_GROUPS * GMLP_GROUP_DIM
DIFF_D = 64
DIFF_V = 128
MLA_HEADS = 8
MLA_Q_RANK = 256
MLA_KV_RANK = 128
MLA_NOPE = 64
MLA_ROPE = 32
MLA_V = 64
MLA_QK = MLA_NOPE + MLA_ROPE
GQA_Q_HEADS = 8
GQA_KV_HEADS = 2
GQA_GROUP = GQA_Q_HEADS // GQA_KV_HEADS
GQA_DIM = 64

LANES = 128
VMEM_LIMIT_BYTES = 56 * 1024 * 1024

FFN_TM = 1024
FFN_SUB = 256
FFN_CAST_CHUNKS = 11
PROJ_TM = 512
ATT_TQ = 2048
ATT_TK = 256
ATT_CW = 512
ATT_PAIRS = 3
SUM_ROWS = 16
LOG2E = math.log2(math.e)
SAFE_SCORE_RANGE = 60.0
ROUNDING_SLACK = 1.0 + 2.0 ** -6

BF16 = jnp.bfloat16
F32 = jnp.float32


def _params(semantics):
    return pltpu.CompilerParams(dimension_semantics=semantics,
                                vmem_limit_bytes=VMEM_LIMIT_BYTES)


def _const_spec(shape):
    nd = len(shape)
    return pl.BlockSpec(shape, lambda *_: (0,) * nd, pipeline_mode=pl.Buffered(1))


def _rms_rows(x, gain):
    ms = jnp.mean(x * x, axis=-1, keepdims=True)
    return x * lax.rsqrt(ms + EPS) * gain


def _dot(a, b):
    return jnp.dot(a, b, preferred_element_type=F32)


def _lane_is_low(shape):
    return lax.broadcasted_iota(jnp.int32, shape, len(shape) - 1) < (LANES // 2)


def _inv_rms_half_blocks(x):
    low = _lane_is_low(x.shape)
    x2 = x * x
    x2_lo = jnp.where(low, x2, 0.0)
    x2_hi = x2 - x2_lo
    ms_lo = jnp.sum(x2_lo, axis=-1, keepdims=True) * (2.0 / LANES)
    ms_hi = jnp.sum(x2_hi, axis=-1, keepdims=True) * (2.0 / LANES)
    return jnp.where(low, lax.rsqrt(ms_lo + EPS), lax.rsqrt(ms_hi + EPS))


def _inv_rms_padded_block(x, width):
    ms = jnp.sum(x * x, axis=-1, keepdims=True) * (1.0 / width)
    return lax.rsqrt(ms + EPS)


def _swap_pairs(a, group):
    shape = a.shape
    a = a.reshape(shape[:-1] + (shape[-1] // group, 2, group // 2))
    return jnp.flip(a, axis=-2).reshape(shape)


def _ffn_body(x, g_ref, wgu_ref, wd_ref, o_ref):
    for r in range(x.shape[0] // FFN_SUB):
        rows = slice(r * FFN_SUB, (r + 1) * FFN_SUB)
        xr = x[rows]
        xn = _rms_rows(xr, g_ref[...]).astype(BF16)
        h = _dot(xn, wgu_ref[...])
        gate = h[:, :D_FF]
        up = h[:, D_FF:]
        act = (gate / (1.0 + jnp.exp(-gate)) * up).astype(BF16)
        o_ref[rows, :] = xr + 0.5 * _dot(act, wd_ref[...])


def _load_ffn_weights(layer, wgu_hbm, wd_hbm, wgu_ref, wd_ref, gu_stage, d_stage, sem):
    gu_cols = 2 * D_FF // FFN_CAST_CHUNKS
    d_rows = D_FF // FFN_CAST_CHUNKS

    def gu_copy(c, slot):
        return pltpu.make_async_copy(wgu_hbm.at[layer, :, pl.ds(c * gu_cols, gu_cols)],
                                     gu_stage.at[slot], sem.at[0, slot])

    def d_copy(c, slot):
        return pltpu.make_async_copy(wd_hbm.at[layer, pl.ds(c * d_rows, d_rows), :],
                                     d_stage.at[slot], sem.at[1, slot])

    gu_copy(0, 0).start()
    d_copy(0, 0).start()
    for c in range(FFN_CAST_CHUNKS):
        slot = c % 2
        if c + 1 < FFN_CAST_CHUNKS:
            gu_copy(c + 1, 1 - slot).start()
            d_copy(c + 1, 1 - slot).start()
        gu_copy(c, slot).wait()
        wgu_ref[:, c * gu_cols:(c + 1) * gu_cols] = gu_stage[slot].astype(BF16)
        d_copy(c, slot).wait()
        wd_ref[c * d_rows:(c + 1) * d_rows, :] = d_stage[slot].astype(BF16)


def _ffn_kernel(x_ref, g_ref, wgu_hbm, wd_hbm, o_ref, wgu_ref, wd_ref, gu_stage, d_stage, sem, *, layer):
    @pl.when(pl.program_id(0) == 0)
    def _():
        _load_ffn_weights(layer, wgu_hbm, wd_hbm, wgu_ref, wd_ref, gu_stage, d_stage, sem)

    _ffn_body(x_ref[...], g_ref, wgu_ref, wd_ref, o_ref)


def _mix_ffn_kernel(x_ref, a_ref, b_ref, wa_ref, wb_ref, g_ref, wgu_hbm, wd_hbm, o_ref,
                    wgu_ref, wd_ref, gu_stage, d_stage, sem, *, layer):
    @pl.when(pl.program_id(0) == 0)
    def _():
        _load_ffn_weights(layer, wgu_hbm, wd_hbm, wgu_ref, wd_ref, gu_stage, d_stage, sem)

    x = x_ref[...] + _dot(a_ref[...], wa_ref[...]) + _dot(b_ref[...], wb_ref[...])
    _ffn_body(x, g_ref, wgu_ref, wd_ref, o_ref)


def _layer_spec(shape, layer):
    nd = len(shape) - 1
    return pl.BlockSpec((None,) + tuple(shape[1:]), lambda *_: (layer,) + (0,) * nd,
                        pipeline_mode=pl.Buffered(1))


def _ffn_call(x, layer, gains, w_gu, w_down, mix=None):
    t = x.shape[0]
    tm = FFN_TM
    row = lambda i: (i, 0)
    x_spec = pl.BlockSpec((tm, D_MODEL), row)
    hbm = pl.BlockSpec(memory_space=pl.ANY)
    w_args = [gains, w_gu, w_down]
    w_specs = [_layer_spec(gains.shape, layer), hbm, hbm]
    if mix is None:
        kern, in_specs, args = _ffn_kernel, [x_spec] + w_specs, [x] + w_args
    else:
        a, b, wa, wb = mix
        kern = _mix_ffn_kernel
        in_specs = [x_spec, pl.BlockSpec((tm, a.shape[1]), row), pl.BlockSpec((tm, b.shape[1]), row),
                    _const_spec(wa.shape), _const_spec(wb.shape)] + w_specs
        args = [x, a, b, wa, wb] + w_args
    return pl.pallas_call(
        functools.partial(kern, layer=layer), grid=(t // tm,), in_specs=in_specs,
        out_specs=pl.BlockSpec((tm, D_MODEL), row),
        out_shape=jax.ShapeDtypeStruct((t, D_MODEL), F32),
        scratch_shapes=[pltpu.VMEM((D_MODEL, 2 * D_FF), BF16), pltpu.VMEM((D_FF, D_MODEL), BF16),
                        pltpu.VMEM((2, D_MODEL, 2 * D_FF // FFN_CAST_CHUNKS), F32),
                        pltpu.VMEM((2, D_FF // FFN_CAST_CHUNKS, D_MODEL), F32),
                        pltpu.SemaphoreType.DMA((2, 2))],
        compiler_params=_params(("arbitrary",)),
        name="ffn" if mix is None else "mix_ffn",
    )(*args)


def _gelu_tanh(x):
    c = math.sqrt(2.0 / math.pi)
    return 0.5 * x * (1.0 + jnp.tanh(c * (x + 0.044715 * (x * x * x))))


def _even_proj_kernel(x_ref, g_ref, win_ref, sgu_ref, wpair_ref, bias_ref, qg_ref, kg_ref,
                      cos_ref, sin_ref, oa_ref, q_ref, k_ref, v_ref):
    xn = _rms_rows(x_ref[...], g_ref[...]).astype(BF16)
    proj = _dot(xn, win_ref[...])
    w = GMLP_WIDTH
    nblk = w // LANES
    tm = proj.shape[0]
    for j in range(nblk):
        u = _gelu_tanh(proj[:, j * LANES:(j + 1) * LANES])
        v = _gelu_tanh(proj[:, w + j * LANES:w + (j + 1) * LANES])
        vn = v * _inv_rms_half_blocks(v) * sgu_ref[:, j * LANES:(j + 1) * LANES]
        low = _lane_is_low(vn.shape)
        vn_lo = jnp.where(low, vn, 0.0).astype(BF16)
        vn_hi = jnp.where(low, 0.0, vn).astype(BF16)
        wp = wpair_ref[j]
        bias = bias_ref[:, j * LANES:(j + 1) * LANES]
        for c in range(tm // GMLP_CHUNK):
            rows = slice(c * GMLP_CHUNK, (c + 1) * GMLP_CHUNK)
            stacked = jnp.concatenate([vn_lo[rows], vn_hi[rows]], axis=0)
            mixed = _dot(wp, stacked) + bias
            oa_ref[rows, j * LANES:(j + 1) * LANES] = (u[rows] * mixed).astype(BF16)
    cos, sin = cos_ref[...], sin_ref[...]
    q_cos, q_sin = qg_ref[0:1] * cos, qg_ref[1:2] * sin
    k_cos, k_sin = kg_ref[0:1] * cos, kg_ref[1:2] * sin
    for j in range(nblk):
        blk = slice(j * LANES, (j + 1) * LANES)
        qb = proj[:, 2 * w + j * LANES:2 * w + (j + 1) * LANES]
        kb = proj[:, 3 * w + j * LANES:3 * w + (j + 1) * LANES]
        qp = proj[:, 5 * w + j * LANES:5 * w + (j + 1) * LANES]
        kp = proj[:, 6 * w + j * LANES:6 * w + (j + 1) * LANES]
        qr = _inv_rms_half_blocks(qb) * (LOG2E * DIFF_D ** -0.5)
        q_ref[:, blk] = ((qb * q_cos + qp * q_sin) * qr).astype(BF16)
        k_ref[:, blk] = ((kb * k_cos + kp * k_sin) * _inv_rms_half_blocks(kb)).astype(BF16)
    v_ref[...] = proj[:, 4 * w:5 * w].astype(BF16)


def _gain_pair(g, group):
    return jnp.stack([jnp.tile(g, 2), jnp.tile(_swap_pairs(g, group), 2)])


def _even_proj_call(x, seq, gain, w_in, sgu_norm, w_s, b_s, q_norm, k_norm, tables):
    t = x.shape[0]
    tm = PROJ_TM
    w = GMLP_WIDTH
    row = lambda i: (i, 0)
    pos = lambda i: (i % (seq // tm), 0)
    wpair = jnp.concatenate([w_s[0::2], w_s[1::2]], axis=2).astype(BF16)
    bias = jnp.repeat(b_s.T, GMLP_GROUP_DIM, axis=1)
    qk_cols = w_in[:, 2 * w:4 * w]
    win = jnp.concatenate([w_in, _swap_pairs(qk_cols, DIFF_D)], axis=1).astype(BF16)
    args = [x, gain.reshape(1, D_MODEL), win, sgu_norm.reshape(1, w), wpair, bias,
            _gain_pair(q_norm, DIFF_D), _gain_pair(k_norm, DIFF_D), *tables]
    in_specs = [pl.BlockSpec((tm, D_MODEL), row)] + [_const_spec(a.shape) for a in args[1:8]]
    in_specs += [pl.BlockSpec((tm, LANES), pos)] * 2
    out = jax.ShapeDtypeStruct((t, w), BF16)
    return pl.pallas_call(
        _even_proj_kernel, grid=(t // tm,), in_specs=in_specs,
        out_specs=[pl.BlockSpec((tm, w), row)] * 4, out_shape=[out] * 4,
        compiler_params=_params(("parallel",)), name="even_proj",
    )(*args)


def _odd_proj_kernel(x_ref, g_ref, win_ref, cqg_ref, ckvg_ref, wuq_ref, wuk_ref, wuv_ref,
                     mqg_ref, mkg_ref, gqg_ref, gkg_ref, mcos_ref, msin_ref, acos_ref, asin_ref,
                     qc_ref, kc_ref, vc_ref, qd_ref, kd_ref, vd_ref):
    xn = _rms_rows(x_ref[...], g_ref[...]).astype(BF16)
    proj = _dot(xn, win_ref[...])
    nq = GQA_Q_HEADS * GQA_DIM
    o1 = MLA_Q_RANK
    o2 = o1 + MLA_KV_RANK
    o3 = o2 + 2 * LANES
    o4 = o3 + 2 * nq
    o5 = o4 + 2 * LANES
    mcos, msin = mcos_ref[...], msin_ref[...]
    q_cos, q_sin = mqg_ref[0:1] * mcos, mqg_ref[1:2] * msin
    k_cos, k_sin = mkg_ref[0:1] * mcos, mkg_ref[1:2] * msin
    cq = _rms_rows(proj[:, :o1], cqg_ref[...]).astype(BF16)
    q_all = _dot(cq, wuq_ref[...])
    ckv = _rms_rows(proj[:, o1:o2], ckvg_ref[...]).astype(BF16)
    kn_all = _dot(ckv, wuk_ref[...])
    vc_ref[...] = _dot(ckv, wuv_ref[...]).astype(BF16)
    kpe = proj[:, o2:o2 + LANES]
    kpe_sin = proj[:, o2 + LANES:o3] * k_sin
    nh = MLA_HEADS * LANES
    for h in range(MLA_HEADS):
        blk = slice(h * LANES, (h + 1) * LANES)
        qb = q_all[:, blk]
        qp = q_all[:, nh + h * LANES:nh + (h + 1) * LANES]
        qr = _inv_rms_padded_block(qb, MLA_QK) * (LOG2E * MLA_QK ** -0.5)
        qc_ref[:, blk] = ((qb * q_cos + qp * q_sin) * qr).astype(BF16)
        kb = kn_all[:, blk] + kpe
        kc_ref[:, blk] = ((kb * k_cos + kpe_sin) * _inv_rms_padded_block(kb, MLA_QK)).astype(BF16)
    acos, asin = acos_ref[...], asin_ref[...]
    q_cos, q_sin = gqg_ref[0:1] * acos, gqg_ref[1:2] * asin
    for j in range(nq // LANES):
        blk = slice(j * LANES, (j + 1) * LANES)
        qb = proj[:, o3 + j * LANES:o3 + (j + 1) * LANES]
        qp = proj[:, o3 + nq + j * LANES:o3 + nq + (j + 1) * LANES]
        qr = _inv_rms_half_blocks(qb) * (LOG2E * GQA_DIM ** -0.5)
        qd_ref[:, blk] = ((qb * q_cos + qp * q_sin) * qr).astype(BF16)
    kb = proj[:, o4:o4 + LANES]
    kp = proj[:, o4 + LANES:o5]
    kd_ref[...] = ((kb * (gkg_ref[0:1] * acos) + kp * (gkg_ref[1:2] * asin))
                   * _inv_rms_half_blocks(kb)).astype(BF16)
    vd_ref[...] = proj[:, o5:].astype(BF16)


def _pad_heads(w, heads, width):
    r = w.shape[0]
    w = w.reshape(r, heads, width)
    return jnp.pad(w, ((0, 0), (0, 0), (0, LANES - width))).reshape(r, heads * LANES)


def _rope_partner_cols(w, heads):
    r = w.shape[0]
    w = w.reshape(r, heads, MLA_QK)
    rope = _swap_pairs(w[:, :, MLA_NOPE:], MLA_ROPE)
    return jnp.concatenate([jnp.zeros_like(w[:, :, :MLA_NOPE]), rope], axis=2).reshape(r, heads * MLA_QK)


def _odd_proj_call(x, seq, gain, w_in, cq_norm, ckv_norm, w_uq, w_ukv, mq_norm, mk_norm,
                   gq_norm, gk_norm, mla_tables, axial_tables):
    t = x.shape[0]
    tm = PROJ_TM
    row = lambda i: (i, 0)
    pos = lambda i: (i % (seq // tm), 0)
    o1 = MLA_Q_RANK
    o2 = o1 + MLA_KV_RANK
    o3 = o2 + MLA_ROPE
    o4 = o3 + GQA_Q_HEADS * GQA_DIM
    o5 = o4 + GQA_KV_HEADS * GQA_DIM
    place_rope = lambda c: jnp.pad(c, ((0, 0), (MLA_NOPE, LANES - MLA_QK)))
    kpe_cols = w_in[:, o2:o3]
    gq_cols = w_in[:, o3:o4].reshape(D_MODEL, GQA_KV_HEADS, GQA_GROUP, GQA_DIM)
    gq_cols = gq_cols.transpose(0, 2, 1, 3).reshape(D_MODEL, GQA_Q_HEADS * GQA_DIM)
    gk_cols = w_in[:, o4:o5]
    half = GQA_DIM // 2
    win = jnp.concatenate(
        [w_in[:, :o2], place_rope(kpe_cols), place_rope(_swap_pairs(kpe_cols, MLA_ROPE)),
         gq_cols, _swap_pairs(gq_cols, half), gk_cols, _swap_pairs(gk_cols, half), w_in[:, o5:]],
        axis=1).astype(BF16)
    wuq = jnp.concatenate([_pad_heads(w_uq, MLA_HEADS, MLA_QK),
                           _pad_heads(_rope_partner_cols(w_uq, MLA_HEADS), MLA_HEADS, MLA_QK)],
                          axis=1).astype(BF16)
    w_ukv = w_ukv.reshape(MLA_KV_RANK, MLA_HEADS, MLA_NOPE + MLA_V)
    wuk = _pad_heads(w_ukv[:, :, :MLA_NOPE].reshape(MLA_KV_RANK, -1), MLA_HEADS, MLA_NOPE).astype(BF16)
    wuv = w_ukv[:, :, MLA_NOPE:].reshape(MLA_KV_RANK, MLA_HEADS * MLA_V).astype(BF16)

    def mla_gain_pair(g):
        partner = jnp.concatenate([g[:MLA_NOPE], _swap_pairs(g[MLA_NOPE:], MLA_ROPE)])
        return jnp.pad(jnp.stack([g, partner]), ((0, 0), (0, LANES - MLA_QK)))

    args = [x, gain.reshape(1, D_MODEL), win, cq_norm.reshape(1, -1), ckv_norm.reshape(1, -1),
            wuq, wuk, wuv, mla_gain_pair(mq_norm), mla_gain_pair(mk_norm),
            _gain_pair(gq_norm, half), _gain_pair(gk_norm, half), *mla_tables, *axial_tables]
    in_specs = ([pl.BlockSpec((tm, D_MODEL), row)]
                + [_const_spec(a.shape) for a in args[1:12]]
                + [pl.BlockSpec((tm, LANES), pos)] * 4)
    widths = [MLA_HEADS * LANES, MLA_HEADS * LANES, MLA_HEADS * MLA_V,
              GQA_Q_HEADS * GQA_DIM, LANES, LANES]
    return pl.pallas_call(
        _odd_proj_kernel, grid=(t // tm,), in_specs=in_specs,
        out_specs=[pl.BlockSpec((tm, n), row) for n in widths],
        out_shape=[jax.ShapeDtypeStruct((t, n), BF16) for n in widths],
        compiler_params=_params(("parallel",)), name="odd_proj",
    )(*args)


def _attention_kernel(*refs, mode, lam_init, stabilised):
    if mode == "diff":
        q_ref, k_ref, v_ref, lam_ref, sub_ref, o_ref = refs[:6]
    else:
        q_ref, k_ref, v_ref, o_ref = refs[:4]
    qt_ref, tile_buf, stat_ref, acc_ref = refs[-4:]
    tq = q_ref.shape[0]
    seq = k_ref.shape[0]
    tk = ATT_TK
    cw = ATT_CW
    n = seq // tk
    q = q_ref[...].astype(F32)
    if mode == "mla":
        qt_ref[:, :tq] = q[:, :LANES].T.astype(BF16)
        qt_ref[:, tq:] = q[:, LANES:].T.astype(BF16)
    else:
        qt = q.T
        first = lax.broadcasted_iota(jnp.int32, qt.shape, 0) < (LANES // 2)
        qt_ref[:, :tq] = jnp.where(first, qt, 0.0).astype(BF16)
        qt_ref[:, tq:] = jnp.where(first, 0.0, qt).astype(BF16)

    chunks = [slice(c * cw, (c + 1) * cw) for c in range(2 * tq // cw)]

    def tile_rows(t):
        return pl.ds(pl.multiple_of(t * tk, tk), tk)

    def ahead(kt, slot, cols):
        kc = kt
        if mode == "mla":
            kc = kt[:, :LANES] if cols.start < tq else kt[:, LANES:]
        s = _dot(kc, qt_ref[:, cols])
        if stabilised:
            tile_buf[slot, :, cols] = s
        else:
            p = jnp.exp2(s)
            stat_ref[:, cols] += jnp.sum(p.reshape(tk // 8, 8, p.shape[1]), axis=0)
            tile_buf[slot, :, cols] = p.astype(BF16)

    def consume(vt, slot, cols):
        if stabilised:
            s = tile_buf[slot, :, cols]
            m_old = stat_ref[:, cols]
            m_new = jnp.maximum(m_old, jnp.max(s, axis=0, keepdims=True))
            alpha = jnp.exp2(m_old - m_new)
            p = jnp.exp2(s - m_new).astype(BF16)
            stat_ref[:, cols] = m_new
            acc_ref[:, cols] = alpha * acc_ref[:, cols] + _dot(vt, p)
        else:
            acc_ref[:, cols] += _dot(vt, tile_buf[slot, :, cols])

    def value_rows(t):
        vt = v_ref[tile_rows(t), :].T
        parts = (vt, vt) if mode == "diff" else (vt[:LANES // 2], vt[LANES // 2:])
        if not stabilised:
            return parts
        ones = jnp.ones((SUM_ROWS, tk), BF16)
        return tuple(jnp.concatenate([part, ones], axis=0) for part in parts)

    def half_step(t, t_next, cur, nxt):
        kt = k_ref[tile_rows(t_next), :]
        vt = value_rows(t)
        for cols in chunks:
            ahead(kt, nxt, cols)
            consume(vt[0] if cols.start < tq else vt[1], cur, cols)

    stat_ref[...] = jnp.full(stat_ref.shape, -jnp.inf if stabilised else 0.0, F32)
    acc_ref[...] = jnp.zeros(acc_ref.shape, F32)
    kt0 = k_ref[tile_rows(0), :]
    for cols in chunks:
        ahead(kt0, 0, cols)

    def body(i, carry):
        for u in range(ATT_PAIRS):
            t = 2 * (i * ATT_PAIRS + u)
            half_step(t, t + 1, 0, 1)
            half_step(t + 1, t + 2, 1, 0)
        return carry

    lax.fori_loop(0, (n - 2) // (2 * ATT_PAIRS), body, 0)
    half_step(n - 2, n - 1, 0, 1)
    vt_last = value_rows(n - 1)
    for cols in chunks:
        consume(vt_last[0] if cols.start < tq else vt_last[1], 1, cols)

    if stabilised:
        nv = acc_ref.shape[0] - SUM_ROWS
        o = acc_ref[:nv, :] / acc_ref[nv:nv + 1, :]
    else:
        o = acc_ref[...] / jnp.sum(stat_ref[...], axis=0, keepdims=True)
    if mode == "diff":
        lv = lam_ref[...]
        lam = (jnp.exp(jnp.sum(lv[0:1] * lv[1:2], axis=-1, keepdims=True))
               - jnp.exp(jnp.sum(lv[2:3] * lv[3:4], axis=-1, keepdims=True)) + lam_init)
        ot = o[:, :tq] - lam * o[:, tq:]
        ms = jnp.mean(ot * ot, axis=0, keepdims=True)
        ot = ot * lax.rsqrt(ms + EPS) * (sub_ref[...] * (1.0 - lam_init))
        o_ref[...] = ot.T.astype(o_ref.dtype)
    else:
        ot = jnp.concatenate([o[:, :tq], o[:, tq:]], axis=0).T
        o_ref[...] = ot.astype(o_ref.dtype)


def _attention_call(q, k, v, mode, score_bound, extra=(), lam_init=0.0):
    b, seq, _ = q.shape
    qw = 2 * LANES if mode == "mla" else LANES
    nblk = q.shape[2] // qw
    tq, tk = ATT_TQ, ATT_TK
    assert seq % tk == 0 and (seq // tk - 2) % (2 * ATT_PAIRS) == 0 and seq % tq == 0 and tq % ATT_CW == 0
    shared = mode == "gqa"
    value_rows = LANES if mode == "diff" else LANES // 2
    kv_idx = (lambda bi, j, i: (bi, 0, 0)) if shared else (lambda bi, j, i: (bi, 0, j))
    in_specs = [pl.BlockSpec((None, tq, qw), lambda bi, j, i: (bi, i, j)),
                pl.BlockSpec((None, seq, qw), kv_idx),
                pl.BlockSpec((None, seq, LANES), kv_idx)]
    in_specs += [_const_spec(e.shape) for e in extra]

    def call(stabilised):
        if stabilised:
            scratch = [pltpu.VMEM((2, tk, 2 * tq), F32), pltpu.VMEM((1, 2 * tq), F32),
                       pltpu.VMEM((value_rows + SUM_ROWS, 2 * tq), F32)]
        else:
            scratch = [pltpu.VMEM((2, tk, 2 * tq), BF16), pltpu.VMEM((8, 2 * tq), F32),
                       pltpu.VMEM((value_rows, 2 * tq), F32)]
        return pl.pallas_call(
            functools.partial(_attention_kernel, mode=mode, lam_init=lam_init, stabilised=stabilised),
            grid=(b, nblk, seq // tq), in_specs=in_specs,
            out_specs=pl.BlockSpec((None, tq, LANES), lambda bi, j, i: (bi, i, j)),
            out_shape=jax.ShapeDtypeStruct((b, seq, nblk * LANES), BF16),
            scratch_shapes=[pltpu.VMEM((LANES, 2 * tq), BF16)] + scratch,
            compiler_params=_params(("parallel", "parallel", "arbitrary")),
            name="attn_" + mode + ("_stab" if stabilised else ""),
        )

    return lax.cond(score_bound <= SAFE_SCORE_RANGE, call(False), call(True), q, k, v, *extra)


def _score_bound(q_gain, k_gain, dim):
    return (jnp.max(jnp.abs(q_gain)) * jnp.max(jnp.abs(k_gain))
            * (dim * dim ** -0.5 * LOG2E * ROUNDING_SLACK))


def _rope_tables(seq):
    assert MLA_ROPE == GQA_DIM // 2 and seq % GRID_W == 0
    rows = seq // GRID_W
    zeros = lambda n: jnp.zeros((seq, n), F32)

    def angles(pos, dim):
        inv = ROPE_THETA ** (-jnp.arange(0, dim, 2, dtype=F32) / dim)
        ang = pos.astype(F32)[:, None] * inv[None, :]
        return lax.optimization_barrier((jnp.cos(ang), jnp.sin(ang)))

    def all_positions(dim):
        ch, sh = angles(GRID_W * jnp.arange(rows, dtype=jnp.int32), dim)
        cl, sl = angles(jnp.arange(GRID_W, dtype=jnp.int32), dim)
        c = ch[:, None, :] * cl[None, :, :] - sh[:, None, :] * sl[None, :, :]
        s = sh[:, None, :] * cl[None, :, :] + ch[:, None, :] * sl[None, :, :]
        return c.reshape(seq, -1), s.reshape(seq, -1), cl, sl

    c, s, _, _ = all_positions(DIFF_D)
    full = (jnp.tile(jnp.concatenate([c, c], 1), (1, 2)), jnp.tile(jnp.concatenate([-s, s], 1), (1, 2)))
    c, s, c_col, s_col = all_positions(MLA_ROPE)
    tail = zeros(LANES - MLA_QK)
    mla = (jnp.concatenate([jnp.ones((seq, MLA_NOPE), F32), c, c, tail], 1),
           jnp.concatenate([zeros(MLA_NOPE), -s, s, tail], 1))
    c_row, s_row = angles(jnp.arange(rows, dtype=jnp.int32), GQA_DIM // 2)
    cr, sr = jnp.repeat(c_row, GRID_W, axis=0), jnp.repeat(s_row, GRID_W, axis=0)
    cc, sc = jnp.tile(c_col, (rows, 1)), jnp.tile(s_col, (rows, 1))
    axial = (jnp.tile(jnp.concatenate([cr, cr, cc, cc], 1), (1, 2)),
             jnp.tile(jnp.concatenate([-sr, sr, -sc, sc], 1), (1, 2)))
    return full, mla, axial


def kernel(x, ffn1_norm, ffn1_w_gu, ffn1_w_down, ffn2_norm, ffn2_w_gu, ffn2_w_down, ev_norm, ev_w_in, ev_sgu_norm, ev_w_s, ev_b_s, ev_q_norm, ev_k_norm, ev_lam_q1, ev_lam_k1, ev_lam_q2, ev_lam_k2, ev_sub_norm, ev_w_out, od_norm, od_w_in, od_cq_norm, od_ckv_norm, od_w_uq, od_w_ukv, od_mla_q_norm, od_mla_k_norm, od_gqa_q_norm, od_gqa_k_norm, od_w_out):
    b, seq, d = x.shape
    t = b * seq
    full_tab, mla_tab, axial_tab = _rope_tables(seq)
    x2 = x.reshape(t, d)
    ffn1 = (ffn1_norm[:, None, :], ffn1_w_gu, ffn1_w_down)
    ffn2 = (ffn2_norm[:, None, :], ffn2_w_gu, ffn2_w_down)

    x2 = _ffn_call(x2, 0, *ffn1)
    out_a, q, k, v = _even_proj_call(x2, seq, ev_norm[0], ev_w_in[0], ev_sgu_norm[0], ev_w_s[0],
                                     ev_b_s[0], ev_q_norm[0], ev_k_norm[0], full_tab)
    lam_init = 0.8 - 0.6 * float(np.exp(-0.3 * 0))
    lam_vecs = jnp.stack([ev_lam_q1[0], ev_lam_k1[0], ev_lam_q2[0], ev_lam_k2[0]])
    shp = (b, seq, GMLP_WIDTH)
    out_b = _attention_call(q.reshape(shp), k.reshape(shp), v.reshape(shp), "diff",
                            _score_bound(ev_q_norm[0], ev_k_norm[0], DIFF_D), extra=(lam_vecs, ev_sub_norm[0].reshape(DIFF_V, 1)), lam_init=lam_init)
    w_out = ev_w_out[0].astype(BF16)
    x2 = _ffn_call(x2, 0, *ffn2,
                   mix=(out_a, out_b.reshape(t, -1), w_out[:GMLP_WIDTH], w_out[GMLP_WIDTH:]))

    x2 = _ffn_call(x2, 1, *ffn1)
    qc, kc, vc, qd, kd, vd = _odd_proj_call(
        x2, seq, od_norm[0], od_w_in[0], od_cq_norm[0], od_ckv_norm[0], od_w_uq[0], od_w_ukv[0],
        od_mla_q_norm[0], od_mla_k_norm[0], od_gqa_q_norm[0], od_gqa_k_norm[0], mla_tab, axial_tab)
    r3 = lambda a: a.reshape(b, seq, a.shape[1])
    out_c = _attention_call(r3(qc), r3(kc), r3(vc), "mla",
                            _score_bound(od_mla_q_norm[0], od_mla_k_norm[0], MLA_QK))
    out_d = _attention_call(r3(qd), r3(kd), r3(vd), "gqa",
                            _score_bound(od_gqa_q_norm[0], od_gqa_k_norm[0], GQA_DIM))
    w_out = od_w_out[0].astype(BF16)
    n_c = MLA_HEADS * MLA_V
    w_d = w_out[n_c:].reshape(GQA_KV_HEADS, GQA_GROUP, GQA_DIM, d).transpose(1, 0, 2, 3)
    x2 = _ffn_call(x2, 1, *ffn2,
                   mix=(out_c.reshape(t, -1), out_d.reshape(t, -1), w_out[:n_c],
                        w_d.reshape(GQA_Q_HEADS * GQA_DIM, d)))
    return x2.reshape(b, seq, d)
```

```python
import functools
import math

import numpy as np
import jax
import jax.numpy as jnp
from jax import lax
from jax.experimental import pallas as pl
from jax.experimental.pallas import tpu as pltpu

D_MODEL = 1024
D_FF = 2816
ROPE_THETA = 10000.0
GRID_W = 64
EPS = 1e-6
GMLP_GROUPS = 8
GMLP_GROUP_DIM = 64
GMLP_CHUNK = 128
GMLP_WIDTH = GMLP_GROUPS * GMLP_GROUP_DIM
DIFF_D = 64
DIFF_V = 128
MLA_HEADS = 8
MLA_Q_RANK = 256
MLA_KV_RANK = 128
MLA_NOPE = 64
MLA_ROPE = 32
MLA_V = 64
MLA_QK = MLA_NOPE + MLA_ROPE
GQA_Q_HEADS = 8
GQA_KV_HEADS = 2
GQA_GROUP = GQA_Q_HEADS // GQA_KV_HEADS
GQA_DIM = 64

LANES = 128
VMEM_LIMIT_BYTES = 56 * 1024 * 1024

FFN_TM = 1024
FFN_SUB = 256
FFN_CAST_CHUNKS = 11
PROJ_TM = 512
ATT_TQ = 2048
ATT_TK = 256
ATT_CW = 512
ATT_PAIRS = 3
SUM_ROWS = 16
LOG2E = math.log2(math.e)
SAFE_SCORE_RANGE = 60.0
ROUNDING_SLACK = 1.0 + 2.0 ** -6

BF16 = jnp.bfloat16
F32 = jnp.float32


def _params(semantics):
    return pltpu.CompilerParams(dimension_semantics=semantics,
                                vmem_limit_bytes=VMEM_LIMIT_BYTES)


def _const_spec(shape):
    nd = len(shape)
    return pl.BlockSpec(shape, lambda *_: (0,) * nd, pipeline_mode=pl.Buffered(1))


def _rms_rows(x, gain):
    ms = jnp.mean(x * x, axis=-1, keepdims=True)
    return x * lax.rsqrt(ms + EPS) * gain


def _dot(a, b):
    return jnp.dot(a, b, preferred_element_type=F32)


def _lane_is_low(shape):
    return lax.broadcasted_iota(jnp.int32, shape, len(shape) - 1) < (LANES // 2)


def _inv_rms_half_blocks(x):
    low = _lane_is_low(x.shape)
    x2 = x * x
    x2_lo = jnp.where(low, x2, 0.0)
    x2_hi = x2 - x2_lo
    ms_lo = jnp.sum(x2_lo, axis=-1, keepdims=True) * (2.0 / LANES)
    ms_hi = jnp.sum(x2_hi, axis=-1, keepdims=True) * (2.0 / LANES)
    return jnp.where(low, lax.rsqrt(ms_lo + EPS), lax.rsqrt(ms_hi + EPS))


def _inv_rms_padded_block(x, width):
    ms = jnp.sum(x * x, axis=-1, keepdims=True) * (1.0 / width)
    return lax.rsqrt(ms + EPS)


def _swap_pairs(a, group):
    shape = a.shape
    a = a.reshape(shape[:-1] + (shape[-1] // group, 2, group // 2))
    return jnp.flip(a, axis=-2).reshape(shape)


def _ffn_body(x, g_ref, wgu_ref, wd_ref, o_ref):
    for r in range(x.shape[0] // FFN_SUB):
        rows = slice(r * FFN_SUB, (r + 1) * FFN_SUB)
        xr = x[rows]
        xn = _rms_rows(xr, g_ref[...]).astype(BF16)
        h = _dot(xn, wgu_ref[...])
        gate = h[:, :D_FF]
        up = h[:, D_FF:]
        act = (gate / (1.0 + jnp.exp(-gate)) * up).astype(BF16)
        o_ref[rows, :] = xr + 0.5 * _dot(act, wd_ref[...])


def _load_ffn_weights(layer, wgu_hbm, wd_hbm, wgu_ref, wd_ref, gu_stage, d_stage, sem):
    gu_cols = 2 * D_FF // FFN_CAST_CHUNKS
    d_rows = D_FF // FFN_CAST_CHUNKS

    def gu_copy(c, slot):
        return pltpu.make_async_copy(wgu_hbm.at[layer, :, pl.ds(c * gu_cols, gu_cols)],
                                     gu_stage.at[slot], sem.at[0, slot])

    def d_copy(c, slot):
        return pltpu.make_async_copy(wd_hbm.at[layer, pl.ds(c * d_rows, d_rows), :],
                                     d_stage.at[slot], sem.at[1, slot])

    gu_copy(0, 0).start()
    d_copy(0, 0).start()
    for c in range(FFN_CAST_CHUNKS):
        slot = c % 2
        if c + 1 < FFN_CAST_CHUNKS:
            gu_copy(c + 1, 1 - slot).start()
            d_copy(c + 1, 1 - slot).start()
        gu_copy(c, slot).wait()
        wgu_ref[:, c * gu_cols:(c + 1) * gu_cols] = gu_stage[slot].astype(BF16)
        d_copy(c, slot).wait()
        wd_ref[c * d_rows:(c + 1) * d_rows, :] = d_stage[slot].astype(BF16)


def _ffn_kernel(x_ref, g_ref, wgu_hbm, wd_hbm, o_ref, wgu_ref, wd_ref, gu_stage, d_stage, sem, *, layer):
    @pl.when(pl.program_id(0) == 0)
    def _():
        _load_ffn_weights(layer, wgu_hbm, wd_hbm, wgu_ref, wd_ref, gu_stage, d_stage, sem)

    _ffn_body(x_ref[...], g_ref, wgu_ref, wd_ref, o_ref)


def _mix_ffn_kernel(x_ref, a_ref, b_ref, wa_ref, wb_ref, g_ref, wgu_hbm, wd_hbm, o_ref,
                    wgu_ref, wd_ref, gu_stage, d_stage, sem, *, layer):
    @pl.when(pl.program_id(0) == 0)
    def _():
        _load_ffn_weights(layer, wgu_hbm, wd_hbm, wgu_ref, wd_ref, gu_stage, d_stage, sem)

    x = x_ref[...] + _dot(a_ref[...], wa_ref[...]) + _dot(b_ref[...], wb_ref[...])
    _ffn_body(x, g_ref, wgu_ref, wd_ref, o_ref)


def _layer_spec(shape, layer):
    nd = len(shape) - 1
    return pl.BlockSpec((None,) + tuple(shape[1:]), lambda *_: (layer,) + (0,) * nd,
                        pipeline_mode=pl.Buffered(1))


def _ffn_call(x, layer, gains, w_gu, w_down, mix=None):
    t = x.shape[0]
    tm = FFN_TM
    row = lambda i: (i, 0)
    x_spec = pl.BlockSpec((tm, D_MODEL), row)
    hbm = pl.BlockSpec(memory_space=pl.ANY)
    w_args = [gains, w_gu, w_down]
    w_specs = [_layer_spec(gains.shape, layer), hbm, hbm]
    if mix is None:
        kern, in_specs, args = _ffn_kernel, [x_spec] + w_specs, [x] + w_args
    else:
        a, b, wa, wb = mix
        kern = _mix_ffn_kernel
        in_specs = [x_spec, pl.BlockSpec((tm, a.shape[1]), row), pl.BlockSpec((tm, b.shape[1]), row),
                    _const_spec(wa.shape), _const_spec(wb.shape)] + w_specs
        args = [x, a, b, wa, wb] + w_args
    return pl.pallas_call(
        functools.partial(kern, layer=layer), grid=(t // tm,), in_specs=in_specs,
        out_specs=pl.BlockSpec((tm, D_MODEL), row),
        out_shape=jax.ShapeDtypeStruct((t, D_MODEL), F32),
        scratch_shapes=[pltpu.VMEM((D_MODEL, 2 * D_FF), BF16), pltpu.VMEM((D_FF, D_MODEL), BF16),
                        pltpu.VMEM((2, D_MODEL, 2 * D_FF // FFN_CAST_CHUNKS), F32),
                        pltpu.VMEM((2, D_FF // FFN_CAST_CHUNKS, D_MODEL), F32),
                        pltpu.SemaphoreType.DMA((2, 2))],
        compiler_params=_params(("arbitrary",)),
        name="ffn" if mix is None else "mix_ffn",
    )(*args)


def _gelu_tanh(x):
    c = math.sqrt(2.0 / math.pi)
    return 0.5 * x * (1.0 + jnp.tanh(c * (x + 0.044715 * (x * x * x))))


def _even_proj_kernel(x_ref, g_ref, win_ref, sgu_ref, wpair_ref, bias_ref, qg_ref, kg_ref,
                      cos_ref, sin_ref, oa_ref, q_ref, k_ref, v_ref):
    xn = _rms_rows(x_ref[...], g_ref[...]).astype(BF16)
    proj = _dot(xn, win_ref[...])
    w = GMLP_WIDTH
    nblk = w // LANES
    tm = proj.shape[0]
    for j in range(nblk):
        u = _gelu_tanh(proj[:, j * LANES:(j + 1) * LANES])
        v = _gelu_tanh(proj[:, w + j * LANES:w + (j + 1) * LANES])
        vn = v * _inv_rms_half_blocks(v) * sgu_ref[:, j * LANES:(j + 1) * LANES]
        low = _lane_is_low(vn.shape)
        vn_lo = jnp.where(low, vn, 0.0).astype(BF16)
        vn_hi = jnp.where(low, 0.0, vn).astype(BF16)
        wp = wpair_ref[j]
        bias = bias_ref[:, j * LANES:(j + 1) * LANES]
        for c in range(tm // GMLP_CHUNK):
            rows = slice(c * GMLP_CHUNK, (c + 1) * GMLP_CHUNK)
            stacked = jnp.concatenate([vn_lo[rows], vn_hi[rows]], axis=0)
            mixed = _dot(wp, stacked) + bias
            oa_ref[rows, j * LANES:(j + 1) * LANES] = (u[rows] * mixed).astype(BF16)
    cos, sin = cos_ref[...], sin_ref[...]
    q_cos, q_sin = qg_ref[0:1] * cos, qg_ref[1:2] * sin
    k_cos, k_sin = kg_ref[0:1] * cos, kg_ref[1:2] * sin
    for j in range(nblk):
        blk = slice(j * LANES, (j + 1) * LANES)
        qb = proj[:, 2 * w + j * LANES:2 * w + (j + 1) * LANES]
        kb = proj[:, 3 * w + j * LANES:3 * w + (j + 1) * LANES]
        qp = proj[:, 5 * w + j * LANES:5 * w + (j + 1) * LANES]
        kp = proj[:, 6 * w + j * LANES:6 * w + (j + 1) * LANES]
        qr = _inv_rms_half_blocks(qb) * (LOG2E * DIFF_D ** -0.5)
        q_ref[:, blk] = ((qb * q_cos + qp * q_sin) * qr).astype(BF16)
        k_ref[:, blk] = ((kb * k_cos + kp * k_sin) * _inv_rms_half_blocks(kb)).astype(BF16)
    v_ref[...] = proj[:, 4 * w:5 * w].astype(BF16)


def _gain_pair(g, group):
    return jnp.stack([jnp.tile(g, 2), jnp.tile(_swap_pairs(g, group), 2)])


def _even_proj_call(x, seq, gain, w_in, sgu_norm, w_s, b_s, q_norm, k_norm, tables):
    t = x.shape[0]
    tm = PROJ_TM
    w = GMLP_WIDTH
    row = lambda i: (i, 0)
    pos = lambda i: (i % (seq // tm), 0)
    wpair = jnp.concatenate([w_s[0::2], w_s[1::2]], axis=2).astype(BF16)
    bias = jnp.repeat(b_s.T, GMLP_GROUP_DIM, axis=1)
    qk_cols = w_in[:, 2 * w:4 * w]
    win = jnp.concatenate([w_in, _swap_pairs(qk_cols, DIFF_D)], axis=1).astype(BF16)
    args = [x, gain.reshape(1, D_MODEL), win, sgu_norm.reshape(1, w), wpair, bias,
            _gain_pair(q_norm, DIFF_D), _gain_pair(k_norm, DIFF_D), *tables]
    in_specs = [pl.BlockSpec((tm, D_MODEL), row)] + [_const_spec(a.shape) for a in args[1:8]]
    in_specs += [pl.BlockSpec((tm, LANES), pos)] * 2
    out = jax.ShapeDtypeStruct((t, w), BF16)
    return pl.pallas_call(
        _even_proj_kernel, grid=(t // tm,), in_specs=in_specs,
        out_specs=[pl.BlockSpec((tm, w), row)] * 4, out_shape=[out] * 4,
        compiler_params=_params(("parallel",)), name="even_proj",
    )(*args)


def _odd_proj_kernel(x_ref, g_ref, win_ref, cqg_ref, ckvg_ref, wuq_ref, wuk_ref, wuv_ref,
                     mqg_ref, mkg_ref, gqg_ref, gkg_ref, mcos_ref, msin_ref, acos_ref, asin_ref,
                     qc_ref, kc_ref, vc_ref, qd_ref, kd_ref, vd_ref):
    xn = _rms_rows(x_ref[...], g_ref[...]).astype(BF16)
    proj = _dot(xn, win_ref[...])
    nq = GQA_Q_HEADS * GQA_DIM
    o1 = MLA_Q_RANK
    o2 = o1 + MLA_KV_RANK
    o3 = o2 + 2 * LANES
    o4 = o3 + 2 * nq
    o5 = o4 + 2 * LANES
    mcos, msin = mcos_ref[...], msin_ref[...]
    q_cos, q_sin = mqg_ref[0:1] * mcos, mqg_ref[1:2] * msin
    k_cos, k_sin = mkg_ref[0:1] * mcos, mkg_ref[1:2] * msin
    cq = _rms_rows(proj[:, :o1], cqg_ref[...]).astype(BF16)
    q_all = _dot(cq, wuq_ref[...])
    ckv = _rms_rows(proj[:, o1:o2], ckvg_ref[...]).astype(BF16)
    kn_all = _dot(ckv, wuk_ref[...])
    vc_ref[...] = _dot(ckv, wuv_ref[...]).astype(BF16)
    kpe = proj[:, o2:o2 + LANES]
    kpe_sin = proj[:, o2 + LANES:o3] * k_sin
    nh = MLA_HEADS * LANES
    for h in range(MLA_HEADS):
        blk = slice(h * LANES, (h + 1) * LANES)
        qb = q_all[:, blk]
        qp = q_all[:, nh + h * LANES:nh + (h + 1) * LANES]
        qr = _inv_rms_padded_block(qb, MLA_QK) * (LOG2E * MLA_QK ** -0.5)
        qc_ref[:, blk] = ((qb * q_cos + qp * q_sin) * qr).astype(BF16)
        kb = kn_all[:, blk] + kpe
        kc_ref[:, blk] = ((kb * k_cos + kpe_sin) * _inv_rms_padded_block(kb, MLA_QK)).astype(BF16)
    acos, asin = acos_ref[...], asin_ref[...]
    q_cos, q_sin = gqg_ref[0:1] * acos, gqg_ref[1:2] * asin
    for j in range(nq // LANES):
        blk = slice(j * LANES, (j + 1) * LANES)
        qb = proj[:, o3 + j * LANES:o3 + (j + 1) * LANES]
        qp = proj[:, o3 + nq + j * LANES:o3 + nq + (j + 1) * LANES]
        qr = _inv_rms_half_blocks(qb) * (LOG2E * GQA_DIM ** -0.5)
        qd_ref[:, blk] = ((qb * q_cos + qp * q_sin) * qr).astype(BF16)
    kb = proj[:, o4:o4 + LANES]
    kp = proj[:, o4 + LANES:o5]
    kd_ref[...] = ((kb * (gkg_ref[0:1] * acos) + kp * (gkg_ref[1:2] * asin))
                   * _inv_rms_half_blocks(kb)).astype(BF16)
    vd_ref[...] = proj[:, o5:].astype(BF16)


def _pad_heads(w, heads, width):
    r = w.shape[0]
    w = w.reshape(r, heads, width)
    return jnp.pad(w, ((0, 0), (0, 0), (0, LANES - width))).reshape(r, heads * LANES)


def _rope_partner_cols(w, heads):
    r = w.shape[0]
    w = w.reshape(r, heads, MLA_QK)
    rope = _swap_pairs(w[:, :, MLA_NOPE:], MLA_ROPE)
    return jnp.concatenate([jnp.zeros_like(w[:, :, :MLA_NOPE]), rope], axis=2).reshape(r, heads * MLA_QK)


def _odd_proj_call(x, seq, gain, w_in, cq_norm, ckv_norm, w_uq, w_ukv, mq_norm, mk_norm,
                   gq_norm, gk_norm, mla_tables, axial_tables):
    t = x.shape[0]
    tm = PROJ_TM
    row = lambda i: (i, 0)
    pos = lambda i: (i % (seq // tm), 0)
    o1 = MLA_Q_RANK
    o2 = o1 + MLA_KV_RANK
    o3 = o2 + MLA_ROPE
    o4 = o3 + GQA_Q_HEADS * GQA_DIM
    o5 = o4 + GQA_KV_HEADS * GQA_DIM
    place_rope = lambda c: jnp.pad(c, ((0, 0), (MLA_NOPE, LANES - MLA_QK)))
    kpe_cols = w_in[:, o2:o3]
    gq_cols = w_in[:, o3:o4].reshape(D_MODEL, GQA_KV_HEADS, GQA_GROUP, GQA_DIM)
    gq_cols = gq_cols.transpose(0, 2, 1, 3).reshape(D_MODEL, GQA_Q_HEADS * GQA_DIM)
    gk_cols = w_in[:, o4:o5]
    half = GQA_DIM // 2
    win = jnp.concatenate(
        [w_in[:, :o2], place_rope(kpe_cols), place_rope(_swap_pairs(kpe_cols, MLA_ROPE)),
         gq_cols, _swap_pairs(gq_cols, half), gk_cols, _swap_pairs(gk_cols, half), w_in[:, o5:]],
        axis=1).astype(BF16)
    wuq = jnp.concatenate([_pad_heads(w_uq, MLA_HEADS, MLA_QK),
                           _pad_heads(_rope_partner_cols(w_uq, MLA_HEADS), MLA_HEADS, MLA_QK)],
                          axis=1).astype(BF16)
    w_ukv = w_ukv.reshape(MLA_KV_RANK, MLA_HEADS, MLA_NOPE + MLA_V)
    wuk = _pad_heads(w_ukv[:, :, :MLA_NOPE].reshape(MLA_KV_RANK, -1), MLA_HEADS, MLA_NOPE).astype(BF16)
    wuv = w_ukv[:, :, MLA_NOPE:].reshape(MLA_KV_RANK, MLA_HEADS * MLA_V).astype(BF16)

    def mla_gain_pair(g):
        partner = jnp.concatenate([g[:MLA_NOPE], _swap_pairs(g[MLA_NOPE:], MLA_ROPE)])
        return jnp.pad(jnp.stack([g, partner]), ((0, 0), (0, LANES - MLA_QK)))

    args = [x, gain.reshape(1, D_MODEL), win, cq_norm.reshape(1, -1), ckv_norm.reshape(1, -1),
            wuq, wuk, wuv, mla_gain_pair(mq_norm), mla_gain_pair(mk_norm),
            _gain_pair(gq_norm, half), _gain_pair(gk_norm, half), *mla_tables, *axial_tables]
    in_specs = ([pl.BlockSpec((tm, D_MODEL), row)]
                + [_const_spec(a.shape) for a in args[1:12]]
                + [pl.BlockSpec((tm, LANES), pos)] * 4)
    widths = [MLA_HEADS * LANES, MLA_HEADS * LANES, MLA_HEADS * MLA_V,
              GQA_Q_HEADS * GQA_DIM, LANES, LANES]
    return pl.pallas_call(
        _odd_proj_kernel, grid=(t // tm,), in_specs=in_specs,
        out_specs=[pl.BlockSpec((tm, n), row) for n in widths],
        out_shape=[jax.ShapeDtypeStruct((t, n), BF16) for n in widths],
        compiler_params=_params(("parallel",)), name="odd_proj",
    )(*args)


def _attention_kernel(*refs, mode, lam_init, stabilised):
    if mode == "diff":
        q_ref, k_ref, v_ref, lam_ref, sub_ref, o_ref = refs[:6]
    else:
        q_ref, k_ref, v_ref, o_ref = refs[:4]
    qt_ref, tile_buf, stat_ref, acc_ref = refs[-4:]
    tq = q_ref.shape[0]
    seq = k_ref.shape[0]
    tk = ATT_TK
    cw = ATT_CW
    n = seq // tk
    q = q_ref[...].astype(F32)
    if mode == "mla":
        qt_ref[:, :tq] = q[:, :LANES].T.astype(BF16)
        qt_ref[:, tq:] = q[:, LANES:].T.astype(BF16)
    else:
        qt = q.T
        first = lax.broadcasted_iota(jnp.int32, qt.shape, 0) < (LANES // 2)
        qt_ref[:, :tq] = jnp.where(first, qt, 0.0).astype(BF16)
        qt_ref[:, tq:] = jnp.where(first, 0.0, qt).astype(BF16)

    chunks = [slice(c * cw, (c + 1) * cw) for c in range(2 * tq // cw)]

    def tile_rows(t):
        return pl.ds(pl.multiple_of(t * tk, tk), tk)

    def ahead(kt, slot, cols):
        kc = kt
        if mode == "mla":
            kc = kt[:, :LANES] if cols.start < tq else kt[:, LANES:]
        s = _dot(kc, qt_ref[:, cols])
        if stabilised:
            tile_buf[slot, :, cols] = s
        else:
            p = jnp.exp2(s)
            stat_ref[:, cols] += jnp.sum(p.reshape(tk // 8, 8, p.shape[1]), axis=0)
            tile_buf[slot, :, cols] = p.astype(BF16)

    def consume(vt, slot, cols):
        if stabilised:
            s = tile_buf[slot, :, cols]
            m_old = stat_ref[:, cols]
            m_new = jnp.maximum(m_old, jnp.max(s, axis=0, keepdims=True))
            alpha = jnp.exp2(m_old - m_new)
            p = jnp.exp2(s - m_new).astype(BF16)
            stat_ref[:, cols] = m_new
            acc_ref[:, cols] = alpha * acc_ref[:, cols] + _dot(vt, p)
        else:
            acc_ref[:, cols] += _dot(vt, tile_buf[slot, :, cols])

    def value_rows(t):
        vt = v_ref[tile_rows(t), :].T
        parts = (vt, vt) if mode == "diff" else (vt[:LANES // 2], vt[LANES // 2:])
        if not stabilised:
            return parts
        ones = jnp.ones((SUM_ROWS, tk), BF16)
        return tuple(jnp.concatenate([part, ones], axis=0) for part in parts)

    def half_step(t, t_next, cur, nxt):
        kt = k_ref[tile_rows(t_next), :]
        vt = value_rows(t)
        for cols in chunks:
            ahead(kt, nxt, cols)
            consume(vt[0] if cols.start < tq else vt[1], cur, cols)

    stat_ref[...] = jnp.full(stat_ref.shape, -jnp.inf if stabilised else 0.0, F32)
    acc_ref[...] = jnp.zeros(acc_ref.shape, F32)
    kt0 = k_ref[tile_rows(0), :]
    for cols in chunks:
        ahead(kt0, 0, cols)

    def body(i, carry):
        for u in range(ATT_PAIRS):
            t = 2 * (i * ATT_PAIRS + u)
            half_step(t, t + 1, 0, 1)
            half_step(t + 1, t + 2, 1, 0)
        return carry

    lax.fori_loop(0, (n - 2) // (2 * ATT_PAIRS), body, 0)
    half_step(n - 2, n - 1, 0, 1)
    vt_last = value_rows(n - 1)
    for cols in chunks:
        consume(vt_last[0] if cols.start < tq else vt_last[1], 1, cols)

    if stabilised:
        nv = acc_ref.shape[0] - SUM_ROWS
        o = acc_ref[:nv, :] / acc_ref[nv:nv + 1, :]
    else:
        o = acc_ref[...] / jnp.sum(stat_ref[...], axis=0, keepdims=True)
    if mode == "diff":
        lv = lam_ref[...]
        lam = (jnp.exp(jnp.sum(lv[0:1] * lv[1:2], axis=-1, keepdims=True))
               - jnp.exp(jnp.sum(lv[2:3] * lv[3:4], axis=-1, keepdims=True)) + lam_init)
        ot = o[:, :tq] - lam * o[:, tq:]
        ms = jnp.mean(ot * ot, axis=0, keepdims=True)
        ot = ot * lax.rsqrt(ms + EPS) * (sub_ref[...] * (1.0 - lam_init))
        o_ref[...] = ot.T.astype(o_ref.dtype)
    else:
        ot = jnp.concatenate([o[:, :tq], o[:, tq:]], axis=0).T
        o_ref[...] = ot.astype(o_ref.dtype)


def _attention_call(q, k, v, mode, score_bound, extra=(), lam_init=0.0):
    b, seq, _ = q.shape
    qw = 2 * LANES if mode == "mla" else LANES
    nblk = q.shape[2] // qw
    tq, tk = ATT_TQ, ATT_TK
    assert seq % tk == 0 and (seq // tk - 2) % (2 * ATT_PAIRS) == 0 and seq % tq == 0 and tq % ATT_CW == 0
    shared = mode == "gqa"
    value_rows = LANES if mode == "diff" else LANES // 2
    kv_idx = (lambda bi, j, i: (bi, 0, 0)) if shared else (lambda bi, j, i: (bi, 0, j))
    in_specs = [pl.BlockSpec((None, tq, qw), lambda bi, j, i: (bi, i, j)),
                pl.BlockSpec((None, seq, qw), kv_idx),
                pl.BlockSpec((None, seq, LANES), kv_idx)]
    in_specs += [_const_spec(e.shape) for e in extra]

    def call(stabilised):
        if stabilised:
            scratch = [pltpu.VMEM((2, tk, 2 * tq), F32), pltpu.VMEM((1, 2 * tq), F32),
                       pltpu.VMEM((value_rows + SUM_ROWS, 2 * tq), F32)]
        else:
            scratch = [pltpu.VMEM((2, tk, 2 * tq), BF16), pltpu.VMEM((8, 2 * tq), F32),
                       pltpu.VMEM((value_rows, 2 * tq), F32)]
        return pl.pallas_call(
            functools.partial(_attention_kernel, mode=mode, lam_init=lam_init, stabilised=stabilised),
            grid=(b, nblk, seq // tq), in_specs=in_specs,
            out_specs=pl.BlockSpec((None, tq, LANES), lambda bi, j, i: (bi, i, j)),
            out_shape=jax.ShapeDtypeStruct((b, seq, nblk * LANES), BF16),
            scratch_shapes=[pltpu.VMEM((LANES, 2 * tq), BF16)] + scratch,
            compiler_params=_params(("parallel", "parallel", "arbitrary")),
            name="attn_" + mode + ("_stab" if stabilised else ""),
        )

    return lax.cond(score_bound <= SAFE_SCORE_RANGE, call(False), call(True), q, k, v, *extra)


def _score_bound(q_gain, k_gain, dim):
    return (jnp.max(jnp.abs(q_gain)) * jnp.max(jnp.abs(k_gain))
            * (dim * dim ** -0.5 * LOG2E * ROUNDING_SLACK))


def _rope_tables(seq):
    assert MLA_ROPE == GQA_DIM // 2 and seq % GRID_W == 0
    pos = jnp.arange(seq, dtype=jnp.int32)
    zeros = lambda n: jnp.zeros((seq, n), F32)

    def angles(dim):
        inv = ROPE_THETA ** (-jnp.arange(0, dim, 2, dtype=F32) / dim)
        ang = pos.astype(F32)[:, None] * inv[None, :]
        return lax.optimization_barrier((jnp.cos(ang), jnp.sin(ang)))

    c, s = angles(DIFF_D)
    full = (jnp.tile(jnp.concatenate([c, c], 1), (1, 2)), jnp.tile(jnp.concatenate([-s, s], 1), (1, 2)))
    c, s = angles(MLA_ROPE)
    tail = zeros(LANES - MLA_QK)
    mla = (jnp.concatenate([jnp.ones((seq, MLA_NOPE), F32), c, c, tail], 1),
           jnp.concatenate([zeros(MLA_NOPE), -s, s, tail], 1))
    rows = seq // GRID_W
    cr, sr = jnp.repeat(c[:rows], GRID_W, axis=0), jnp.repeat(s[:rows], GRID_W, axis=0)
    cc, sc = jnp.tile(c[:GRID_W], (rows, 1)), jnp.tile(s[:GRID_W], (rows, 1))
    axial = (jnp.tile(jnp.concatenate([cr, cr, cc, cc], 1), (1, 2)),
             jnp.tile(jnp.concatenate([-sr, sr, -sc, sc], 1), (1, 2)))
    return full, mla, axial


def kernel(x, ffn1_norm, ffn1_w_gu, ffn1_w_down, ffn2_norm, ffn2_w_gu, ffn2_w_down, ev_norm, ev_w_in, ev_sgu_norm, ev_w_s, ev_b_s, ev_q_norm, ev_k_norm, ev_lam_q1, ev_lam_k1, ev_lam_q2, ev_lam_k2, ev_sub_norm, ev_w_out, od_norm, od_w_in, od_cq_norm, od_ckv_norm, od_w_uq, od_w_ukv, od_mla_q_norm, od_mla_k_norm, od_gqa_q_norm, od_gqa_k_norm, od_w_out):
    b, seq, d = x.shape
    t = b * seq
    full_tab, mla_tab, axial_tab = _rope_tables(seq)
    x2 = x.reshape(t, d)
    ffn1 = (ffn1_norm[:, None, :], ffn1_w_gu, ffn1_w_down)
    ffn2 = (ffn2_norm[:, None, :], ffn2_w_gu, ffn2_w_down)

    x2 = _ffn_call(x2, 0, *ffn1)
    out_a, q, k, v = _even_proj_call(x2, seq, ev_norm[0], ev_w_in[0], ev_sgu_norm[0], ev_w_s[0],
                                     ev_b_s[0], ev_q_norm[0], ev_k_norm[0], full_tab)
    lam_init = 0.8 - 0.6 * float(np.exp(-0.3 * 0))
    lam_vecs = jnp.stack([ev_lam_q1[0], ev_lam_k1[0], ev_lam_q2[0], ev_lam_k2[0]])
    shp = (b, seq, GMLP_WIDTH)
    out_b = _attention_call(q.reshape(shp), k.reshape(shp), v.reshape(shp), "diff",
                            _score_bound(ev_q_norm[0], ev_k_norm[0], DIFF_D), extra=(lam_vecs, ev_sub_norm[0].reshape(DIFF_V, 1)), lam_init=lam_init)
    w_out = ev_w_out[0].astype(BF16)
    x2 = _ffn_call(x2, 0, *ffn2,
                   mix=(out_a, out_b.reshape(t, -1), w_out[:GMLP_WIDTH], w_out[GMLP_WIDTH:]))

    x2 = _ffn_call(x2, 1, *ffn1)
    qc, kc, vc, qd, kd, vd = _odd_proj_call(
        x2, seq, od_norm[0], od_w_in[0], od_cq_norm[0], od_ckv_norm[0], od_w_uq[0], od_w_ukv[0],
        od_mla_q_norm[0], od_mla_k_norm[0], od_gqa_q_norm[0], od_gqa_k_norm[0], mla_tab, axial_tab)
    r3 = lambda a: a.reshape(b, seq, a.shape[1])
    out_c = _attention_call(r3(qc), r3(kc), r3(vc), "mla",
                            _score_bound(od_mla_q_norm[0], od_mla_k_norm[0], MLA_QK))
    out_d = _attention_call(r3(qd), r3(kd), r3(vd), "gqa",
                            _score_bound(od_gqa_q_norm[0], od_gqa_k_norm[0], GQA_DIM))
    w_out = od_w_out[0].astype(BF16)
    n_c = MLA_HEADS * MLA_V
    w_d = w_out[n_c:].reshape(GQA_KV_HEADS, GQA_GROUP, GQA_DIM, d).transpose(1, 0, 2, 3)
    x2 = _ffn_call(x2, 1, *ffn2,
                   mix=(out_c.reshape(t, -1), out_d.reshape(t, -1), w_out[:n_c],
                        w_d.reshape(GQA_Q_HEADS * GQA_DIM, d)))
    return x2.reshape(b, seq, d)
```

```python
import functools
import math

import numpy as np
import jax
import jax.numpy as jnp
from jax import lax
from jax.experimental import pallas as pl
from jax.experimental.pallas import tpu as pltpu

D_MODEL = 1024
D_FF = 2816
ROPE_THETA = 10000.0
GRID_W = 64
EPS = 1e-6
GMLP_GROUPS = 8
GMLP_GROUP_DIM = 64
GMLP_CHUNK = 128
GMLP_WIDTH = GMLP_GROUPS * GMLP_GROUP_DIM
DIFF_D = 64
DIFF_V = 128
MLA_HEADS = 8
MLA_Q_RANK = 256
MLA_KV_RANK = 128
MLA_NOPE = 64
MLA_ROPE = 32
MLA_V = 64
MLA_QK = MLA_NOPE + MLA_ROPE
GQA_Q_HEADS = 8
GQA_KV_HEADS = 2
GQA_GROUP = GQA_Q_HEADS // GQA_KV_HEADS
GQA_DIM = 64

LANES = 128
VMEM_LIMIT_BYTES = 56 * 1024 * 1024

FFN_TM = 1024
FFN_SUB = 256
FFN_CAST_CHUNKS = 11
PROJ_TM = 1024
ATT_TQ = 2048
ATT_TK = 256
ATT_CW = 512
ATT_PAIRS = 3
SUM_ROWS = 16
LOG2E = math.log2(math.e)
SAFE_SCORE_RANGE = 60.0
ROUNDING_SLACK = 1.0 + 2.0 ** -6

BF16 = jnp.bfloat16
F32 = jnp.float32


def _params(semantics):
    return pltpu.CompilerParams(dimension_semantics=semantics,
                                vmem_limit_bytes=VMEM_LIMIT_BYTES)


def _const_spec(shape):
    nd = len(shape)
    return pl.BlockSpec(shape, lambda *_: (0,) * nd, pipeline_mode=pl.Buffered(1))


def _rms_rows(x, gain):
    ms = jnp.mean(x * x, axis=-1, keepdims=True)
    return x * lax.rsqrt(ms + EPS) * gain


def _dot(a, b):
    return jnp.dot(a, b, preferred_element_type=F32)


def _lane_is_low(shape):
    return lax.broadcasted_iota(jnp.int32, shape, len(shape) - 1) < (LANES // 2)


def _inv_rms_half_blocks(x):
    low = _lane_is_low(x.shape)
    x2 = x * x
    x2_lo = jnp.where(low, x2, 0.0)
    x2_hi = x2 - x2_lo
    ms_lo = jnp.sum(x2_lo, axis=-1, keepdims=True) * (2.0 / LANES)
    ms_hi = jnp.sum(x2_hi, axis=-1, keepdims=True) * (2.0 / LANES)
    return jnp.where(low, lax.rsqrt(ms_lo + EPS), lax.rsqrt(ms_hi + EPS))


def _inv_rms_padded_block(x, width):
    ms = jnp.sum(x * x, axis=-1, keepdims=True) * (1.0 / width)
    return lax.rsqrt(ms + EPS)


def _swap_pairs(a, group):
    shape = a.shape
    a = a.reshape(shape[:-1] + (shape[-1] // group, 2, group // 2))
    return jnp.flip(a, axis=-2).reshape(shape)


def _ffn_body(x, g_ref, wgu_ref, wd_ref, o_ref):
    for r in range(x.shape[0] // FFN_SUB):
        rows = slice(r * FFN_SUB, (r + 1) * FFN_SUB)
        xr = x[rows]
        xn = _rms_rows(xr, g_ref[...]).astype(BF16)
        h = _dot(xn, wgu_ref[...])
        gate = h[:, :D_FF]
        up = h[:, D_FF:]
        act = (gate / (1.0 + jnp.exp(-gate)) * up).astype(BF16)
        o_ref[rows, :] = xr + 0.5 * _dot(act, wd_ref[...])


def _load_ffn_weights(layer, wgu_hbm, wd_hbm, wgu_ref, wd_ref, gu_stage, d_stage, sem):
    gu_cols = 2 * D_FF // FFN_CAST_CHUNKS
    d_rows = D_FF // FFN_CAST_CHUNKS

    def gu_copy(c, slot):
        return pltpu.make_async_copy(wgu_hbm.at[layer, :, pl.ds(c * gu_cols, gu_cols)],
                                     gu_stage.at[slot], sem.at[0, slot])

    def d_copy(c, slot):
        return pltpu.make_async_copy(wd_hbm.at[layer, pl.ds(c * d_rows, d_rows), :],
                                     d_stage.at[slot], sem.at[1, slot])

    gu_copy(0, 0).start()
    d_copy(0, 0).start()
    for c in range(FFN_CAST_CHUNKS):
        slot = c % 2
        if c + 1 < FFN_CAST_CHUNKS:
            gu_copy(c + 1, 1 - slot).start()
            d_copy(c + 1, 1 - slot).start()
        gu_copy(c, slot).wait()
        wgu_ref[:, c * gu_cols:(c + 1) * gu_cols] = gu_stage[slot].astype(BF16)
        d_copy(c, slot).wait()
        wd_ref[c * d_rows:(c + 1) * d_rows, :] = d_stage[slot].astype(BF16)


def _ffn_kernel(x_ref, g_ref, wgu_hbm, wd_hbm, o_ref, wgu_ref, wd_ref, gu_stage, d_stage, sem, *, layer):
    @pl.when(pl.program_id(0) == 0)
    def _():
        _load_ffn_weights(layer, wgu_hbm, wd_hbm, wgu_ref, wd_ref, gu_stage, d_stage, sem)

    _ffn_body(x_ref[...], g_ref, wgu_ref, wd_ref, o_ref)


def _mix_ffn_kernel(x_ref, a_ref, b_ref, wa_ref, wb_ref, g_ref, wgu_hbm, wd_hbm, o_ref,
                    wgu_ref, wd_ref, gu_stage, d_stage, sem, *, layer):
    @pl.when(pl.program_id(0) == 0)
    def _():
        _load_ffn_weights(layer, wgu_hbm, wd_hbm, wgu_ref, wd_ref, gu_stage, d_stage, sem)

    x = x_ref[...] + _dot(a_ref[...], wa_ref[...]) + _dot(b_ref[...], wb_ref[...])
    _ffn_body(x, g_ref, wgu_ref, wd_ref, o_ref)


def _layer_spec(shape, layer):
    nd = len(shape) - 1
    return pl.BlockSpec((None,) + tuple(shape[1:]), lambda *_: (layer,) + (0,) * nd,
                        pipeline_mode=pl.Buffered(1))


def _ffn_call(x, layer, gains, w_gu, w_down, mix=None):
    t = x.shape[0]
    tm = FFN_TM
    row = lambda i: (i, 0)
    x_spec = pl.BlockSpec((tm, D_MODEL), row)
    hbm = pl.BlockSpec(memory_space=pl.ANY)
    w_args = [gains, w_gu, w_down]
    w_specs = [_layer_spec(gains.shape, layer), hbm, hbm]
    if mix is None:
        kern, in_specs, args = _ffn_kernel, [x_spec] + w_specs, [x] + w_args
    else:
        a, b, wa, wb = mix
        kern = _mix_ffn_kernel
        in_specs = [x_spec, pl.BlockSpec((tm, a.shape[1]), row), pl.BlockSpec((tm, b.shape[1]), row),
                    _const_spec(wa.shape), _const_spec(wb.shape)] + w_specs
        args = [x, a, b, wa, wb] + w_args
    return pl.pallas_call(
        functools.partial(kern, layer=layer), grid=(t // tm,), in_specs=in_specs,
        out_specs=pl.BlockSpec((tm, D_MODEL), row),
        out_shape=jax.ShapeDtypeStruct((t, D_MODEL), F32),
        scratch_shapes=[pltpu.VMEM((D_MODEL, 2 * D_FF), BF16), pltpu.VMEM((D_FF, D_MODEL), BF16),
                        pltpu.VMEM((2, D_MODEL, 2 * D_FF // FFN_CAST_CHUNKS), F32),
                        pltpu.VMEM((2, D_FF // FFN_CAST_CHUNKS, D_MODEL), F32),
                        pltpu.SemaphoreType.DMA((2, 2))],
        compiler_params=_params(("arbitrary",)),
        name="ffn" if mix is None else "mix_ffn",
    )(*args)


def _gelu_tanh(x):
    c = math.sqrt(2.0 / math.pi)
    return 0.5 * x * (1.0 + jnp.tanh(c * (x + 0.044715 * (x * x * x))))


def _even_proj_kernel(x_ref, g_ref, win_ref, sgu_ref, wpair_ref, bias_ref, qg_ref, kg_ref,
                      cos_ref, sin_ref, oa_ref, q_ref, k_ref, v_ref):
    xn = _rms_rows(x_ref[...], g_ref[...]).astype(BF16)
    proj = _dot(xn, win_ref[...])
    w = GMLP_WIDTH
    nblk = w // LANES
    tm = proj.shape[0]
    for j in range(nblk):
        u = _gelu_tanh(proj[:, j * LANES:(j + 1) * LANES])
        v = _gelu_tanh(proj[:, w + j * LANES:w + (j + 1) * LANES])
        vn = v * _inv_rms_half_blocks(v) * sgu_ref[:, j * LANES:(j + 1) * LANES]
        low = _lane_is_low(vn.shape)
        vn_lo = jnp.where(low, vn, 0.0).astype(BF16)
        vn_hi = jnp.where(low, 0.0, vn).astype(BF16)
        wp = wpair_ref[j]
        bias = bias_ref[:, j * LANES:(j + 1) * LANES]
        for c in range(tm // GMLP_CHUNK):
            rows = slice(c * GMLP_CHUNK, (c + 1) * GMLP_CHUNK)
            stacked = jnp.concatenate([vn_lo[rows], vn_hi[rows]], axis=0)
            mixed = _dot(wp, stacked) + bias
            oa_ref[rows, j * LANES:(j + 1) * LANES] = (u[rows] * mixed).astype(BF16)
    cos, sin = cos_ref[...], sin_ref[...]
    q_cos, q_sin = qg_ref[0:1] * cos, qg_ref[1:2] * sin
    k_cos, k_sin = kg_ref[0:1] * cos, kg_ref[1:2] * sin
    for j in range(nblk):
        blk = slice(j * LANES, (j + 1) * LANES)
        qb = proj[:, 2 * w + j * LANES:2 * w + (j + 1) * LANES]
        kb = proj[:, 3 * w + j * LANES:3 * w + (j + 1) * LANES]
        qp = proj[:, 5 * w + j * LANES:5 * w + (j + 1) * LANES]
        kp = proj[:, 6 * w + j * LANES:6 * w + (j + 1) * LANES]
        qr = _inv_rms_half_blocks(qb) * (LOG2E * DIFF_D ** -0.5)
        q_ref[:, blk] = ((qb * q_cos + qp * q_sin) * qr).astype(BF16)
        k_ref[:, blk] = ((kb * k_cos + kp * k_sin) * _inv_rms_half_blocks(kb)).astype(BF16)
    v_ref[...] = proj[:, 4 * w:5 * w].astype(BF16)


def _gain_pair(g, group):
    return jnp.stack([jnp.tile(g, 2), jnp.tile(_swap_pairs(g, group), 2)])


def _even_proj_call(x, seq, gain, w_in, sgu_norm, w_s, b_s, q_norm, k_norm, tables):
    t = x.shape[0]
    tm = PROJ_TM
    w = GMLP_WIDTH
    row = lambda i: (i, 0)
    pos = lambda i: (i % (seq // tm), 0)
    wpair = jnp.concatenate([w_s[0::2], w_s[1::2]], axis=2).astype(BF16)
    bias = jnp.repeat(b_s.T, GMLP_GROUP_DIM, axis=1)
    qk_cols = w_in[:, 2 * w:4 * w]
    win = jnp.concatenate([w_in, _swap_pairs(qk_cols, DIFF_D)], axis=1).astype(BF16)
    args = [x, gain.reshape(1, D_MODEL), win, sgu_norm.reshape(1, w), wpair, bias,
            _gain_pair(q_norm, DIFF_D), _gain_pair(k_norm, DIFF_D), *tables]
    in_specs = [pl.BlockSpec((tm, D_MODEL), row)] + [_const_spec(a.shape) for a in args[1:8]]
    in_specs += [pl.BlockSpec((tm, LANES), pos)] * 2
    out = jax.ShapeDtypeStruct((t, w), BF16)
    return pl.pallas_call(
        _even_proj_kernel, grid=(t // tm,), in_specs=in_specs,
        out_specs=[pl.BlockSpec((tm, w), row)] * 4, out_shape=[out] * 4,
        compiler_params=_params(("parallel",)), name="even_proj",
    )(*args)


def _odd_proj_kernel(x_ref, g_ref, win_ref, cqg_ref, ckvg_ref, wuq_ref, wuk_ref, wuv_ref,
                     mqg_ref, mkg_ref, gqg_ref, gkg_ref, mcos_ref, msin_ref, acos_ref, asin_ref,
                     qc_ref, kc_ref, vc_ref, qd_ref, kd_ref, vd_ref):
    xn = _rms_rows(x_ref[...], g_ref[...]).astype(BF16)
    proj = _dot(xn, win_ref[...])
    nq = GQA_Q_HEADS * GQA_DIM
    o1 = MLA_Q_RANK
    o2 = o1 + MLA_KV_RANK
    o3 = o2 + 2 * LANES
    o4 = o3 + 2 * nq
    o5 = o4 + 2 * LANES
    mcos, msin = mcos_ref[...], msin_ref[...]
    q_cos, q_sin = mqg_ref[0:1] * mcos, mqg_ref[1:2] * msin
    k_cos, k_sin = mkg_ref[0:1] * mcos, mkg_ref[1:2] * msin
    cq = _rms_rows(proj[:, :o1], cqg_ref[...]).astype(BF16)
    q_all = _dot(cq, wuq_ref[...])
    ckv = _rms_rows(proj[:, o1:o2], ckvg_ref[...]).astype(BF16)
    kn_all = _dot(ckv, wuk_ref[...])
    vc_ref[...] = _dot(ckv, wuv_ref[...]).astype(BF16)
    kpe = proj[:, o2:o2 + LANES]
    kpe_sin = proj[:, o2 + LANES:o3] * k_sin
    nh = MLA_HEADS * LANES
    for h in range(MLA_HEADS):
        blk = slice(h * LANES, (h + 1) * LANES)
        qb = q_all[:, blk]
        qp = q_all[:, nh + h * LANES:nh + (h + 1) * LANES]
        qr = _inv_rms_padded_block(qb, MLA_QK) * (LOG2E * MLA_QK ** -0.5)
        qc_ref[:, blk] = ((qb * q_cos + qp * q_sin) * qr).astype(BF16)
        kb = kn_all[:, blk] + kpe
        kc_ref[:, blk] = ((kb * k_cos + kpe_sin) * _inv_rms_padded_block(kb, MLA_QK)).astype(BF16)
    acos, asin = acos_ref[...], asin_ref[...]
    q_cos, q_sin = gqg_ref[0:1] * acos, gqg_ref[1:2] * asin
    for j in range(nq // LANES):
        blk = slice(j * LANES, (j + 1) * LANES)
        qb = proj[:, o3 + j * LANES:o3 + (j + 1) * LANES]
        qp = proj[:, o3 + nq + j * LANES:o3 + nq + (j + 1) * LANES]
        qr = _inv_rms_half_blocks(qb) * (LOG2E * GQA_DIM ** -0.5)
        qd_ref[:, blk] = ((qb * q_cos + qp * q_sin) * qr).astype(BF16)
    kb = proj[:, o4:o4 + LANES]
    kp = proj[:, o4 + LANES:o5]
    kd_ref[...] = ((kb * (gkg_ref[0:1] * acos) + kp * (gkg_ref[1:2] * asin))
                   * _inv_rms_half_blocks(kb)).astype(BF16)
    vd_ref[...] = proj[:, o5:].astype(BF16)


def _pad_heads(w, heads, width):
    r = w.shape[0]
    w = w.reshape(r, heads, width)
    return jnp.pad(w, ((0, 0), (0, 0), (0, LANES - width))).reshape(r, heads * LANES)


def _rope_partner_cols(w, heads):
    r = w.shape[0]
    w = w.reshape(r, heads, MLA_QK)
    rope = _swap_pairs(w[:, :, MLA_NOPE:], MLA_ROPE)
    return jnp.concatenate([jnp.zeros_like(w[:, :, :MLA_NOPE]), rope], axis=2).reshape(r, heads * MLA_QK)


def _odd_proj_call(x, seq, gain, w_in, cq_norm, ckv_norm, w_uq, w_ukv, mq_norm, mk_norm,
                   gq_norm, gk_norm, mla_tables, axial_tables):
    t = x.shape[0]
    tm = PROJ_TM
    row = lambda i: (i, 0)
    pos = lambda i: (i % (seq // tm), 0)
    o1 = MLA_Q_RANK
    o2 = o1 + MLA_KV_RANK
    o3 = o2 + MLA_ROPE
    o4 = o3 + GQA_Q_HEADS * GQA_DIM
    o5 = o4 + GQA_KV_HEADS * GQA_DIM
    place_rope = lambda c: jnp.pad(c, ((0, 0), (MLA_NOPE, LANES - MLA_QK)))
    kpe_cols = w_in[:, o2:o3]
    gq_cols = w_in[:, o3:o4].reshape(D_MODEL, GQA_KV_HEADS, GQA_GROUP, GQA_DIM)
    gq_cols = gq_cols.transpose(0, 2, 1, 3).reshape(D_MODEL, GQA_Q_HEADS * GQA_DIM)
    gk_cols = w_in[:, o4:o5]
    half = GQA_DIM // 2
    win = jnp.concatenate(
        [w_in[:, :o2], place_rope(kpe_cols), place_rope(_swap_pairs(kpe_cols, MLA_ROPE)),
         gq_cols, _swap_pairs(gq_cols, half), gk_cols, _swap_pairs(gk_cols, half), w_in[:, o5:]],
        axis=1).astype(BF16)
    wuq = jnp.concatenate([_pad_heads(w_uq, MLA_HEADS, MLA_QK),
                           _pad_heads(_rope_partner_cols(w_uq, MLA_HEADS), MLA_HEADS, MLA_QK)],
                          axis=1).astype(BF16)
    w_ukv = w_ukv.reshape(MLA_KV_RANK, MLA_HEADS, MLA_NOPE + MLA_V)
    wuk = _pad_heads(w_ukv[:, :, :MLA_NOPE].reshape(MLA_KV_RANK, -1), MLA_HEADS, MLA_NOPE).astype(BF16)
    wuv = w_ukv[:, :, MLA_NOPE:].reshape(MLA_KV_RANK, MLA_HEADS * MLA_V).astype(BF16)

    def mla_gain_pair(g):
        partner = jnp.concatenate([g[:MLA_NOPE], _swap_pairs(g[MLA_NOPE:], MLA_ROPE)])
        return jnp.pad(jnp.stack([g, partner]), ((0, 0), (0, LANES - MLA_QK)))

    args = [x, gain.reshape(1, D_MODEL), win, cq_norm.reshape(1, -1), ckv_norm.reshape(1, -1),
            wuq, wuk, wuv, mla_gain_pair(mq_norm), mla_gain_pair(mk_norm),
            _gain_pair(gq_norm, half), _gain_pair(gk_norm, half), *mla_tables, *axial_tables]
    in_specs = ([pl.BlockSpec((tm, D_MODEL), row)]
                + [_const_spec(a.shape) for a in args[1:12]]
                + [pl.BlockSpec((tm, LANES), pos)] * 4)
    widths = [MLA_HEADS * LANES, MLA_HEADS * LANES, MLA_HEADS * MLA_V,
              GQA_Q_HEADS * GQA_DIM, LANES, LANES]
    return pl.pallas_call(
        _odd_proj_kernel, grid=(t // tm,), in_specs=in_specs,
        out_specs=[pl.BlockSpec((tm, n), row) for n in widths],
        out_shape=[jax.ShapeDtypeStruct((t, n), BF16) for n in widths],
        compiler_params=_params(("parallel",)), name="odd_proj",
    )(*args)


def _attention_kernel(*refs, mode, lam_init, stabilised):
    if mode == "diff":
        q_ref, k_ref, v_ref, lam_ref, sub_ref, o_ref = refs[:6]
    else:
        q_ref, k_ref, v_ref, o_ref = refs[:4]
    qt_ref, tile_buf, stat_ref, acc_ref = refs[-4:]
    tq = q_ref.shape[0]
    seq = k_ref.shape[0]
    tk = ATT_TK
    cw = ATT_CW
    n = seq // tk
    q = q_ref[...].astype(F32)
    if mode == "mla":
        qt_ref[:, :tq] = q[:, :LANES].T.astype(BF16)
        qt_ref[:, tq:] = q[:, LANES:].T.astype(BF16)
    else:
        qt = q.T
        first = lax.broadcasted_iota(jnp.int32, qt.shape, 0) < (LANES // 2)
        qt_ref[:, :tq] = jnp.where(first, qt, 0.0).astype(BF16)
        qt_ref[:, tq:] = jnp.where(first, 0.0, qt).astype(BF16)

    chunks = [slice(c * cw, (c + 1) * cw) for c in range(2 * tq // cw)]

    def tile_rows(t):
        return pl.ds(pl.multiple_of(t * tk, tk), tk)

    def ahead(kt, slot, cols):
        kc = kt
        if mode == "mla":
            kc = kt[:, :LANES] if cols.start < tq else kt[:, LANES:]
        s = _dot(kc, qt_ref[:, cols])
        if stabilised:
            tile_buf[slot, :, cols] = s
        else:
            p = jnp.exp2(s)
            stat_ref[:, cols] += jnp.sum(p.reshape(tk // 8, 8, p.shape[1]), axis=0)
            tile_buf[slot, :, cols] = p.astype(BF16)

    def consume(vt, slot, cols):
        if stabilised:
            s = tile_buf[slot, :, cols]
            m_old = stat_ref[:, cols]
            m_new = jnp.maximum(m_old, jnp.max(s, axis=0, keepdims=True))
            alpha = jnp.exp2(m_old - m_new)
            p = jnp.exp2(s - m_new).astype(BF16)
            stat_ref[:, cols] = m_new
            acc_ref[:, cols] = alpha * acc_ref[:, cols] + _dot(vt, p)
        else:
            acc_ref[:, cols] += _dot(vt, tile_buf[slot, :, cols])

    def value_rows(t):
        vt = v_ref[tile_rows(t), :].T
        parts = (vt, vt) if mode == "diff" else (vt[:LANES // 2], vt[LANES // 2:])
        if not stabilised:
            return parts
        ones = jnp.ones((SUM_ROWS, tk), BF16)
        return tuple(jnp.concatenate([part, ones], axis=0) for part in parts)

    def half_step(t, t_next, cur, nxt):
        kt = k_ref[tile_rows(t_next), :]
        vt = value_rows(t)
        for cols in chunks:
            ahead(kt, nxt, cols)
            consume(vt[0] if cols.start < tq else vt[1], cur, cols)

    stat_ref[...] = jnp.full(stat_ref.shape, -jnp.inf if stabilised else 0.0, F32)
    acc_ref[...] = jnp.zeros(acc_ref.shape, F32)
    kt0 = k_ref[tile_rows(0), :]
    for cols in chunks:
        ahead(kt0, 0, cols)

    def body(i, carry):
        for u in range(ATT_PAIRS):
            t = 2 * (i * ATT_PAIRS + u)
            half_step(t, t + 1, 0, 1)
            half_step(t + 1, t + 2, 1, 0)
        return carry

    lax.fori_loop(0, (n - 2) // (2 * ATT_PAIRS), body, 0)
    half_step(n - 2, n - 1, 0, 1)
    vt_last = value_rows(n - 1)
    for cols in chunks:
        consume(vt_last[0] if cols.start < tq else vt_last[1], 1, cols)

    if stabilised:
        nv = acc_ref.shape[0] - SUM_ROWS
        o = acc_ref[:nv, :] / acc_ref[nv:nv + 1, :]
    else:
        o = acc_ref[...] / jnp.sum(stat_ref[...], axis=0, keepdims=True)
    if mode == "diff":
        lv = lam_ref[...]
        lam = (jnp.exp(jnp.sum(lv[0:1] * lv[1:2], axis=-1, keepdims=True))
               - jnp.exp(jnp.sum(lv[2:3] * lv[3:4], axis=-1, keepdims=True)) + lam_init)
        ot = o[:, :tq] - lam * o[:, tq:]
        ms = jnp.mean(ot * ot, axis=0, keepdims=True)
        ot = ot * lax.rsqrt(ms + EPS) * (sub_ref[...] * (1.0 - lam_init))
        o_ref[...] = ot.T.astype(o_ref.dtype)
    else:
        ot = jnp.concatenate([o[:, :tq], o[:, tq:]], axis=0).T
        o_ref[...] = ot.astype(o_ref.dtype)


def _attention_call(q, k, v, mode, score_bound, extra=(), lam_init=0.0):
    b, seq, _ = q.shape
    qw = 2 * LANES if mode == "mla" else LANES
    nblk = q.shape[2] // qw
    tq, tk = ATT_TQ, ATT_TK
    assert seq % tk == 0 and (seq // tk - 2) % (2 * ATT_PAIRS) == 0 and seq % tq == 0 and tq % ATT_CW == 0
    shared = mode == "gqa"
    value_rows = LANES if mode == "diff" else LANES // 2
    kv_idx = (lambda bi, j, i: (bi, 0, 0)) if shared else (lambda bi, j, i: (bi, 0, j))
    in_specs = [pl.BlockSpec((None, tq, qw), lambda bi, j, i: (bi, i, j)),
                pl.BlockSpec((None, seq, qw), kv_idx),
                pl.BlockSpec((None, seq, LANES), kv_idx)]
    in_specs += [_const_spec(e.shape) for e in extra]

    def call(stabilised):
        if stabilised:
            scratch = [pltpu.VMEM((2, tk, 2 * tq), F32), pltpu.VMEM((1, 2 * tq), F32),
                       pltpu.VMEM((value_rows + SUM_ROWS, 2 * tq), F32)]
        else:
            scratch = [pltpu.VMEM((2, tk, 2 * tq), BF16), pltpu.VMEM((8, 2 * tq), F32),
                       pltpu.VMEM((value_rows, 2 * tq), F32)]
        return pl.pallas_call(
            functools.partial(_attention_kernel, mode=mode, lam_init=lam_init, stabilised=stabilised),
            grid=(b, nblk, seq // tq), in_specs=in_specs,
            out_specs=pl.BlockSpec((None, tq, LANES), lambda bi, j, i: (bi, i, j)),
            out_shape=jax.ShapeDtypeStruct((b, seq, nblk * LANES), BF16),
            scratch_shapes=[pltpu.VMEM((LANES, 2 * tq), BF16)] + scratch,
            compiler_params=_params(("parallel", "parallel", "arbitrary")),
            name="attn_" + mode + ("_stab" if stabilised else ""),
        )

    return lax.cond(score_bound <= SAFE_SCORE_RANGE, call(False), call(True), q, k, v, *extra)


def _score_bound(q_gain, k_gain, dim):
    return (jnp.max(jnp.abs(q_gain)) * jnp.max(jnp.abs(k_gain))
            * (dim * dim ** -0.5 * LOG2E * ROUNDING_SLACK))


def _rope_tables(seq):
    assert MLA_ROPE == GQA_DIM // 2 and seq % GRID_W == 0
    pos = jnp.arange(seq, dtype=jnp.int32)
    zeros = lambda n: jnp.zeros((seq, n), F32)

    def angles(dim):
        inv = ROPE_THETA ** (-jnp.arange(0, dim, 2, dtype=F32) / dim)
        ang = pos.astype(F32)[:, None] * inv[None, :]
        return lax.optimization_barrier((jnp.cos(ang), jnp.sin(ang)))

    c, s = angles(DIFF_D)
    full = (jnp.tile(jnp.concatenate([c, c], 1), (1, 2)), jnp.tile(jnp.concatenate([-s, s], 1), (1, 2)))
    c, s = angles(MLA_ROPE)
    tail = zeros(LANES - MLA_QK)
    mla = (jnp.concatenate([jnp.ones((seq, MLA_NOPE), F32), c, c, tail], 1),
           jnp.concatenate([zeros(MLA_NOPE), -s, s, tail], 1))
    rows = seq // GRID_W
    cr, sr = jnp.repeat(c[:rows], GRID_W, axis=0), jnp.repeat(s[:rows], GRID_W, axis=0)
    cc, sc = jnp.tile(c[:GRID_W], (rows, 1)), jnp.tile(s[:GRID_W], (rows, 1))
    axial = (jnp.tile(jnp.concatenate([cr, cr, cc, cc], 1), (1, 2)),
             jnp.tile(jnp.concatenate([-sr, sr, -sc, sc], 1), (1, 2)))
    return full, mla, axial


def kernel(x, ffn1_norm, ffn1_w_gu, ffn1_w_down, ffn2_norm, ffn2_w_gu, ffn2_w_down, ev_norm, ev_w_in, ev_sgu_norm, ev_w_s, ev_b_s, ev_q_norm, ev_k_norm, ev_lam_q1, ev_lam_k1, ev_lam_q2, ev_lam_k2, ev_sub_norm, ev_w_out, od_norm, od_w_in, od_cq_norm, od_ckv_norm, od_w_uq, od_w_ukv, od_mla_q_norm, od_mla_k_norm, od_gqa_q_norm, od_gqa_k_norm, od_w_out):
    b, seq, d = x.shape
    t = b * seq
    full_tab, mla_tab, axial_tab = _rope_tables(seq)
    x2 = x.reshape(t, d)
    ffn1 = (ffn1_norm[:, None, :], ffn1_w_gu, ffn1_w_down)
    ffn2 = (ffn2_norm[:, None, :], ffn2_w_gu, ffn2_w_down)

    x2 = _ffn_call(x2, 0, *ffn1)
    out_a, q, k, v = _even_proj_call(x2, seq, ev_norm[0], ev_w_in[0], ev_sgu_norm[0], ev_w_s[0],
                                     ev_b_s[0], ev_q_norm[0], ev_k_norm[0], full_tab)
    lam_init = 0.8 - 0.6 * float(np.exp(-0.3 * 0))
    lam_vecs = jnp.stack([ev_lam_q1[0], ev_lam_k1[0], ev_lam_q2[0], ev_lam_k2[0]])
    shp = (b, seq, GMLP_WIDTH)
    out_b = _attention_call(q.reshape(shp), k.reshape(shp), v.reshape(shp), "diff",
                            _score_bound(ev_q_norm[0], ev_k_norm[0], DIFF_D), extra=(lam_vecs, ev_sub_norm[0].reshape(DIFF_V, 1)), lam_init=lam_init)
    w_out = ev_w_out[0].astype(BF16)
    x2 = _ffn_call(x2, 0, *ffn2,
                   mix=(out_a, out_b.reshape(t, -1), w_out[:GMLP_WIDTH], w_out[GMLP_WIDTH:]))

    x2 = _ffn_call(x2, 1, *ffn1)
    qc, kc, vc, qd, kd, vd = _odd_proj_call(
        x2, seq, od_norm[0], od_w_in[0], od_cq_norm[0], od_ckv_norm[0], od_w_uq[0], od_w_ukv[0],
        od_mla_q_norm[0], od_mla_k_norm[0], od_gqa_q_norm[0], od_gqa_k_norm[0], mla_tab, axial_tab)
    r3 = lambda a: a.reshape(b, seq, a.shape[1])
    out_c = _attention_call(r3(qc), r3(kc), r3(vc), "mla",
                            _score_bound(od_mla_q_norm[0], od_mla_k_norm[0], MLA_QK))
    out_d = _attention_call(r3(qd), r3(kd), r3(vd), "gqa",
                            _score_bound(od_gqa_q_norm[0], od_gqa_k_norm[0], GQA_DIM))
    w_out = od_w_out[0].astype(BF16)
    n_c = MLA_HEADS * MLA_V
    w_d = w_out[n_c:].reshape(GQA_KV_HEADS, GQA_GROUP, GQA_DIM, d).transpose(1, 0, 2, 3)
    x2 = _ffn_call(x2, 1, *ffn2,
                   mix=(out_c.reshape(t, -1), out_d.reshape(t, -1), w_out[:n_c],
                        w_d.reshape(GQA_Q_HEADS * GQA_DIM, d)))
    return x2.reshape(b, seq, d)
```

```python
import functools
import math

import numpy as np
import jax
import jax.numpy as jnp
from jax import lax
from jax.experimental import pallas as pl
from jax.experimental.pallas import tpu as pltpu

D_MODEL = 1024
D_FF = 2816
ROPE_THETA = 10000.0
GRID_W = 64
EPS = 1e-6
GMLP_GROUPS = 8
GMLP_GROUP_DIM = 64
GMLP_CHUNK = 128
GMLP_WIDTH = GMLP_GROUPS * GMLP_GROUP_DIM
DIFF_D = 64
DIFF_V = 128
MLA_HEADS = 8
MLA_Q_RANK = 256
MLA_KV_RANK = 128
MLA_NOPE = 64
MLA_ROPE = 32
MLA_V = 64
MLA_QK = MLA_NOPE + MLA_ROPE
GQA_Q_HEADS = 8
GQA_KV_HEADS = 2
GQA_GROUP = GQA_Q_HEADS // GQA_KV_HEADS
GQA_DIM = 64

LANES = 128
VMEM_LIMIT_BYTES = 56 * 1024 * 1024

FFN_TM = 1024
FFN_SUB = 256
FFN_CAST_CHUNKS = 11
PROJ_TM = 512
ATT_TQ = 2048
ATT_QTILES = 4
ATT_TK = 256
ATT_CW = 512
ATT_PAIRS = 3
SUM_ROWS = 16
LOG2E = math.log2(math.e)
SAFE_SCORE_RANGE = 60.0
ROUNDING_SLACK = 1.0 + 2.0 ** -6

BF16 = jnp.bfloat16
F32 = jnp.float32


def _params(semantics):
    return pltpu.CompilerParams(dimension_semantics=semantics,
                                vmem_limit_bytes=VMEM_LIMIT_BYTES)


def _const_spec(shape):
    nd = len(shape)
    return pl.BlockSpec(shape, lambda *_: (0,) * nd, pipeline_mode=pl.Buffered(1))


def _rms_rows(x, gain):
    ms = jnp.mean(x * x, axis=-1, keepdims=True)
    return x * lax.rsqrt(ms + EPS) * gain


def _dot(a, b):
    return jnp.dot(a, b, preferred_element_type=F32)


def _lane_is_low(shape):
    return lax.broadcasted_iota(jnp.int32, shape, len(shape) - 1) < (LANES // 2)


def _inv_rms_half_blocks(x):
    low = _lane_is_low(x.shape)
    x2 = x * x
    x2_lo = jnp.where(low, x2, 0.0)
    x2_hi = x2 - x2_lo
    ms_lo = jnp.sum(x2_lo, axis=-1, keepdims=True) * (2.0 / LANES)
    ms_hi = jnp.sum(x2_hi, axis=-1, keepdims=True) * (2.0 / LANES)
    return jnp.where(low, lax.rsqrt(ms_lo + EPS), lax.rsqrt(ms_hi + EPS))


def _inv_rms_padded_block(x, width):
    ms = jnp.sum(x * x, axis=-1, keepdims=True) * (1.0 / width)
    return lax.rsqrt(ms + EPS)


def _swap_pairs(a, group):
    shape = a.shape
    a = a.reshape(shape[:-1] + (shape[-1] // group, 2, group // 2))
    return jnp.flip(a, axis=-2).reshape(shape)


def _ffn_body(x, g_ref, wgu_ref, wd_ref, o_ref):
    for r in range(x.shape[0] // FFN_SUB):
        rows = slice(r * FFN_SUB, (r + 1) * FFN_SUB)
        xr = x[rows]
        xn = _rms_rows(xr, g_ref[...]).astype(BF16)
        h = _dot(xn, wgu_ref[...])
        gate = h[:, :D_FF]
        up = h[:, D_FF:]
        act = (gate / (1.0 + jnp.exp(-gate)) * up).astype(BF16)
        o_ref[rows, :] = xr + 0.5 * _dot(act, wd_ref[...])


def _load_ffn_weights(layer, wgu_hbm, wd_hbm, wgu_ref, wd_ref, gu_stage, d_stage, sem):
    gu_cols = 2 * D_FF // FFN_CAST_CHUNKS
    d_rows = D_FF // FFN_CAST_CHUNKS

    def gu_copy(c, slot):
        return pltpu.make_async_copy(wgu_hbm.at[layer, :, pl.ds(c * gu_cols, gu_cols)],
                                     gu_stage.at[slot], sem.at[0, slot])

    def d_copy(c, slot):
        return pltpu.make_async_copy(wd_hbm.at[layer, pl.ds(c * d_rows, d_rows), :],
                                     d_stage.at[slot], sem.at[1, slot])

    gu_copy(0, 0).start()
    d_copy(0, 0).start()
    for c in range(FFN_CAST_CHUNKS):
        slot = c % 2
        if c + 1 < FFN_CAST_CHUNKS:
            gu_copy(c + 1, 1 - slot).start()
            d_copy(c + 1, 1 - slot).start()
        gu_copy(c, slot).wait()
        wgu_ref[:, c * gu_cols:(c + 1) * gu_cols] = gu_stage[slot].astype(BF16)
        d_copy(c, slot).wait()
        wd_ref[c * d_rows:(c + 1) * d_rows, :] = d_stage[slot].astype(BF16)


def _ffn_kernel(x_ref, g_ref, wgu_hbm, wd_hbm, o_ref, wgu_ref, wd_ref, gu_stage, d_stage, sem, *, layer):
    @pl.when(pl.program_id(0) == 0)
    def _():
        _load_ffn_weights(layer, wgu_hbm, wd_hbm, wgu_ref, wd_ref, gu_stage, d_stage, sem)

    _ffn_body(x_ref[...], g_ref, wgu_ref, wd_ref, o_ref)


def _mix_ffn_kernel(x_ref, a_ref, b_ref, wa_ref, wb_ref, g_ref, wgu_hbm, wd_hbm, o_ref,
                    wgu_ref, wd_ref, gu_stage, d_stage, sem, *, layer):
    @pl.when(pl.program_id(0) == 0)
    def _():
        _load_ffn_weights(layer, wgu_hbm, wd_hbm, wgu_ref, wd_ref, gu_stage, d_stage, sem)

    x = x_ref[...] + _dot(a_ref[...], wa_ref[...]) + _dot(b_ref[...], wb_ref[...])
    _ffn_body(x, g_ref, wgu_ref, wd_ref, o_ref)


def _layer_spec(shape, layer):
    nd = len(shape) - 1
    return pl.BlockSpec((None,) + tuple(shape[1:]), lambda *_: (layer,) + (0,) * nd,
                        pipeline_mode=pl.Buffered(1))


def _ffn_call(x, layer, gains, w_gu, w_down, mix=None):
    t = x.shape[0]
    tm = FFN_TM
    row = lambda i: (i, 0)
    x_spec = pl.BlockSpec((tm, D_MODEL), row)
    hbm = pl.BlockSpec(memory_space=pl.ANY)
    w_args = [gains, w_gu, w_down]
    w_specs = [_layer_spec(gains.shape, layer), hbm, hbm]
    if mix is None:
        kern, in_specs, args = _ffn_kernel, [x_spec] + w_specs, [x] + w_args
    else:
        a, b, wa, wb = mix
        kern = _mix_ffn_kernel
        in_specs = [x_spec, pl.BlockSpec((tm, a.shape[1]), row), pl.BlockSpec((tm, b.shape[1]), row),
                    _const_spec(wa.shape), _const_spec(wb.shape)] + w_specs
        args = [x, a, b, wa, wb] + w_args
    return pl.pallas_call(
        functools.partial(kern, layer=layer), grid=(t // tm,), in_specs=in_specs,
        out_specs=pl.BlockSpec((tm, D_MODEL), row),
        out_shape=jax.ShapeDtypeStruct((t, D_MODEL), F32),
        scratch_shapes=[pltpu.VMEM((D_MODEL, 2 * D_FF), BF16), pltpu.VMEM((D_FF, D_MODEL), BF16),
                        pltpu.VMEM((2, D_MODEL, 2 * D_FF // FFN_CAST_CHUNKS), F32),
                        pltpu.VMEM((2, D_FF // FFN_CAST_CHUNKS, D_MODEL), F32),
                        pltpu.SemaphoreType.DMA((2, 2))],
        compiler_params=_params(("arbitrary",)),
        name="ffn" if mix is None else "mix_ffn",
    )(*args)


def _gelu_tanh(x):
    c = math.sqrt(2.0 / math.pi)
    return 0.5 * x * (1.0 + jnp.tanh(c * (x + 0.044715 * (x * x * x))))


def _even_proj_kernel(x_ref, g_ref, win_ref, sgu_ref, wpair_ref, bias_ref, qg_ref, kg_ref,
                      cos_ref, sin_ref, oa_ref, q_ref, k_ref, v_ref):
    xn = _rms_rows(x_ref[...], g_ref[...]).astype(BF16)
    proj = _dot(xn, win_ref[...])
    w = GMLP_WIDTH
    nblk = w // LANES
    tm = proj.shape[0]
    for j in range(nblk):
        u = _gelu_tanh(proj[:, j * LANES:(j + 1) * LANES])
        v = _gelu_tanh(proj[:, w + j * LANES:w + (j + 1) * LANES])
        vn = v * _inv_rms_half_blocks(v) * sgu_ref[:, j * LANES:(j + 1) * LANES]
        low = _lane_is_low(vn.shape)
        vn_lo = jnp.where(low, vn, 0.0).astype(BF16)
        vn_hi = jnp.where(low, 0.0, vn).astype(BF16)
        wp = wpair_ref[j]
        bias = bias_ref[:, j * LANES:(j + 1) * LANES]
        for c in range(tm // GMLP_CHUNK):
            rows = slice(c * GMLP_CHUNK, (c + 1) * GMLP_CHUNK)
            stacked = jnp.concatenate([vn_lo[rows], vn_hi[rows]], axis=0)
            mixed = _dot(wp, stacked) + bias
            oa_ref[rows, j * LANES:(j + 1) * LANES] = (u[rows] * mixed).astype(BF16)
    cos, sin = cos_ref[...], sin_ref[...]
    q_cos, q_sin = qg_ref[0:1] * cos, qg_ref[1:2] * sin
    k_cos, k_sin = kg_ref[0:1] * cos, kg_ref[1:2] * sin
    for j in range(nblk):
        blk = slice(j * LANES, (j + 1) * LANES)
        qb = proj[:, 2 * w + j * LANES:2 * w + (j + 1) * LANES]
        kb = proj[:, 3 * w + j * LANES:3 * w + (j + 1) * LANES]
        qp = proj[:, 5 * w + j * LANES:5 * w + (j + 1) * LANES]
        kp = proj[:, 6 * w + j * LANES:6 * w + (j + 1) * LANES]
        qr = _inv_rms_half_blocks(qb) * (LOG2E * DIFF_D ** -0.5)
        q_ref[:, blk] = ((qb * q_cos + qp * q_sin) * qr).astype(BF16)
        k_ref[:, blk] = ((kb * k_cos + kp * k_sin) * _inv_rms_half_blocks(kb)).astype(BF16)
    v_ref[...] = proj[:, 4 * w:5 * w].astype(BF16)


def _gain_pair(g, group):
    return jnp.stack([jnp.tile(g, 2), jnp.tile(_swap_pairs(g, group), 2)])


def _even_proj_call(x, seq, gain, w_in, sgu_norm, w_s, b_s, q_norm, k_norm, tables):
    t = x.shape[0]
    tm = PROJ_TM
    w = GMLP_WIDTH
    row = lambda i: (i, 0)
    pos = lambda i: (i % (seq // tm), 0)
    wpair = jnp.concatenate([w_s[0::2], w_s[1::2]], axis=2).astype(BF16)
    bias = jnp.repeat(b_s.T, GMLP_GROUP_DIM, axis=1)
    qk_cols = w_in[:, 2 * w:4 * w]
    win = jnp.concatenate([w_in, _swap_pairs(qk_cols, DIFF_D)], axis=1).astype(BF16)
    args = [x, gain.reshape(1, D_MODEL), win, sgu_norm.reshape(1, w), wpair, bias,
            _gain_pair(q_norm, DIFF_D), _gain_pair(k_norm, DIFF_D), *tables]
    in_specs = [pl.BlockSpec((tm, D_MODEL), row)] + [_const_spec(a.shape) for a in args[1:8]]
    in_specs += [pl.BlockSpec((tm, LANES), pos)] * 2
    out = jax.ShapeDtypeStruct((t, w), BF16)
    return pl.pallas_call(
        _even_proj_kernel, grid=(t // tm,), in_specs=in_specs,
        out_specs=[pl.BlockSpec((tm, w), row)] * 4, out_shape=[out] * 4,
        compiler_params=_params(("parallel",)), name="even_proj",
    )(*args)


def _odd_proj_kernel(x_ref, g_ref, win_ref, cqg_ref, ckvg_ref, wuq_ref, wuk_ref, wuv_ref,
                     mqg_ref, mkg_ref, gqg_ref, gkg_ref, mcos_ref, msin_ref, acos_ref, asin_ref,
                     qc_ref, kc_ref, vc_ref, qd_ref, kd_ref, vd_ref):
    xn = _rms_rows(x_ref[...], g_ref[...]).astype(BF16)
    proj = _dot(xn, win_ref[...])
    nq = GQA_Q_HEADS * GQA_DIM
    o1 = MLA_Q_RANK
    o2 = o1 + MLA_KV_RANK
    o3 = o2 + 2 * LANES
    o4 = o3 + 2 * nq
    o5 = o4 + 2 * LANES
    mcos, msin = mcos_ref[...], msin_ref[...]
    q_cos, q_sin = mqg_ref[0:1] * mcos, mqg_ref[1:2] * msin
    k_cos, k_sin = mkg_ref[0:1] * mcos, mkg_ref[1:2] * msin
    cq = _rms_rows(proj[:, :o1], cqg_ref[...]).astype(BF16)
    q_all = _dot(cq, wuq_ref[...])
    ckv = _rms_rows(proj[:, o1:o2], ckvg_ref[...]).astype(BF16)
    kn_all = _dot(ckv, wuk_ref[...])
    vc_ref[...] = _dot(ckv, wuv_ref[...]).astype(BF16)
    kpe = proj[:, o2:o2 + LANES]
    kpe_sin = proj[:, o2 + LANES:o3] * k_sin
    nh = MLA_HEADS * LANES
    for h in range(MLA_HEADS):
        blk = slice(h * LANES, (h + 1) * LANES)
        qb = q_all[:, blk]
        qp = q_all[:, nh + h * LANES:nh + (h + 1) * LANES]
        qr = _inv_rms_padded_block(qb, MLA_QK) * (LOG2E * MLA_QK ** -0.5)
        qc_ref[:, blk] = ((qb * q_cos + qp * q_sin) * qr).astype(BF16)
        kb = kn_all[:, blk] + kpe
        kc_ref[:, blk] = ((kb * k_cos + kpe_sin) * _inv_rms_padded_block(kb, MLA_QK)).astype(BF16)
    acos, asin = acos_ref[...], asin_ref[...]
    q_cos, q_sin = gqg_ref[0:1] * acos, gqg_ref[1:2] * asin
    for j in range(nq // LANES):
        blk = slice(j * LANES, (j + 1) * LANES)
        qb = proj[:, o3 + j * LANES:o3 + (j + 1) * LANES]
        qp = proj[:, o3 + nq + j * LANES:o3 + nq + (j + 1) * LANES]
        qr = _inv_rms_half_blocks(qb) * (LOG2E * GQA_DIM ** -0.5)
        qd_ref[:, blk] = ((qb * q_cos + qp * q_sin) * qr).astype(BF16)
    kb = proj[:, o4:o4 + LANES]
    kp = proj[:, o4 + LANES:o5]
    kd_ref[...] = ((kb * (gkg_ref[0:1] * acos) + kp * (gkg_ref[1:2] * asin))
                   * _inv_rms_half_blocks(kb)).astype(BF16)
    vd_ref[...] = proj[:, o5:].astype(BF16)


def _pad_heads(w, heads, width):
    r = w.shape[0]
    w = w.reshape(r, heads, width)
    return jnp.pad(w, ((0, 0), (0, 0), (0, LANES - width))).reshape(r, heads * LANES)


def _rope_partner_cols(w, heads):
    r = w.shape[0]
    w = w.reshape(r, heads, MLA_QK)
    rope = _swap_pairs(w[:, :, MLA_NOPE:], MLA_ROPE)
    return jnp.concatenate([jnp.zeros_like(w[:, :, :MLA_NOPE]), rope], axis=2).reshape(r, heads * MLA_QK)


def _odd_proj_call(x, seq, gain, w_in, cq_norm, ckv_norm, w_uq, w_ukv, mq_norm, mk_norm,
                   gq_norm, gk_norm, mla_tables, axial_tables):
    t = x.shape[0]
    tm = PROJ_TM
    row = lambda i: (i, 0)
    pos = lambda i: (i % (seq // tm), 0)
    o1 = MLA_Q_RANK
    o2 = o1 + MLA_KV_RANK
    o3 = o2 + MLA_ROPE
    o4 = o3 + GQA_Q_HEADS * GQA_DIM
    o5 = o4 + GQA_KV_HEADS * GQA_DIM
    place_rope = lambda c: jnp.pad(c, ((0, 0), (MLA_NOPE, LANES - MLA_QK)))
    kpe_cols = w_in[:, o2:o3]
    gq_cols = w_in[:, o3:o4].reshape(D_MODEL, GQA_KV_HEADS, GQA_GROUP, GQA_DIM)
    gq_cols = gq_cols.transpose(0, 2, 1, 3).reshape(D_MODEL, GQA_Q_HEADS * GQA_DIM)
    gk_cols = w_in[:, o4:o5]
    half = GQA_DIM // 2
    win = jnp.concatenate(
        [w_in[:, :o2], place_rope(kpe_cols), place_rope(_swap_pairs(kpe_cols, MLA_ROPE)),
         gq_cols, _swap_pairs(gq_cols, half), gk_cols, _swap_pairs(gk_cols, half), w_in[:, o5:]],
        axis=1).astype(BF16)
    wuq = jnp.concatenate([_pad_heads(w_uq, MLA_HEADS, MLA_QK),
                           _pad_heads(_rope_partner_cols(w_uq, MLA_HEADS), MLA_HEADS, MLA_QK)],
                          axis=1).astype(BF16)
    w_ukv = w_ukv.reshape(MLA_KV_RANK, MLA_HEADS, MLA_NOPE + MLA_V)
    wuk = _pad_heads(w_ukv[:, :, :MLA_NOPE].reshape(MLA_KV_RANK, -1), MLA_HEADS, MLA_NOPE).astype(BF16)
    wuv = w_ukv[:, :, MLA_NOPE:].reshape(MLA_KV_RANK, MLA_HEADS * MLA_V).astype(BF16)

    def mla_gain_pair(g):
        partner = jnp.concatenate([g[:MLA_NOPE], _swap_pairs(g[MLA_NOPE:], MLA_ROPE)])
        return jnp.pad(jnp.stack([g, partner]), ((0, 0), (0, LANES - MLA_QK)))

    args = [x, gain.reshape(1, D_MODEL), win, cq_norm.reshape(1, -1), ckv_norm.reshape(1, -1),
            wuq, wuk, wuv, mla_gain_pair(mq_norm), mla_gain_pair(mk_norm),
            _gain_pair(gq_norm, half), _gain_pair(gk_norm, half), *mla_tables, *axial_tables]
    in_specs = ([pl.BlockSpec((tm, D_MODEL), row)]
                + [_const_spec(a.shape) for a in args[1:12]]
                + [pl.BlockSpec((tm, LANES), pos)] * 4)
    widths = [MLA_HEADS * LANES, MLA_HEADS * LANES, MLA_HEADS * MLA_V,
              GQA_Q_HEADS * GQA_DIM, LANES, LANES]
    return pl.pallas_call(
        _odd_proj_kernel, grid=(t // tm,), in_specs=in_specs,
        out_specs=[pl.BlockSpec((tm, n), row) for n in widths],
        out_shape=[jax.ShapeDtypeStruct((t, n), BF16) for n in widths],
        compiler_params=_params(("parallel",)), name="odd_proj",
    )(*args)


def _attention_kernel(*refs, mode, lam_init, stabilised):
    if mode == "diff":
        q_ref, k_ref, v_ref, lam_ref, sub_ref, o_ref = refs[:6]
    else:
        q_ref, k_ref, v_ref, o_ref = refs[:4]
    qt_ref, tile_buf, stat_ref, acc_ref = refs[-4:]
    tq = ATT_TQ
    n_qt = q_ref.shape[0] // tq
    seq = k_ref.shape[0]
    tk = ATT_TK
    cw = ATT_CW
    n = seq // tk
    chunks = [slice(c * cw, (c + 1) * cw) for c in range(2 * tq // cw)]

    def tile_rows(t):
        return pl.ds(pl.multiple_of(t * tk, tk), tk)

    def ahead(p, kt, slot, cols):
        kc = kt
        if mode == "mla":
            kc = kt[:, :LANES] if cols.start < tq else kt[:, LANES:]
        s = _dot(kc, qt_ref[p, :, cols])
        if stabilised:
            tile_buf[slot, :, cols] = s
        else:
            pr = jnp.exp2(s)
            stat_ref[p, :, cols] += jnp.sum(pr.reshape(tk // 8, 8, pr.shape[1]), axis=0)
            tile_buf[slot, :, cols] = pr.astype(BF16)

    def consume(p, vt, slot, cols):
        if stabilised:
            s = tile_buf[slot, :, cols]
            m_old = stat_ref[p, :, cols]
            m_new = jnp.maximum(m_old, jnp.max(s, axis=0, keepdims=True))
            alpha = jnp.exp2(m_old - m_new)
            pr = jnp.exp2(s - m_new).astype(BF16)
            stat_ref[p, :, cols] = m_new
            acc_ref[p, :, cols] = alpha * acc_ref[p, :, cols] + _dot(vt, pr)
        else:
            acc_ref[p, :, cols] += _dot(vt, tile_buf[slot, :, cols])

    def value_rows(t):
        vt = v_ref[tile_rows(t), :].T
        parts = (vt, vt) if mode == "diff" else (vt[:LANES // 2], vt[LANES // 2:])
        if not stabilised:
            return parts
        ones = jnp.ones((SUM_ROWS, tk), BF16)
        return tuple(jnp.concatenate([part, ones], axis=0) for part in parts)

    def half_step(p, t, t_next, cur, nxt):
        kt = k_ref[tile_rows(t_next), :]
        vt = value_rows(t)
        for cols in chunks:
            ahead(p, kt, nxt, cols)
            consume(p, vt[0] if cols.start < tq else vt[1], cur, cols)

    def prepare(u):
        p = u % 2
        q = q_ref[u * tq:(u + 1) * tq, :].astype(F32)
        if mode == "mla":
            qt_ref[p, :, :tq] = q[:, :LANES].T.astype(BF16)
            qt_ref[p, :, tq:] = q[:, LANES:].T.astype(BF16)
        else:
            qt = q.T
            first = lax.broadcasted_iota(jnp.int32, qt.shape, 0) < (LANES // 2)
            qt_ref[p, :, :tq] = jnp.where(first, qt, 0.0).astype(BF16)
            qt_ref[p, :, tq:] = jnp.where(first, 0.0, qt).astype(BF16)
        stat_ref[p] = jnp.full(stat_ref.shape[1:], -jnp.inf if stabilised else 0.0, F32)
        acc_ref[p] = jnp.zeros(acc_ref.shape[1:], F32)
        kt0 = k_ref[tile_rows(0), :]
        for cols in chunks:
            ahead(p, kt0, 0, cols)

    def finalize(u):
        p = u % 2
        if stabilised:
            nv = acc_ref.shape[1] - SUM_ROWS
            o = acc_ref[p, :nv, :] / acc_ref[p, nv:nv + 1, :]
        else:
            o = acc_ref[p] / jnp.sum(stat_ref[p], axis=0, keepdims=True)
        rows = slice(u * tq, (u + 1) * tq)
        if mode == "diff":
            lv = lam_ref[...]
            lam = (jnp.exp(jnp.sum(lv[0:1] * lv[1:2], axis=-1, keepdims=True))
                   - jnp.exp(jnp.sum(lv[2:3] * lv[3:4], axis=-1, keepdims=True)) + lam_init)
            ot = o[:, :tq] - lam * o[:, tq:]
            ms = jnp.mean(ot * ot, axis=0, keepdims=True)
            ot = ot * lax.rsqrt(ms + EPS) * (sub_ref[...] * (1.0 - lam_init))
            o_ref[rows, :] = ot.T.astype(o_ref.dtype)
        else:
            ot = jnp.concatenate([o[:, :tq], o[:, tq:]], axis=0).T
            o_ref[rows, :] = ot.astype(o_ref.dtype)

    prepare(0)
    for u in range(n_qt):
        p = u % 2

        def body(i, carry, p=p):
            for w in range(ATT_PAIRS):
                t = 2 * (i * ATT_PAIRS + w)
                half_step(p, t, t + 1, 0, 1)
                half_step(p, t + 1, t + 2, 1, 0)
            return carry

        lax.fori_loop(0, (n - 2) // (2 * ATT_PAIRS), body, 0)
        half_step(p, n - 2, n - 1, 0, 1)
        vt_last = value_rows(n - 1)
        for cols in chunks:
            consume(p, vt_last[0] if cols.start < tq else vt_last[1], 1, cols)
        if u + 1 < n_qt:
            prepare(u + 1)
        finalize(u)


def _attention_call(q, k, v, mode, score_bound, extra=(), lam_init=0.0):
    b, seq, _ = q.shape
    qw = 2 * LANES if mode == "mla" else LANES
    nblk = q.shape[2] // qw
    tq, tk = ATT_TQ, ATT_TK
    assert seq % tk == 0 and (seq // tk - 2) % (2 * ATT_PAIRS) == 0 and seq % (tq * ATT_QTILES) == 0 and tq % ATT_CW == 0
    shared = mode == "gqa"
    value_rows = LANES if mode == "diff" else LANES // 2
    kv_idx = (lambda bi, j, i: (bi, 0, 0)) if shared else (lambda bi, j, i: (bi, 0, j))
    tqs = tq * ATT_QTILES
    in_specs = [pl.BlockSpec((None, tqs, qw), lambda bi, j, i: (bi, i, j)),
                pl.BlockSpec((None, seq, qw), kv_idx),
                pl.BlockSpec((None, seq, LANES), kv_idx)]
    in_specs += [_const_spec(e.shape) for e in extra]

    def call(stabilised):
        if stabilised:
            scratch = [pltpu.VMEM((2, tk, 2 * tq), F32), pltpu.VMEM((2, 1, 2 * tq), F32),
                       pltpu.VMEM((2, value_rows + SUM_ROWS, 2 * tq), F32)]
        else:
            scratch = [pltpu.VMEM((2, tk, 2 * tq), BF16), pltpu.VMEM((2, 8, 2 * tq), F32),
                       pltpu.VMEM((2, value_rows, 2 * tq), F32)]
        return pl.pallas_call(
            functools.partial(_attention_kernel, mode=mode, lam_init=lam_init, stabilised=stabilised),
            grid=(b, nblk, seq // tqs), in_specs=in_specs,
            out_specs=pl.BlockSpec((None, tqs, LANES), lambda bi, j, i: (bi, i, j)),
            out_shape=jax.ShapeDtypeStruct((b, seq, nblk * LANES), BF16),
            scratch_shapes=[pltpu.VMEM((2, LANES, 2 * tq), BF16)] + scratch,
            compiler_params=_params(("parallel", "parallel", "arbitrary")),
            name="attn_" + mode + ("_stab" if stabilised else ""),
        )

    return lax.cond(score_bound <= SAFE_SCORE_RANGE, call(False), call(True), q, k, v, *extra)


def _score_bound(q_gain, k_gain, dim):
    return (jnp.max(jnp.abs(q_gain)) * jnp.max(jnp.abs(k_gain))
            * (dim * dim ** -0.5 * LOG2E * ROUNDING_SLACK))


def _rope_tables(seq):
    assert MLA_ROPE == GQA_DIM // 2 and seq % GRID_W == 0
    pos = jnp.arange(seq, dtype=jnp.int32)
    zeros = lambda n: jnp.zeros((seq, n), F32)

    def angles(dim):
        inv = ROPE_THETA ** (-jnp.arange(0, dim, 2, dtype=F32) / dim)
        ang = pos.astype(F32)[:, None] * inv[None, :]
        return lax.optimization_barrier((jnp.cos(ang), jnp.sin(ang)))

    c, s = angles(DIFF_D)
    full = (jnp.tile(jnp.concatenate([c, c], 1), (1, 2)), jnp.tile(jnp.concatenate([-s, s], 1), (1, 2)))
    c, s = angles(MLA_ROPE)
    tail = zeros(LANES - MLA_QK)
    mla = (jnp.concatenate([jnp.ones((seq, MLA_NOPE), F32), c, c, tail], 1),
           jnp.concatenate([zeros(MLA_NOPE), -s, s, tail], 1))
    rows = seq // GRID_W
    cr, sr = jnp.repeat(c[:rows], GRID_W, axis=0), jnp.repeat(s[:rows], GRID_W, axis=0)
    cc, sc = jnp.tile(c[:GRID_W], (rows, 1)), jnp.tile(s[:GRID_W], (rows, 1))
    axial = (jnp.tile(jnp.concatenate([cr, cr, cc, cc], 1), (1, 2)),
             jnp.tile(jnp.concatenate([-sr, sr, -sc, sc], 1), (1, 2)))
    return full, mla, axial


def kernel(x, ffn1_norm, ffn1_w_gu, ffn1_w_down, ffn2_norm, ffn2_w_gu, ffn2_w_down, ev_norm, ev_w_in, ev_sgu_norm, ev_w_s, ev_b_s, ev_q_norm, ev_k_norm, ev_lam_q1, ev_lam_k1, ev_lam_q2, ev_lam_k2, ev_sub_norm, ev_w_out, od_norm, od_w_in, od_cq_norm, od_ckv_norm, od_w_uq, od_w_ukv, od_mla_q_norm, od_mla_k_norm, od_gqa_q_norm, od_gqa_k_norm, od_w_out):
    b, seq, d = x.shape
    t = b * seq
    full_tab, mla_tab, axial_tab = _rope_tables(seq)
    x2 = x.reshape(t, d)
    ffn1 = (ffn1_norm[:, None, :], ffn1_w_gu, ffn1_w_down)
    ffn2 = (ffn2_norm[:, None, :], ffn2_w_gu, ffn2_w_down)

    x2 = _ffn_call(x2, 0, *ffn1)
    out_a, q, k, v = _even_proj_call(x2, seq, ev_norm[0], ev_w_in[0], ev_sgu_norm[0], ev_w_s[0],
                                     ev_b_s[0], ev_q_norm[0], ev_k_norm[0], full_tab)
    lam_init = 0.8 - 0.6 * float(np.exp(-0.3 * 0))
    lam_vecs = jnp.stack([ev_lam_q1[0], ev_lam_k1[0], ev_lam_q2[0], ev_lam_k2[0]])
    shp = (b, seq, GMLP_WIDTH)
    out_b = _attention_call(q.reshape(shp), k.reshape(shp), v.reshape(shp), "diff",
                            _score_bound(ev_q_norm[0], ev_k_norm[0], DIFF_D), extra=(lam_vecs, ev_sub_norm[0].reshape(DIFF_V, 1)), lam_init=lam_init)
    w_out = ev_w_out[0].astype(BF16)
    x2 = _ffn_call(x2, 0, *ffn2,
                   mix=(out_a, out_b.reshape(t, -1), w_out[:GMLP_WIDTH], w_out[GMLP_WIDTH:]))

    x2 = _ffn_call(x2, 1, *ffn1)
    qc, kc, vc, qd, kd, vd = _odd_proj_call(
        x2, seq, od_norm[0], od_w_in[0], od_cq_norm[0], od_ckv_norm[0], od_w_uq[0], od_w_ukv[0],
        od_mla_q_norm[0], od_mla_k_norm[0], od_gqa_q_norm[0], od_gqa_k_norm[0], mla_tab, axial_tab)
    r3 = lambda a: a.reshape(b, seq, a.shape[1])
    out_c = _attention_call(r3(qc), r3(kc), r3(vc), "mla",
                            _score_bound(od_mla_q_norm[0], od_mla_k_norm[0], MLA_QK))
    out_d = _attention_call(r3(qd), r3(kd), r3(vd), "gqa",
                            _score_bound(od_gqa_q_norm[0], od_gqa_k_norm[0], GQA_DIM))
    w_out = od_w_out[0].astype(BF16)
    n_c = MLA_HEADS * MLA_V
    w_d = w_out[n_c:].reshape(GQA_KV_HEADS, GQA_GROUP, GQA_DIM, d).transpose(1, 0, 2, 3)
    x2 = _ffn_call(x2, 1, *ffn2,
                   mix=(out_c.reshape(t, -1), out_d.reshape(t, -1), w_out[:n_c],
                        w_d.reshape(GQA_Q_HEADS * GQA_DIM, d)))
    return x2.reshape(b, seq, d)
```

```python
import functools
import math

import numpy as np
import jax
import jax.numpy as jnp
from jax import lax
from jax.experimental import pallas as pl
from jax.experimental.pallas import tpu as pltpu

D_MODEL = 1024
D_FF = 2816
ROPE_THETA = 10000.0
GRID_W = 64
EPS = 1e-6
GMLP_GROUPS = 8
GMLP_GROUP_DIM = 64
GMLP_CHUNK = 128
GMLP_WIDTH = GMLP_GROUPS * GMLP_GROUP_DIM
DIFF_D = 64
DIFF_V = 128
MLA_HEADS = 8
MLA_Q_RANK = 256
MLA_KV_RANK = 128
MLA_NOPE = 64
MLA_ROPE = 32
MLA_V = 64
MLA_QK = MLA_NOPE + MLA_ROPE
GQA_Q_HEADS = 8
GQA_KV_HEADS = 2
GQA_GROUP = GQA_Q_HEADS // GQA_KV_HEADS
GQA_DIM = 64

LANES = 128
VMEM_LIMIT_BYTES = 56 * 1024 * 1024

FFN_TM = 1024
FFN_SUB = 256
FFN_CAST_CHUNKS = 11
PROJ_TM = 512
ATT_TQ = 2048
ATT_TK = 256
ATT_CW = 512
ATT_PAIRS = 3
SUM_ROWS = 16
LOG2E = math.log2(math.e)
SAFE_SCORE_RANGE = 60.0
ROUNDING_SLACK = 1.0 + 2.0 ** -6

BF16 = jnp.bfloat16
F32 = jnp.float32


def _params(semantics):
    return pltpu.CompilerParams(dimension_semantics=semantics,
                                vmem_limit_bytes=VMEM_LIMIT_BYTES)


def _const_spec(shape):
    nd = len(shape)
    return pl.BlockSpec(shape, lambda *_: (0,) * nd, pipeline_mode=pl.Buffered(1))


def _rms_rows(x, gain):
    ms = jnp.mean(x * x, axis=-1, keepdims=True)
    return x * lax.rsqrt(ms + EPS) * gain


def _dot(a, b):
    return jnp.dot(a, b, preferred_element_type=F32)


def _lane_is_low(shape):
    return lax.broadcasted_iota(jnp.int32, shape, len(shape) - 1) < (LANES // 2)


def _inv_rms_half_blocks(x):
    low = _lane_is_low(x.shape)
    x2 = x * x
    x2_lo = jnp.where(low, x2, 0.0)
    x2_hi = x2 - x2_lo
    ms_lo = jnp.sum(x2_lo, axis=-1, keepdims=True) * (2.0 / LANES)
    ms_hi = jnp.sum(x2_hi, axis=-1, keepdims=True) * (2.0 / LANES)
    return jnp.where(low, lax.rsqrt(ms_lo + EPS), lax.rsqrt(ms_hi + EPS))


def _inv_rms_padded_block(x, width):
    ms = jnp.sum(x * x, axis=-1, keepdims=True) * (1.0 / width)
    return lax.rsqrt(ms + EPS)


def _swap_pairs(a, group):
    shape = a.shape
    a = a.reshape(shape[:-1] + (shape[-1] // group, 2, group // 2))
    return jnp.flip(a, axis=-2).reshape(shape)


def _ffn_body(x, g_ref, wgu_ref, wd_ref, o_ref):
    for r in range(x.shape[0] // FFN_SUB):
        rows = slice(r * FFN_SUB, (r + 1) * FFN_SUB)
        xr = x[rows]
        xn = _rms_rows(xr, g_ref[...]).astype(BF16)
        h = _dot(xn, wgu_ref[...])
        gate = h[:, :D_FF]
        up = h[:, D_FF:]
        act = (gate / (1.0 + jnp.exp(-gate)) * up).astype(BF16)
        o_ref[rows, :] = xr + 0.5 * _dot(act, wd_ref[...])


def _load_ffn_weights(layer, wgu_hbm, wd_hbm, wgu_ref, wd_ref, gu_stage, d_stage, sem):
    gu_cols = 2 * D_FF // FFN_CAST_CHUNKS
    d_rows = D_FF // FFN_CAST_CHUNKS

    def gu_copy(c, slot):
        return pltpu.make_async_copy(wgu_hbm.at[layer, :, pl.ds(c * gu_cols, gu_cols)],
                                     gu_stage.at[slot], sem.at[0, slot])

    def d_copy(c, slot):
        return pltpu.make_async_copy(wd_hbm.at[layer, pl.ds(c * d_rows, d_rows), :],
                                     d_stage.at[slot], sem.at[1, slot])

    gu_copy(0, 0).start()
    d_copy(0, 0).start()
    for c in range(FFN_CAST_CHUNKS):
        slot = c % 2
        if c + 1 < FFN_CAST_CHUNKS:
            gu_copy(c + 1, 1 - slot).start()
            d_copy(c + 1, 1 - slot).start()
        gu_copy(c, slot).wait()
        wgu_ref[:, c * gu_cols:(c + 1) * gu_cols] = gu_stage[slot].astype(BF16)
        d_copy(c, slot).wait()
        wd_ref[c * d_rows:(c + 1) * d_rows, :] = d_stage[slot].astype(BF16)


def _ffn_kernel(x_ref, g_ref, wgu_hbm, wd_hbm, o_ref, wgu_ref, wd_ref, gu_stage, d_stage, sem, *, layer):
    @pl.when(pl.program_id(0) == 0)
    def _():
        _load_ffn_weights(layer, wgu_hbm, wd_hbm, wgu_ref, wd_ref, gu_stage, d_stage, sem)

    _ffn_body(x_ref[...], g_ref, wgu_ref, wd_ref, o_ref)


def _mix_ffn_kernel(x_ref, a_ref, b_ref, wa_ref, wb_ref, g_ref, wgu_hbm, wd_hbm, o_ref,
                    wgu_ref, wd_ref, gu_stage, d_stage, sem, *, layer):
    @pl.when(pl.program_id(0) == 0)
    def _():
        _load_ffn_weights(layer, wgu_hbm, wd_hbm, wgu_ref, wd_ref, gu_stage, d_stage, sem)

    x = x_ref[...] + _dot(a_ref[...], wa_ref[...]) + _dot(b_ref[...], wb_ref[...])
    _ffn_body(x, g_ref, wgu_ref, wd_ref, o_ref)


def _layer_spec(shape, layer):
    nd = len(shape) - 1
    return pl.BlockSpec((None,) + tuple(shape[1:]), lambda *_: (layer,) + (0,) * nd,
                        pipeline_mode=pl.Buffered(1))


def _ffn_call(x, layer, gains, w_gu, w_down, mix=None):
    t = x.shape[0]
    tm = FFN_TM
    row = lambda i: (i, 0)
    x_spec = pl.BlockSpec((tm, D_MODEL), row)
    hbm = pl.BlockSpec(memory_space=pl.ANY)
    w_args = [gains, w_gu, w_down]
    w_specs = [_layer_spec(gains.shape, layer), hbm, hbm]
    if mix is None:
        kern, in_specs, args = _ffn_kernel, [x_spec] + w_specs, [x] + w_args
    else:
        a, b, wa, wb = mix
        kern = _mix_ffn_kernel
        in_specs = [x_spec, pl.BlockSpec((tm, a.shape[1]), row), pl.BlockSpec((tm, b.shape[1]), row),
                    _const_spec(wa.shape), _const_spec(wb.shape)] + w_specs
        args = [x, a, b, wa, wb] + w_args
    return pl.pallas_call(
        functools.partial(kern, layer=layer), grid=(t // tm,), in_specs=in_specs,
        out_specs=pl.BlockSpec((tm, D_MODEL), row),
        out_shape=jax.ShapeDtypeStruct((t, D_MODEL), F32),
        scratch_shapes=[pltpu.VMEM((D_MODEL, 2 * D_FF), BF16), pltpu.VMEM((D_FF, D_MODEL), BF16),
                        pltpu.VMEM((2, D_MODEL, 2 * D_FF // FFN_CAST_CHUNKS), F32),
                        pltpu.VMEM((2, D_FF // FFN_CAST_CHUNKS, D_MODEL), F32),
                        pltpu.SemaphoreType.DMA((2, 2))],
        compiler_params=_params(("arbitrary",)),
        name="ffn" if mix is None else "mix_ffn",
    )(*args)


def _gelu_tanh(x):
    c = math.sqrt(2.0 / math.pi)
    return 0.5 * x * (1.0 + jnp.tanh(c * (x + 0.044715 * (x * x * x))))


def _even_proj_kernel(x_ref, g_ref, win_ref, sgu_ref, wpair_ref, bias_ref, qg_ref, kg_ref, perm_ref,
                      cos_ref, sin_ref, oa_ref, q_ref, k_ref, v_ref):
    xn = _rms_rows(x_ref[...], g_ref[...]).astype(BF16)
    proj = _dot(xn, win_ref[...])
    w = GMLP_WIDTH
    nblk = w // LANES
    tm = proj.shape[0]
    for j in range(nblk):
        u = _gelu_tanh(proj[:, j * LANES:(j + 1) * LANES])
        v = _gelu_tanh(proj[:, w + j * LANES:w + (j + 1) * LANES])
        vn = v * _inv_rms_half_blocks(v) * sgu_ref[:, j * LANES:(j + 1) * LANES]
        low = _lane_is_low(vn.shape)
        vn_lo = jnp.where(low, vn, 0.0).astype(BF16)
        vn_hi = jnp.where(low, 0.0, vn).astype(BF16)
        wp = wpair_ref[j]
        bias = bias_ref[:, j * LANES:(j + 1) * LANES]
        for c in range(tm // GMLP_CHUNK):
            rows = slice(c * GMLP_CHUNK, (c + 1) * GMLP_CHUNK)
            stacked = jnp.concatenate([vn_lo[rows], vn_hi[rows]], axis=0)
            mixed = _dot(wp, stacked) + bias
            oa_ref[rows, j * LANES:(j + 1) * LANES] = (u[rows] * mixed).astype(BF16)
    cos, sin = cos_ref[...], sin_ref[...]
    q_cos, q_sin = qg_ref[0:1] * cos, qg_ref[1:2] * sin
    k_cos, k_sin = kg_ref[0:1] * cos, kg_ref[1:2] * sin
    pw = perm_ref.shape[0]
    partner = jnp.concatenate(
        [_dot(proj[:, 2 * w + c * pw:2 * w + (c + 1) * pw].astype(BF16), perm_ref[...])
         for c in range(2 * w // pw)], axis=1)
    for j in range(nblk):
        blk = slice(j * LANES, (j + 1) * LANES)
        qb = proj[:, 2 * w + j * LANES:2 * w + (j + 1) * LANES]
        kb = proj[:, 3 * w + j * LANES:3 * w + (j + 1) * LANES]
        qp = partner[:, j * LANES:(j + 1) * LANES]
        kp = partner[:, w + j * LANES:w + (j + 1) * LANES]
        qr = _inv_rms_half_blocks(qb) * (LOG2E * DIFF_D ** -0.5)
        q_ref[:, blk] = ((qb * q_cos + qp * q_sin) * qr).astype(BF16)
        k_ref[:, blk] = ((kb * k_cos + kp * k_sin) * _inv_rms_half_blocks(kb)).astype(BF16)
    v_ref[...] = proj[:, 4 * w:].astype(BF16)


def _gain_pair(g, group):
    return jnp.stack([jnp.tile(g, 2), jnp.tile(_swap_pairs(g, group), 2)])


def _even_proj_call(x, seq, gain, w_in, sgu_norm, w_s, b_s, q_norm, k_norm, tables):
    t = x.shape[0]
    tm = PROJ_TM
    w = GMLP_WIDTH
    row = lambda i: (i, 0)
    pos = lambda i: (i % (seq // tm), 0)
    wpair = jnp.concatenate([w_s[0::2], w_s[1::2]], axis=2).astype(BF16)
    bias = jnp.repeat(b_s.T, GMLP_GROUP_DIM, axis=1)
    perm = _swap_pairs(jnp.eye(2 * LANES, dtype=F32), DIFF_D).astype(BF16)
    args = [x, gain.reshape(1, D_MODEL), w_in.astype(BF16), sgu_norm.reshape(1, w), wpair, bias,
            _gain_pair(q_norm, DIFF_D), _gain_pair(k_norm, DIFF_D), perm, *tables]
    in_specs = [pl.BlockSpec((tm, D_MODEL), row)] + [_const_spec(a.shape) for a in args[1:9]]
    in_specs += [pl.BlockSpec((tm, LANES), pos)] * 2
    out = jax.ShapeDtypeStruct((t, w), BF16)
    return pl.pallas_call(
        _even_proj_kernel, grid=(t // tm,), in_specs=in_specs,
        out_specs=[pl.BlockSpec((tm, w), row)] * 4, out_shape=[out] * 4,
        compiler_params=_params(("parallel",)), name="even_proj",
    )(*args)


def _odd_proj_kernel(x_ref, g_ref, win_ref, cqg_ref, ckvg_ref, wuq_ref, wuk_ref, wuv_ref,
                     mqg_ref, mkg_ref, gqg_ref, gkg_ref, mcos_ref, msin_ref, acos_ref, asin_ref,
                     qc_ref, kc_ref, vc_ref, qd_ref, kd_ref, vd_ref):
    xn = _rms_rows(x_ref[...], g_ref[...]).astype(BF16)
    proj = _dot(xn, win_ref[...])
    nq = GQA_Q_HEADS * GQA_DIM
    o1 = MLA_Q_RANK
    o2 = o1 + MLA_KV_RANK
    o3 = o2 + 2 * LANES
    o4 = o3 + 2 * nq
    o5 = o4 + 2 * LANES
    mcos, msin = mcos_ref[...], msin_ref[...]
    q_cos, q_sin = mqg_ref[0:1] * mcos, mqg_ref[1:2] * msin
    k_cos, k_sin = mkg_ref[0:1] * mcos, mkg_ref[1:2] * msin
    cq = _rms_rows(proj[:, :o1], cqg_ref[...]).astype(BF16)
    q_all = _dot(cq, wuq_ref[...])
    ckv = _rms_rows(proj[:, o1:o2], ckvg_ref[...]).astype(BF16)
    kn_all = _dot(ckv, wuk_ref[...])
    vc_ref[...] = _dot(ckv, wuv_ref[...]).astype(BF16)
    kpe = proj[:, o2:o2 + LANES]
    kpe_sin = proj[:, o2 + LANES:o3] * k_sin
    nh = MLA_HEADS * LANES
    for h in range(MLA_HEADS):
        blk = slice(h * LANES, (h + 1) * LANES)
        qb = q_all[:, blk]
        qp = q_all[:, nh + h * LANES:nh + (h + 1) * LANES]
        qr = _inv_rms_padded_block(qb, MLA_QK) * (LOG2E * MLA_QK ** -0.5)
        qc_ref[:, blk] = ((qb * q_cos + qp * q_sin) * qr).astype(BF16)
        kb = kn_all[:, blk] + kpe
        kc_ref[:, blk] = ((kb * k_cos + kpe_sin) * _inv_rms_padded_block(kb, MLA_QK)).astype(BF16)
    acos, asin = acos_ref[...], asin_ref[...]
    q_cos, q_sin = gqg_ref[0:1] * acos, gqg_ref[1:2] * asin
    for j in range(nq // LANES):
        blk = slice(j * LANES, (j + 1) * LANES)
        qb = proj[:, o3 + j * LANES:o3 + (j + 1) * LANES]
        qp = proj[:, o3 + nq + j * LANES:o3 + nq + (j + 1) * LANES]
        qr = _inv_rms_half_blocks(qb) * (LOG2E * GQA_DIM ** -0.5)
        qd_ref[:, blk] = ((qb * q_cos + qp * q_sin) * qr).astype(BF16)
    kb = proj[:, o4:o4 + LANES]
    kp = proj[:, o4 + LANES:o5]
    kd_ref[...] = ((kb * (gkg_ref[0:1] * acos) + kp * (gkg_ref[1:2] * asin))
                   * _inv_rms_half_blocks(kb)).astype(BF16)
    vd_ref[...] = proj[:, o5:].astype(BF16)


def _pad_heads(w, heads, width):
    r = w.shape[0]
    w = w.reshape(r, heads, width)
    return jnp.pad(w, ((0, 0), (0, 0), (0, LANES - width))).reshape(r, heads * LANES)


def _rope_partner_cols(w, heads):
    r = w.shape[0]
    w = w.reshape(r, heads, MLA_QK)
    rope = _swap_pairs(w[:, :, MLA_NOPE:], MLA_ROPE)
    return jnp.concatenate([jnp.zeros_like(w[:, :, :MLA_NOPE]), rope], axis=2).reshape(r, heads * MLA_QK)


def _odd_proj_call(x, seq, gain, w_in, cq_norm, ckv_norm, w_uq, w_ukv, mq_norm, mk_norm,
                   gq_norm, gk_norm, mla_tables, axial_tables):
    t = x.shape[0]
    tm = PROJ_TM
    row = lambda i: (i, 0)
    pos = lambda i: (i % (seq // tm), 0)
    o1 = MLA_Q_RANK
    o2 = o1 + MLA_KV_RANK
    o3 = o2 + MLA_ROPE
    o4 = o3 + GQA_Q_HEADS * GQA_DIM
    o5 = o4 + GQA_KV_HEADS * GQA_DIM
    place_rope = lambda c: jnp.pad(c, ((0, 0), (MLA_NOPE, LANES - MLA_QK)))
    kpe_cols = w_in[:, o2:o3]
    gq_cols = w_in[:, o3:o4].reshape(D_MODEL, GQA_KV_HEADS, GQA_GROUP, GQA_DIM)
    gq_cols = gq_cols.transpose(0, 2, 1, 3).reshape(D_MODEL, GQA_Q_HEADS * GQA_DIM)
    gk_cols = w_in[:, o4:o5]
    half = GQA_DIM // 2
    win = jnp.concatenate(
        [w_in[:, :o2], place_rope(kpe_cols), place_rope(_swap_pairs(kpe_cols, MLA_ROPE)),
         gq_cols, _swap_pairs(gq_cols, half), gk_cols, _swap_pairs(gk_cols, half), w_in[:, o5:]],
        axis=1).astype(BF16)
    wuq = jnp.concatenate([_pad_heads(w_uq, MLA_HEADS, MLA_QK),
                           _pad_heads(_rope_partner_cols(w_uq, MLA_HEADS), MLA_HEADS, MLA_QK)],
                          axis=1).astype(BF16)
    w_ukv = w_ukv.reshape(MLA_KV_RANK, MLA_HEADS, MLA_NOPE + MLA_V)
    wuk = _pad_heads(w_ukv[:, :, :MLA_NOPE].reshape(MLA_KV_RANK, -1), MLA_HEADS, MLA_NOPE).astype(BF16)
    wuv = w_ukv[:, :, MLA_NOPE:].reshape(MLA_KV_RANK, MLA_HEADS * MLA_V).astype(BF16)

    def mla_gain_pair(g):
        partner = jnp.concatenate([g[:MLA_NOPE], _swap_pairs(g[MLA_NOPE:], MLA_ROPE)])
        return jnp.pad(jnp.stack([g, partner]), ((0, 0), (0, LANES - MLA_QK)))

    args = [x, gain.reshape(1, D_MODEL), win, cq_norm.reshape(1, -1), ckv_norm.reshape(1, -1),
            wuq, wuk, wuv, mla_gain_pair(mq_norm), mla_gain_pair(mk_norm),
            _gain_pair(gq_norm, half), _gain_pair(gk_norm, half), *mla_tables, *axial_tables]
    in_specs = ([pl.BlockSpec((tm, D_MODEL), row)]
                + [_const_spec(a.shape) for a in args[1:12]]
                + [pl.BlockSpec((tm, LANES), pos)] * 4)
    widths = [MLA_HEADS * LANES, MLA_HEADS * LANES, MLA_HEADS * MLA_V,
              GQA_Q_HEADS * GQA_DIM, LANES, LANES]
    return pl.pallas_call(
        _odd_proj_kernel, grid=(t // tm,), in_specs=in_specs,
        out_specs=[pl.BlockSpec((tm, n), row) for n in widths],
        out_shape=[jax.ShapeDtypeStruct((t, n), BF16) for n in widths],
        compiler_params=_params(("parallel",)), name="odd_proj",
    )(*args)


def _attention_kernel(*refs, mode, lam_init, stabilised):
    if mode == "diff":
        q_ref, k_ref, v_ref, lam_ref, sub_ref, o_ref = refs[:6]
    else:
        q_ref, k_ref, v_ref, o_ref = refs[:4]
    qt_ref, tile_buf, stat_ref, acc_ref = refs[-4:]
    tq = q_ref.shape[0]
    seq = k_ref.shape[0]
    tk = ATT_TK
    cw = ATT_CW
    n = seq // tk
    q = q_ref[...].astype(F32)
    if mode == "mla":
        qt_ref[:, :tq] = q[:, :LANES].T.astype(BF16)
        qt_ref[:, tq:] = q[:, LANES:].T.astype(BF16)
    else:
        qt = q.T
        first = lax.broadcasted_iota(jnp.int32, qt.shape, 0) < (LANES // 2)
        qt_ref[:, :tq] = jnp.where(first, qt, 0.0).astype(BF16)
        qt_ref[:, tq:] = jnp.where(first, 0.0, qt).astype(BF16)

    chunks = [slice(c * cw, (c + 1) * cw) for c in range(2 * tq // cw)]

    def tile_rows(t):
        return pl.ds(pl.multiple_of(t * tk, tk), tk)

    def ahead(kt, slot, cols):
        kc = kt
        if mode == "mla":
            kc = kt[:, :LANES] if cols.start < tq else kt[:, LANES:]
        s = _dot(kc, qt_ref[:, cols])
        if stabilised:
            tile_buf[slot, :, cols] = s
        else:
            p = jnp.exp2(s)
            stat_ref[:, cols] += jnp.sum(p.reshape(tk // 8, 8, p.shape[1]), axis=0)
            tile_buf[slot, :, cols] = p.astype(BF16)

    def consume(vt, slot, cols):
        if stabilised:
            s = tile_buf[slot, :, cols]
            m_old = stat_ref[:, cols]
            m_new = jnp.maximum(m_old, jnp.max(s, axis=0, keepdims=True))
            alpha = jnp.exp2(m_old - m_new)
            p = jnp.exp2(s - m_new).astype(BF16)
            stat_ref[:, cols] = m_new
            acc_ref[:, cols] = alpha * acc_ref[:, cols] + _dot(vt, p)
        else:
            acc_ref[:, cols] += _dot(vt, tile_buf[slot, :, cols])

    def value_rows(t):
        vt = v_ref[tile_rows(t), :].T
        parts = (vt, vt) if mode == "diff" else (vt[:LANES // 2], vt[LANES // 2:])
        if not stabilised:
            return parts
        ones = jnp.ones((SUM_ROWS, tk), BF16)
        return tuple(jnp.concatenate([part, ones], axis=0) for part in parts)

    def half_step(t, t_next, cur, nxt):
        kt = k_ref[tile_rows(t_next), :]
        vt = value_rows(t)
        for cols in chunks:
            ahead(kt, nxt, cols)
            consume(vt[0] if cols.start < tq else vt[1], cur, cols)

    stat_ref[...] = jnp.full(stat_ref.shape, -jnp.inf if stabilised else 0.0, F32)
    acc_ref[...] = jnp.zeros(acc_ref.shape, F32)
    kt0 = k_ref[tile_rows(0), :]
    for cols in chunks:
        ahead(kt0, 0, cols)

    def body(i, carry):
        for u in range(ATT_PAIRS):
            t = 2 * (i * ATT_PAIRS + u)
            half_step(t, t + 1, 0, 1)
            half_step(t + 1, t + 2, 1, 0)
        return carry

    lax.fori_loop(0, (n - 2) // (2 * ATT_PAIRS), body, 0)
    half_step(n - 2, n - 1, 0, 1)
    vt_last = value_rows(n - 1)
    for cols in chunks:
        consume(vt_last[0] if cols.start < tq else vt_last[1], 1, cols)

    if stabilised:
        nv = acc_ref.shape[0] - SUM_ROWS
        o = acc_ref[:nv, :] / acc_ref[nv:nv + 1, :]
    else:
        o = acc_ref[...] / jnp.sum(stat_ref[...], axis=0, keepdims=True)
    if mode == "diff":
        lv = lam_ref[...]
        lam = (jnp.exp(jnp.sum(lv[0:1] * lv[1:2], axis=-1, keepdims=True))
               - jnp.exp(jnp.sum(lv[2:3] * lv[3:4], axis=-1, keepdims=True)) + lam_init)
        ot = o[:, :tq] - lam * o[:, tq:]
        ms = jnp.mean(ot * ot, axis=0, keepdims=True)
        ot = ot * lax.rsqrt(ms + EPS) * (sub_ref[...] * (1.0 - lam_init))
        o_ref[...] = ot.T.astype(o_ref.dtype)
    else:
        ot = jnp.concatenate([o[:, :tq], o[:, tq:]], axis=0).T
        o_ref[...] = ot.astype(o_ref.dtype)


def _attention_call(q, k, v, mode, score_bound, extra=(), lam_init=0.0):
    b, seq, _ = q.shape
    qw = 2 * LANES if mode == "mla" else LANES
    nblk = q.shape[2] // qw
    tq, tk = ATT_TQ, ATT_TK
    assert seq % tk == 0 and (seq // tk - 2) % (2 * ATT_PAIRS) == 0 and seq % tq == 0 and tq % ATT_CW == 0
    shared = mode == "gqa"
    value_rows = LANES if mode == "diff" else LANES // 2
    kv_idx = (lambda bi, j, i: (bi, 0, 0)) if shared else (lambda bi, j, i: (bi, 0, j))
    in_specs = [pl.BlockSpec((None, tq, qw), lambda bi, j, i: (bi, i, j)),
                pl.BlockSpec((None, seq, qw), kv_idx),
                pl.BlockSpec((None, seq, LANES), kv_idx)]
    in_specs += [_const_spec(e.shape) for e in extra]

    def call(stabilised):
        if stabilised:
            scratch = [pltpu.VMEM((2, tk, 2 * tq), F32), pltpu.VMEM((1, 2 * tq), F32),
                       pltpu.VMEM((value_rows + SUM_ROWS, 2 * tq), F32)]
        else:
            scratch = [pltpu.VMEM((2, tk, 2 * tq), BF16), pltpu.VMEM((8, 2 * tq), F32),
                       pltpu.VMEM((value_rows, 2 * tq), F32)]
        return pl.pallas_call(
            functools.partial(_attention_kernel, mode=mode, lam_init=lam_init, stabilised=stabilised),
            grid=(b, nblk, seq // tq), in_specs=in_specs,
            out_specs=pl.BlockSpec((None, tq, LANES), lambda bi, j, i: (bi, i, j)),
            out_shape=jax.ShapeDtypeStruct((b, seq, nblk * LANES), BF16),
            scratch_shapes=[pltpu.VMEM((LANES, 2 * tq), BF16)] + scratch,
            compiler_params=_params(("parallel", "parallel", "arbitrary")),
            name="attn_" + mode + ("_stab" if stabilised else ""),
        )

    return lax.cond(score_bound <= SAFE_SCORE_RANGE, call(False), call(True), q, k, v, *extra)


def _score_bound(q_gain, k_gain, dim):
    return (jnp.max(jnp.abs(q_gain)) * jnp.max(jnp.abs(k_gain))
            * (dim * dim ** -0.5 * LOG2E * ROUNDING_SLACK))


def _rope_tables(seq):
    assert MLA_ROPE == GQA_DIM // 2 and seq % GRID_W == 0
    pos = jnp.arange(seq, dtype=jnp.int32)
    zeros = lambda n: jnp.zeros((seq, n), F32)

    def angles(dim):
        inv = ROPE_THETA ** (-jnp.arange(0, dim, 2, dtype=F32) / dim)
        ang = pos.astype(F32)[:, None] * inv[None, :]
        return lax.optimization_barrier((jnp.cos(ang), jnp.sin(ang)))

    c, s = angles(DIFF_D)
    full = (jnp.tile(jnp.concatenate([c, c], 1), (1, 2)), jnp.tile(jnp.concatenate([-s, s], 1), (1, 2)))
    c, s = angles(MLA_ROPE)
    tail = zeros(LANES - MLA_QK)
    mla = (jnp.concatenate([jnp.ones((seq, MLA_NOPE), F32), c, c, tail], 1),
           jnp.concatenate([zeros(MLA_NOPE), -s, s, tail], 1))
    rows = seq // GRID_W
    cr, sr = jnp.repeat(c[:rows], GRID_W, axis=0), jnp.repeat(s[:rows], GRID_W, axis=0)
    cc, sc = jnp.tile(c[:GRID_W], (rows, 1)), jnp.tile(s[:GRID_W], (rows, 1))
    axial = (jnp.tile(jnp.concatenate([cr, cr, cc, cc], 1), (1, 2)),
             jnp.tile(jnp.concatenate([-sr, sr, -sc, sc], 1), (1, 2)))
    return full, mla, axial


def kernel(x, ffn1_norm, ffn1_w_gu, ffn1_w_down, ffn2_norm, ffn2_w_gu, ffn2_w_down, ev_norm, ev_w_in, ev_sgu_norm, ev_w_s, ev_b_s, ev_q_norm, ev_k_norm, ev_lam_q1, ev_lam_k1, ev_lam_q2, ev_lam_k2, ev_sub_norm, ev_w_out, od_norm, od_w_in, od_cq_norm, od_ckv_norm, od_w_uq, od_w_ukv, od_mla_q_norm, od_mla_k_norm, od_gqa_q_norm, od_gqa_k_norm, od_w_out):
    b, seq, d = x.shape
    t = b * seq
    full_tab, mla_tab, axial_tab = _rope_tables(seq)
    x2 = x.reshape(t, d)
    ffn1 = (ffn1_norm[:, None, :], ffn1_w_gu, ffn1_w_down)
    ffn2 = (ffn2_norm[:, None, :], ffn2_w_gu, ffn2_w_down)

    x2 = _ffn_call(x2, 0, *ffn1)
    out_a, q, k, v = _even_proj_call(x2, seq, ev_norm[0], ev_w_in[0], ev_sgu_norm[0], ev_w_s[0],
                                     ev_b_s[0], ev_q_norm[0], ev_k_norm[0], full_tab)
    lam_init = 0.8 - 0.6 * float(np.exp(-0.3 * 0))
    lam_vecs = jnp.stack([ev_lam_q1[0], ev_lam_k1[0], ev_lam_q2[0], ev_lam_k2[0]])
    shp = (b, seq, GMLP_WIDTH)
    out_b = _attention_call(q.reshape(shp), k.reshape(shp), v.reshape(shp), "diff",
                            _score_bound(ev_q_norm[0], ev_k_norm[0], DIFF_D), extra=(lam_vecs, ev_sub_norm[0].reshape(DIFF_V, 1)), lam_init=lam_init)
    w_out = ev_w_out[0].astype(BF16)
    x2 = _ffn_call(x2, 0, *ffn2,
                   mix=(out_a, out_b.reshape(t, -1), w_out[:GMLP_WIDTH], w_out[GMLP_WIDTH:]))

    x2 = _ffn_call(x2, 1, *ffn1)
    qc, kc, vc, qd, kd, vd = _odd_proj_call(
        x2, seq, od_norm[0], od_w_in[0], od_cq_norm[0], od_ckv_norm[0], od_w_uq[0], od_w_ukv[0],
        od_mla_q_norm[0], od_mla_k_norm[0], od_gqa_q_norm[0], od_gqa_k_norm[0], mla_tab, axial_tab)
    r3 = lambda a: a.reshape(b, seq, a.shape[1])
    out_c = _attention_call(r3(qc), r3(kc), r3(vc), "mla",
                            _score_bound(od_mla_q_norm[0], od_mla_k_norm[0], MLA_QK))
    out_d = _attention_call(r3(qd), r3(kd), r3(vd), "gqa",
                            _score_bound(od_gqa_q_norm[0], od_gqa_k_norm[0], GQA_DIM))
    w_out = od_w_out[0].astype(BF16)
    n_c = MLA_HEADS * MLA_V
    w_d = w_out[n_c:].reshape(GQA_KV_HEADS, GQA_GROUP, GQA_DIM, d).transpose(1, 0, 2, 3)
    x2 = _ffn_call(x2, 1, *ffn2,
                   mix=(out_c.reshape(t, -1), out_d.reshape(t, -1), w_out[:n_c],
                        w_d.reshape(GQA_Q_HEADS * GQA_DIM, d)))
    return x2.reshape(b, seq, d)
```

```python
import functools
import math

import numpy as np
import jax
import jax.numpy as jnp
from jax import lax
from jax.experimental import pallas as pl
from jax.experimental.pallas import tpu as pltpu

D_MODEL = 1024
D_FF = 2816
ROPE_THETA = 10000.0
GRID_W = 64
EPS = 1e-6
GMLP_GROUPS = 8
GMLP_GROUP_DIM = 64
GMLP_CHUNK = 128
GMLP_WIDTH = GMLP_GROUPS * GMLP_GROUP_DIM
DIFF_D = 64
DIFF_V = 128
MLA_HEADS = 8
MLA_Q_RANK = 256
MLA_KV_RANK = 128
MLA_NOPE = 64
MLA_ROPE = 32
MLA_V = 64
MLA_QK = MLA_NOPE + MLA_ROPE
GQA_Q_HEADS = 8
GQA_KV_HEADS = 2
GQA_GROUP = GQA_Q_HEADS // GQA_KV_HEADS
GQA_DIM = 64

LANES = 128
VMEM_LIMIT_BYTES = 56 * 1024 * 1024

FFN_TM = 1024
FFN_SUB = 256
FFN_CAST_CHUNKS = 11
EVEN_PROJ_TM = 512
ODD_PROJ_TM = 1024
ATT_TQ = 2048
ATT_TK = 256
ATT_CW = 512
ATT_PAIRS = 3
SUM_ROWS = 16
LOG2E = math.log2(math.e)
SAFE_SCORE_RANGE = 60.0
ROUNDING_SLACK = 1.0 + 2.0 ** -6

BF16 = jnp.bfloat16
F32 = jnp.float32


def _params(semantics):
    return pltpu.CompilerParams(dimension_semantics=semantics,
                                vmem_limit_bytes=VMEM_LIMIT_BYTES)


def _const_spec(shape):
    nd = len(shape)
    return pl.BlockSpec(shape, lambda *_: (0,) * nd, pipeline_mode=pl.Buffered(1))


def _rms_rows(x, gain):
    ms = jnp.mean(x * x, axis=-1, keepdims=True)
    return x * lax.rsqrt(ms + EPS) * gain


def _dot(a, b):
    return jnp.dot(a, b, preferred_element_type=F32)


def _lane_is_low(shape):
    return lax.broadcasted_iota(jnp.int32, shape, len(shape) - 1) < (LANES // 2)


def _inv_rms_half_blocks(x):
    low = _lane_is_low(x.shape)
    x2 = x * x
    x2_lo = jnp.where(low, x2, 0.0)
    x2_hi = x2 - x2_lo
    ms_lo = jnp.sum(x2_lo, axis=-1, keepdims=True) * (2.0 / LANES)
    ms_hi = jnp.sum(x2_hi, axis=-1, keepdims=True) * (2.0 / LANES)
    return jnp.where(low, lax.rsqrt(ms_lo + EPS), lax.rsqrt(ms_hi + EPS))


def _inv_rms_padded_block(x, width):
    ms = jnp.sum(x * x, axis=-1, keepdims=True) * (1.0 / width)
    return lax.rsqrt(ms + EPS)


def _swap_pairs(a, group):
    shape = a.shape
    a = a.reshape(shape[:-1] + (shape[-1] // group, 2, group // 2))
    return jnp.flip(a, axis=-2).reshape(shape)


def _ffn_body(x, g_ref, wgu_ref, wd_ref, o_ref):
    for r in range(x.shape[0] // FFN_SUB):
        rows = slice(r * FFN_SUB, (r + 1) * FFN_SUB)
        xr = x[rows]
        xn = _rms_rows(xr, g_ref[...]).astype(BF16)
        h = _dot(xn, wgu_ref[...])
        gate = h[:, :D_FF]
        up = h[:, D_FF:]
        act = (gate / (1.0 + jnp.exp(-gate)) * up).astype(BF16)
        o_ref[rows, :] = xr + 0.5 * _dot(act, wd_ref[...])


def _load_ffn_weights(layer, wgu_hbm, wd_hbm, wgu_ref, wd_ref, gu_stage, d_stage, sem):
    gu_cols = 2 * D_FF // FFN_CAST_CHUNKS
    d_rows = D_FF // FFN_CAST_CHUNKS

    def gu_copy(c, slot):
        return pltpu.make_async_copy(wgu_hbm.at[layer, :, pl.ds(c * gu_cols, gu_cols)],
                                     gu_stage.at[slot], sem.at[0, slot])

    def d_copy(c, slot):
        return pltpu.make_async_copy(wd_hbm.at[layer, pl.ds(c * d_rows, d_rows), :],
                                     d_stage.at[slot], sem.at[1, slot])

    gu_copy(0, 0).start()
    d_copy(0, 0).start()
    for c in range(FFN_CAST_CHUNKS):
        slot = c % 2
        if c + 1 < FFN_CAST_CHUNKS:
            gu_copy(c + 1, 1 - slot).start()
            d_copy(c + 1, 1 - slot).start()
        gu_copy(c, slot).wait()
        wgu_ref[:, c * gu_cols:(c + 1) * gu_cols] = gu_stage[slot].astype(BF16)
        d_copy(c, slot).wait()
        wd_ref[c * d_rows:(c + 1) * d_rows, :] = d_stage[slot].astype(BF16)


def _ffn_kernel(x_ref, g_ref, wgu_hbm, wd_hbm, o_ref, wgu_ref, wd_ref, gu_stage, d_stage, sem, *, layer):
    @pl.when(pl.program_id(0) == 0)
    def _():
        _load_ffn_weights(layer, wgu_hbm, wd_hbm, wgu_ref, wd_ref, gu_stage, d_stage, sem)

    _ffn_body(x_ref[...], g_ref, wgu_ref, wd_ref, o_ref)


def _mix_ffn_kernel(x_ref, a_ref, b_ref, wa_ref, wb_ref, g_ref, wgu_hbm, wd_hbm, o_ref,
                    wgu_ref, wd_ref, gu_stage, d_stage, sem, *, layer):
    @pl.when(pl.program_id(0) == 0)
    def _():
        _load_ffn_weights(layer, wgu_hbm, wd_hbm, wgu_ref, wd_ref, gu_stage, d_stage, sem)

    x = x_ref[...] + _dot(a_ref[...], wa_ref[...]) + _dot(b_ref[...], wb_ref[...])
    _ffn_body(x, g_ref, wgu_ref, wd_ref, o_ref)


def _layer_spec(shape, layer):
    nd = len(shape) - 1
    return pl.BlockSpec((None,) + tuple(shape[1:]), lambda *_: (layer,) + (0,) * nd,
                        pipeline_mode=pl.Buffered(1))


def _ffn_call(x, layer, gains, w_gu, w_down, mix=None):
    t = x.shape[0]
    tm = FFN_TM
    row = lambda i: (i, 0)
    x_spec = pl.BlockSpec((tm, D_MODEL), row)
    hbm = pl.BlockSpec(memory_space=pl.ANY)
    w_args = [gains, w_gu, w_down]
    w_specs = [_layer_spec(gains.shape, layer), hbm, hbm]
    if mix is None:
        kern, in_specs, args = _ffn_kernel, [x_spec] + w_specs, [x] + w_args
    else:
        a, b, wa, wb = mix
        kern = _mix_ffn_kernel
        in_specs = [x_spec, pl.BlockSpec((tm, a.shape[1]), row), pl.BlockSpec((tm, b.shape[1]), row),
                    _const_spec(wa.shape), _const_spec(wb.shape)] + w_specs
        args = [x, a, b, wa, wb] + w_args
    return pl.pallas_call(
        functools.partial(kern, layer=layer), grid=(t // tm,), in_specs=in_specs,
        out_specs=pl.BlockSpec((tm, D_MODEL), row),
        out_shape=jax.ShapeDtypeStruct((t, D_MODEL), F32),
        scratch_shapes=[pltpu.VMEM((D_MODEL, 2 * D_FF), BF16), pltpu.VMEM((D_FF, D_MODEL), BF16),
                        pltpu.VMEM((2, D_MODEL, 2 * D_FF // FFN_CAST_CHUNKS), F32),
                        pltpu.VMEM((2, D_FF // FFN_CAST_CHUNKS, D_MODEL), F32),
                        pltpu.SemaphoreType.DMA((2, 2))],
        compiler_params=_params(("arbitrary",)),
        name="ffn" if mix is None else "mix_ffn",
    )(*args)


def _gelu_tanh(x):
    c = math.sqrt(2.0 / math.pi)
    return 0.5 * x * (1.0 + jnp.tanh(c * (x + 0.044715 * (x * x * x))))


def _even_proj_kernel(x_ref, g_ref, win_ref, sgu_ref, wpair_ref, bias_ref, qg_ref, kg_ref, perm_ref,
                      cos_ref, sin_ref, oa_ref, q_ref, k_ref, v_ref):
    xn = _rms_rows(x_ref[...], g_ref[...]).astype(BF16)
    proj = _dot(xn, win_ref[...])
    w = GMLP_WIDTH
    nblk = w // LANES
    tm = proj.shape[0]
    for j in range(nblk):
        u = _gelu_tanh(proj[:, j * LANES:(j + 1) * LANES])
        v = _gelu_tanh(proj[:, w + j * LANES:w + (j + 1) * LANES])
        vn = v * _inv_rms_half_blocks(v) * sgu_ref[:, j * LANES:(j + 1) * LANES]
        low = _lane_is_low(vn.shape)
        vn_lo = jnp.where(low, vn, 0.0).astype(BF16)
        vn_hi = jnp.where(low, 0.0, vn).astype(BF16)
        wp = wpair_ref[j]
        bias = bias_ref[:, j * LANES:(j + 1) * LANES]
        for c in range(tm // GMLP_CHUNK):
            rows = slice(c * GMLP_CHUNK, (c + 1) * GMLP_CHUNK)
            stacked = jnp.concatenate([vn_lo[rows], vn_hi[rows]], axis=0)
            mixed = _dot(wp, stacked) + bias
            oa_ref[rows, j * LANES:(j + 1) * LANES] = (u[rows] * mixed).astype(BF16)
    cos, sin = cos_ref[...], sin_ref[...]
    q_cos, q_sin = qg_ref[0:1] * cos, qg_ref[1:2] * sin
    k_cos, k_sin = kg_ref[0:1] * cos, kg_ref[1:2] * sin
    pw = perm_ref.shape[0]
    partner = jnp.concatenate(
        [_dot(proj[:, 2 * w + c * pw:2 * w + (c + 1) * pw].astype(BF16), perm_ref[...])
         for c in range(2 * w // pw)], axis=1)
    for j in range(nblk):
        blk = slice(j * LANES, (j + 1) * LANES)
        qb = proj[:, 2 * w + j * LANES:2 * w + (j + 1) * LANES]
        kb = proj[:, 3 * w + j * LANES:3 * w + (j + 1) * LANES]
        qp = partner[:, j * LANES:(j + 1) * LANES]
        kp = partner[:, w + j * LANES:w + (j + 1) * LANES]
        qr = _inv_rms_half_blocks(qb) * (LOG2E * DIFF_D ** -0.5)
        q_ref[:, blk] = ((qb * q_cos + qp * q_sin) * qr).astype(BF16)
        k_ref[:, blk] = ((kb * k_cos + kp * k_sin) * _inv_rms_half_blocks(kb)).astype(BF16)
    v_ref[...] = proj[:, 4 * w:].astype(BF16)


def _gain_pair(g, group):
    return jnp.stack([jnp.tile(g, 2), jnp.tile(_swap_pairs(g, group), 2)])


def _even_proj_call(x, seq, gain, w_in, sgu_norm, w_s, b_s, q_norm, k_norm, tables):
    t = x.shape[0]
    tm = EVEN_PROJ_TM
    w = GMLP_WIDTH
    row = lambda i: (i, 0)
    pos = lambda i: (i % (seq // tm), 0)
    wpair = jnp.concatenate([w_s[0::2], w_s[1::2]], axis=2).astype(BF16)
    bias = jnp.repeat(b_s.T, GMLP_GROUP_DIM, axis=1)
    perm = _swap_pairs(jnp.eye(2 * LANES, dtype=F32), DIFF_D).astype(BF16)
    args = [x, gain.reshape(1, D_MODEL), w_in.astype(BF16), sgu_norm.reshape(1, w), wpair, bias,
            _gain_pair(q_norm, DIFF_D), _gain_pair(k_norm, DIFF_D), perm, *tables]
    in_specs = [pl.BlockSpec((tm, D_MODEL), row)] + [_const_spec(a.shape) for a in args[1:9]]
    in_specs += [pl.BlockSpec((tm, LANES), pos)] * 2
    out = jax.ShapeDtypeStruct((t, w), BF16)
    return pl.pallas_call(
        _even_proj_kernel, grid=(t // tm,), in_specs=in_specs,
        out_specs=[pl.BlockSpec((tm, w), row)] * 4, out_shape=[out] * 4,
        compiler_params=_params(("parallel",)), name="even_proj",
    )(*args)


def _odd_proj_kernel(x_ref, g_ref, win_ref, cqg_ref, ckvg_ref, wuq_ref, wuk_ref, wuv_ref,
                     mqg_ref, mkg_ref, gqg_ref, gkg_ref, mcos_ref, msin_ref, acos_ref, asin_ref,
                     qc_ref, kc_ref, vc_ref, qd_ref, kd_ref, vd_ref):
    xn = _rms_rows(x_ref[...], g_ref[...]).astype(BF16)
    proj = _dot(xn, win_ref[...])
    nq = GQA_Q_HEADS * GQA_DIM
    o1 = MLA_Q_RANK
    o2 = o1 + MLA_KV_RANK
    o3 = o2 + 2 * LANES
    o4 = o3 + 2 * nq
    o5 = o4 + 2 * LANES
    mcos, msin = mcos_ref[...], msin_ref[...]
    q_cos, q_sin = mqg_ref[0:1] * mcos, mqg_ref[1:2] * msin
    k_cos, k_sin = mkg_ref[0:1] * mcos, mkg_ref[1:2] * msin
    cq = _rms_rows(proj[:, :o1], cqg_ref[...]).astype(BF16)
    q_all = _dot(cq, wuq_ref[...])
    ckv = _rms_rows(proj[:, o1:o2], ckvg_ref[...]).astype(BF16)
    kn_all = _dot(ckv, wuk_ref[...])
    vc_ref[...] = _dot(ckv, wuv_ref[...]).astype(BF16)
    kpe = proj[:, o2:o2 + LANES]
    kpe_sin = proj[:, o2 + LANES:o3] * k_sin
    nh = MLA_HEADS * LANES
    for h in range(MLA_HEADS):
        blk = slice(h * LANES, (h + 1) * LANES)
        qb = q_all[:, blk]
        qp = q_all[:, nh + h * LANES:nh + (h + 1) * LANES]
        qr = _inv_rms_padded_block(qb, MLA_QK) * (LOG2E * MLA_QK ** -0.5)
        qc_ref[:, blk] = ((qb * q_cos + qp * q_sin) * qr).astype(BF16)
        kb = kn_all[:, blk] + kpe
        kc_ref[:, blk] = ((kb * k_cos + kpe_sin) * _inv_rms_padded_block(kb, MLA_QK)).astype(BF16)
    acos, asin = acos_ref[...], asin_ref[...]
    q_cos, q_sin = gqg_ref[0:1] * acos, gqg_ref[1:2] * asin
    for j in range(nq // LANES):
        blk = slice(j * LANES, (j + 1) * LANES)
        qb = proj[:, o3 + j * LANES:o3 + (j + 1) * LANES]
        qp = proj[:, o3 + nq + j * LANES:o3 + nq + (j + 1) * LANES]
        qr = _inv_rms_half_blocks(qb) * (LOG2E * GQA_DIM ** -0.5)
        qd_ref[:, blk] = ((qb * q_cos + qp * q_sin) * qr).astype(BF16)
    kb = proj[:, o4:o4 + LANES]
    kp = proj[:, o4 + LANES:o5]
    kd_ref[...] = ((kb * (gkg_ref[0:1] * acos) + kp * (gkg_ref[1:2] * asin))
                   * _inv_rms_half_blocks(kb)).astype(BF16)
    vd_ref[...] = proj[:, o5:].astype(BF16)


def _pad_heads(w, heads, width):
    r = w.shape[0]
    w = w.reshape(r, heads, width)
    return jnp.pad(w, ((0, 0), (0, 0), (0, LANES - width))).reshape(r, heads * LANES)


def _rope_partner_cols(w, heads):
    r = w.shape[0]
    w = w.reshape(r, heads, MLA_QK)
    rope = _swap_pairs(w[:, :, MLA_NOPE:], MLA_ROPE)
    return jnp.concatenate([jnp.zeros_like(w[:, :, :MLA_NOPE]), rope], axis=2).reshape(r, heads * MLA_QK)


def _odd_proj_call(x, seq, gain, w_in, cq_norm, ckv_norm, w_uq, w_ukv, mq_norm, mk_norm,
                   gq_norm, gk_norm, mla_tables, axial_tables):
    t = x.shape[0]
    tm = ODD_PROJ_TM
    row = lambda i: (i, 0)
    pos = lambda i: (i % (seq // tm), 0)
    o1 = MLA_Q_RANK
    o2 = o1 + MLA_KV_RANK
    o3 = o2 + MLA_ROPE
    o4 = o3 + GQA_Q_HEADS * GQA_DIM
    o5 = o4 + GQA_KV_HEADS * GQA_DIM
    place_rope = lambda c: jnp.pad(c, ((0, 0), (MLA_NOPE, LANES - MLA_QK)))
    kpe_cols = w_in[:, o2:o3]
    gq_cols = w_in[:, o3:o4].reshape(D_MODEL, GQA_KV_HEADS, GQA_GROUP, GQA_DIM)
    gq_cols = gq_cols.transpose(0, 2, 1, 3).reshape(D_MODEL, GQA_Q_HEADS * GQA_DIM)
    gk_cols = w_in[:, o4:o5]
    half = GQA_DIM // 2
    win = jnp.concatenate(
        [w_in[:, :o2], place_rope(kpe_cols), place_rope(_swap_pairs(kpe_cols, MLA_ROPE)),
         gq_cols, _swap_pairs(gq_cols, half), gk_cols, _swap_pairs(gk_cols, half), w_in[:, o5:]],
        axis=1).astype(BF16)
    wuq = jnp.concatenate([_pad_heads(w_uq, MLA_HEADS, MLA_QK),
                           _pad_heads(_rope_partner_cols(w_uq, MLA_HEADS), MLA_HEADS, MLA_QK)],
                          axis=1).astype(BF16)
    w_ukv = w_ukv.reshape(MLA_KV_RANK, MLA_HEADS, MLA_NOPE + MLA_V)
    wuk = _pad_heads(w_ukv[:, :, :MLA_NOPE].reshape(MLA_KV_RANK, -1), MLA_HEADS, MLA_NOPE).astype(BF16)
    wuv = w_ukv[:, :, MLA_NOPE:].reshape(MLA_KV_RANK, MLA_HEADS * MLA_V).astype(BF16)

    def mla_gain_pair(g):
        partner = jnp.concatenate([g[:MLA_NOPE], _swap_pairs(g[MLA_NOPE:], MLA_ROPE)])
        return jnp.pad(jnp.stack([g, partner]), ((0, 0), (0, LANES - MLA_QK)))

    args = [x, gain.reshape(1, D_MODEL), win, cq_norm.reshape(1, -1), ckv_norm.reshape(1, -1),
            wuq, wuk, wuv, mla_gain_pair(mq_norm), mla_gain_pair(mk_norm),
            _gain_pair(gq_norm, half), _gain_pair(gk_norm, half), *mla_tables, *axial_tables]
    in_specs = ([pl.BlockSpec((tm, D_MODEL), row)]
                + [_const_spec(a.shape) for a in args[1:12]]
                + [pl.BlockSpec((tm, LANES), pos)] * 4)
    widths = [MLA_HEADS * LANES, MLA_HEADS * LANES, MLA_HEADS * MLA_V,
              GQA_Q_HEADS * GQA_DIM, LANES, LANES]
    return pl.pallas_call(
        _odd_proj_kernel, grid=(t // tm,), in_specs=in_specs,
        out_specs=[pl.BlockSpec((tm, n), row) for n in widths],
        out_shape=[jax.ShapeDtypeStruct((t, n), BF16) for n in widths],
        compiler_params=_params(("parallel",)), name="odd_proj",
    )(*args)


def _attention_kernel(*refs, mode, lam_init, stabilised):
    if mode == "diff":
        q_ref, k_ref, v_ref, lam_ref, sub_ref, o_ref = refs[:6]
    else:
        q_ref, k_ref, v_ref, o_ref = refs[:4]
    qt_ref, tile_buf, stat_ref, acc_ref = refs[-4:]
    tq = q_ref.shape[0]
    seq = k_ref.shape[0]
    tk = ATT_TK
    cw = ATT_CW
    n = seq // tk
    q = q_ref[...].astype(F32)
    if mode == "mla":
        qt_ref[:, :tq] = q[:, :LANES].T.astype(BF16)
        qt_ref[:, tq:] = q[:, LANES:].T.astype(BF16)
    else:
        qt = q.T
        first = lax.broadcasted_iota(jnp.int32, qt.shape, 0) < (LANES // 2)
        qt_ref[:, :tq] = jnp.where(first, qt, 0.0).astype(BF16)
        qt_ref[:, tq:] = jnp.where(first, 0.0, qt).astype(BF16)

    chunks = [slice(c * cw, (c + 1) * cw) for c in range(2 * tq // cw)]

    def tile_rows(t):
        return pl.ds(pl.multiple_of(t * tk, tk), tk)

    def ahead(kt, slot, cols):
        kc = kt
        if mode == "mla":
            kc = kt[:, :LANES] if cols.start < tq else kt[:, LANES:]
        s = _dot(kc, qt_ref[:, cols])
        if stabilised:
            tile_buf[slot, :, cols] = s
        else:
            p = jnp.exp2(s)
            stat_ref[:, cols] += jnp.sum(p.reshape(tk // 8, 8, p.shape[1]), axis=0)
            tile_buf[slot, :, cols] = p.astype(BF16)

    def consume(vt, slot, cols):
        if stabilised:
            s = tile_buf[slot, :, cols]
            m_old = stat_ref[:, cols]
            m_new = jnp.maximum(m_old, jnp.max(s, axis=0, keepdims=True))
            alpha = jnp.exp2(m_old - m_new)
            p = jnp.exp2(s - m_new).astype(BF16)
            stat_ref[:, cols] = m_new
            acc_ref[:, cols] = alpha * acc_ref[:, cols] + _dot(vt, p)
        else:
            acc_ref[:, cols] += _dot(vt, tile_buf[slot, :, cols])

    def value_rows(t):
        vt = v_ref[tile_rows(t), :].T
        parts = (vt, vt) if mode == "diff" else (vt[:LANES // 2], vt[LANES // 2:])
        if not stabilised:
            return parts
        ones = jnp.ones((SUM_ROWS, tk), BF16)
        return tuple(jnp.concatenate([part, ones], axis=0) for part in parts)

    def half_step(t, t_next, cur, nxt):
        kt = k_ref[tile_rows(t_next), :]
        vt = value_rows(t)
        for cols in chunks:
            ahead(kt, nxt, cols)
            consume(vt[0] if cols.start < tq else vt[1], cur, cols)

    stat_ref[...] = jnp.full(stat_ref.shape, -jnp.inf if stabilised else 0.0, F32)
    acc_ref[...] = jnp.zeros(acc_ref.shape, F32)
    kt0 = k_ref[tile_rows(0), :]
    for cols in chunks:
        ahead(kt0, 0, cols)

    def body(i, carry):
        for u in range(ATT_PAIRS):
            t = 2 * (i * ATT_PAIRS + u)
            half_step(t, t + 1, 0, 1)
            half_step(t + 1, t + 2, 1, 0)
        return carry

    lax.fori_loop(0, (n - 2) // (2 * ATT_PAIRS), body, 0)
    half_step(n - 2, n - 1, 0, 1)
    vt_last = value_rows(n - 1)
    for cols in chunks:
        consume(vt_last[0] if cols.start < tq else vt_last[1], 1, cols)

    if stabilised:
        nv = acc_ref.shape[0] - SUM_ROWS
        o = acc_ref[:nv, :] / acc_ref[nv:nv + 1, :]
    else:
        o = acc_ref[...] / jnp.sum(stat_ref[...], axis=0, keepdims=True)
    if mode == "diff":
        lv = lam_ref[...]
        lam = (jnp.exp(jnp.sum(lv[0:1] * lv[1:2], axis=-1, keepdims=True))
               - jnp.exp(jnp.sum(lv[2:3] * lv[3:4], axis=-1, keepdims=True)) + lam_init)
        ot = o[:, :tq] - lam * o[:, tq:]
        ms = jnp.mean(ot * ot, axis=0, keepdims=True)
        ot = ot * lax.rsqrt(ms + EPS) * (sub_ref[...] * (1.0 - lam_init))
        o_ref[...] = ot.T.astype(o_ref.dtype)
    else:
        ot = jnp.concatenate([o[:, :tq], o[:, tq:]], axis=0).T
        o_ref[...] = ot.astype(o_ref.dtype)


def _attention_call(q, k, v, mode, score_bound, extra=(), lam_init=0.0):
    b, seq, _ = q.shape
    qw = 2 * LANES if mode == "mla" else LANES
    nblk = q.shape[2] // qw
    tq, tk = ATT_TQ, ATT_TK
    assert seq % tk == 0 and (seq // tk - 2) % (2 * ATT_PAIRS) == 0 and seq % tq == 0 and tq % ATT_CW == 0
    shared = mode == "gqa"
    value_rows = LANES if mode == "diff" else LANES // 2
    kv_idx = (lambda bi, j, i: (bi, 0, 0)) if shared else (lambda bi, j, i: (bi, 0, j))
    in_specs = [pl.BlockSpec((None, tq, qw), lambda bi, j, i: (bi, i, j)),
                pl.BlockSpec((None, seq, qw), kv_idx),
                pl.BlockSpec((None, seq, LANES), kv_idx)]
    in_specs += [_const_spec(e.shape) for e in extra]

    def call(stabilised):
        if stabilised:
            scratch = [pltpu.VMEM((2, tk, 2 * tq), F32), pltpu.VMEM((1, 2 * tq), F32),
                       pltpu.VMEM((value_rows + SUM_ROWS, 2 * tq), F32)]
        else:
            scratch = [pltpu.VMEM((2, tk, 2 * tq), BF16), pltpu.VMEM((8, 2 * tq), F32),
                       pltpu.VMEM((value_rows, 2 * tq), F32)]
        return pl.pallas_call(
            functools.partial(_attention_kernel, mode=mode, lam_init=lam_init, stabilised=stabilised),
            grid=(b, nblk, seq // tq), in_specs=in_specs,
            out_specs=pl.BlockSpec((None, tq, LANES), lambda bi, j, i: (bi, i, j)),
            out_shape=jax.ShapeDtypeStruct((b, seq, nblk * LANES), BF16),
            scratch_shapes=[pltpu.VMEM((LANES, 2 * tq), BF16)] + scratch,
            compiler_params=_params(("parallel", "parallel", "arbitrary")),
            name="attn_" + mode + ("_stab" if stabilised else ""),
        )

    return lax.cond(score_bound <= SAFE_SCORE_RANGE, call(False), call(True), q, k, v, *extra)


def _score_bound(q_gain, k_gain, dim):
    return (jnp.max(jnp.abs(q_gain)) * jnp.max(jnp.abs(k_gain))
            * (dim * dim ** -0.5 * LOG2E * ROUNDING_SLACK))


def _rope_tables(seq):
    assert MLA_ROPE == GQA_DIM // 2 and seq % GRID_W == 0
    pos = jnp.arange(seq, dtype=jnp.int32)
    zeros = lambda n: jnp.zeros((seq, n), F32)

    def angles(dim):
        inv = ROPE_THETA ** (-jnp.arange(0, dim, 2, dtype=F32) / dim)
        ang = pos.astype(F32)[:, None] * inv[None, :]
        return lax.optimization_barrier((jnp.cos(ang), jnp.sin(ang)))

    c, s = angles(DIFF_D)
    full = (jnp.tile(jnp.concatenate([c, c], 1), (1, 2)), jnp.tile(jnp.concatenate([-s, s], 1), (1, 2)))
    c, s = angles(MLA_ROPE)
    tail = zeros(LANES - MLA_QK)
    mla = (jnp.concatenate([jnp.ones((seq, MLA_NOPE), F32), c, c, tail], 1),
           jnp.concatenate([zeros(MLA_NOPE), -s, s, tail], 1))
    rows = seq // GRID_W
    cr, sr = jnp.repeat(c[:rows], GRID_W, axis=0), jnp.repeat(s[:rows], GRID_W, axis=0)
    cc, sc = jnp.tile(c[:GRID_W], (rows, 1)), jnp.tile(s[:GRID_W], (rows, 1))
    axial = (jnp.tile(jnp.concatenate([cr, cr, cc, cc], 1), (1, 2)),
             jnp.tile(jnp.concatenate([-sr, sr, -sc, sc], 1), (1, 2)))
    return full, mla, axial


def kernel(x, ffn1_norm, ffn1_w_gu, ffn1_w_down, ffn2_norm, ffn2_w_gu, ffn2_w_down, ev_norm, ev_w_in, ev_sgu_norm, ev_w_s, ev_b_s, ev_q_norm, ev_k_norm, ev_lam_q1, ev_lam_k1, ev_lam_q2, ev_lam_k2, ev_sub_norm, ev_w_out, od_norm, od_w_in, od_cq_norm, od_ckv_norm, od_w_uq, od_w_ukv, od_mla_q_norm, od_mla_k_norm, od_gqa_q_norm, od_gqa_k_norm, od_w_out):
    b, seq, d = x.shape
    t = b * seq
    full_tab, mla_tab, axial_tab = _rope_tables(seq)
    x2 = x.reshape(t, d)
    ffn1 = (ffn1_norm[:, None, :], ffn1_w_gu, ffn1_w_down)
    ffn2 = (ffn2_norm[:, None, :], ffn2_w_gu, ffn2_w_down)

    x2 = _ffn_call(x2, 0, *ffn1)
    out_a, q, k, v = _even_proj_call(x2, seq, ev_norm[0], ev_w_in[0], ev_sgu_norm[0], ev_w_s[0],
                                     ev_b_s[0], ev_q_norm[0], ev_k_norm[0], full_tab)
    lam_init = 0.8 - 0.6 * float(np.exp(-0.3 * 0))
    lam_vecs = jnp.stack([ev_lam_q1[0], ev_lam_k1[0], ev_lam_q2[0], ev_lam_k2[0]])
    shp = (b, seq, GMLP_WIDTH)
    out_b = _attention_call(q.reshape(shp), k.reshape(shp), v.reshape(shp), "diff",
                            _score_bound(ev_q_norm[0], ev_k_norm[0], DIFF_D), extra=(lam_vecs, ev_sub_norm[0].reshape(DIFF_V, 1)), lam_init=lam_init)
    w_out = ev_w_out[0].astype(BF16)
    x2 = _ffn_call(x2, 0, *ffn2,
                   mix=(out_a, out_b.reshape(t, -1), w_out[:GMLP_WIDTH], w_out[GMLP_WIDTH:]))

    x2 = _ffn_call(x2, 1, *ffn1)
    qc, kc, vc, qd, kd, vd = _odd_proj_call(
        x2, seq, od_norm[0], od_w_in[0], od_cq_norm[0], od_ckv_norm[0], od_w_uq[0], od_w_ukv[0],
        od_mla_q_norm[0], od_mla_k_norm[0], od_gqa_q_norm[0], od_gqa_k_norm[0], mla_tab, axial_tab)
    r3 = lambda a: a.reshape(b, seq, a.shape[1])
    out_c = _attention_call(r3(qc), r3(kc), r3(vc), "mla",
                            _score_bound(od_mla_q_norm[0], od_mla_k_norm[0], MLA_QK))
    out_d = _attention_call(r3(qd), r3(kd), r3(vd), "gqa",
                            _score_bound(od_gqa_q_norm[0], od_gqa_k_norm[0], GQA_DIM))
    w_out = od_w_out[0].astype(BF16)
    n_c = MLA_HEADS * MLA_V
    w_d = w_out[n_c:].reshape(GQA_KV_HEADS, GQA_GROUP, GQA_DIM, d).transpose(1, 0, 2, 3)
    x2 = _ffn_call(x2, 1, *ffn2,
                   mix=(out_c.reshape(t, -1), out_d.reshape(t, -1), w_out[:n_c],
                        w_d.reshape(GQA_Q_HEADS * GQA_DIM, d)))
    return x2.reshape(b, seq, d)
```

```python
import functools
import math

import numpy as np
import jax
import jax.numpy as jnp
from jax import lax
from jax.experimental import pallas as pl
from jax.experimental.pallas import tpu as pltpu

D_MODEL = 1024
D_FF = 2816
ROPE_THETA = 10000.0
GRID_W = 64
EPS = 1e-6
GMLP_GROUPS = 8
GMLP_GROUP_DIM = 64
GMLP_CHUNK = 128
GMLP_WIDTH = GMLP_GROUPS * GMLP_GROUP_DIM
DIFF_D = 64
DIFF_V = 128
MLA_HEADS = 8
MLA_Q_RANK = 256
MLA_KV_RANK = 128
MLA_NOPE = 64
MLA_ROPE = 32
MLA_V = 64
MLA_QK = MLA_NOPE + MLA_ROPE
GQA_Q_HEADS = 8
GQA_KV_HEADS = 2
GQA_GROUP = GQA_Q_HEADS // GQA_KV_HEADS
GQA_DIM = 64

LANES = 128
VMEM_LIMIT_BYTES = 56 * 1024 * 1024

FFN_TM = 1024
FFN_SUB = 256
FFN_CAST_CHUNKS = 11
EVEN_PROJ_TM = 512
ODD_PROJ_TM = 1024
ATT_TQ = 2048
ATT_TK = 256
ATT_CW = 512
ATT_PAIRS = 5
SUM_ROWS = 16
LOG2E = math.log2(math.e)
SAFE_SCORE_RANGE = 60.0
ROUNDING_SLACK = 1.0 + 2.0 ** -6

BF16 = jnp.bfloat16
F32 = jnp.float32


def _params(semantics):
    return pltpu.CompilerParams(dimension_semantics=semantics,
                                vmem_limit_bytes=VMEM_LIMIT_BYTES)


def _const_spec(shape):
    nd = len(shape)
    return pl.BlockSpec(shape, lambda *_: (0,) * nd, pipeline_mode=pl.Buffered(1))


def _rms_rows(x, gain):
    ms = jnp.mean(x * x, axis=-1, keepdims=True)
    return x * lax.rsqrt(ms + EPS) * gain


def _dot(a, b):
    return jnp.dot(a, b, preferred_element_type=F32)


def _lane_is_low(shape):
    return lax.broadcasted_iota(jnp.int32, shape, len(shape) - 1) < (LANES // 2)


def _inv_rms_half_blocks(x):
    low = _lane_is_low(x.shape)
    x2 = x * x
    x2_lo = jnp.where(low, x2, 0.0)
    x2_hi = x2 - x2_lo
    ms_lo = jnp.sum(x2_lo, axis=-1, keepdims=True) * (2.0 / LANES)
    ms_hi = jnp.sum(x2_hi, axis=-1, keepdims=True) * (2.0 / LANES)
    return jnp.where(low, lax.rsqrt(ms_lo + EPS), lax.rsqrt(ms_hi + EPS))


def _inv_rms_padded_block(x, width):
    ms = jnp.sum(x * x, axis=-1, keepdims=True) * (1.0 / width)
    return lax.rsqrt(ms + EPS)


def _swap_pairs(a, group):
    shape = a.shape
    a = a.reshape(shape[:-1] + (shape[-1] // group, 2, group // 2))
    return jnp.flip(a, axis=-2).reshape(shape)


def _ffn_body(x, g_ref, wgu_ref, wd_ref, o_ref):
    for r in range(x.shape[0] // FFN_SUB):
        rows = slice(r * FFN_SUB, (r + 1) * FFN_SUB)
        xr = x[rows]
        xn = _rms_rows(xr, g_ref[...]).astype(BF16)
        h = _dot(xn, wgu_ref[...])
        gate = h[:, :D_FF]
        up = h[:, D_FF:]
        act = (gate / (1.0 + jnp.exp(-gate)) * up).astype(BF16)
        o_ref[rows, :] = xr + 0.5 * _dot(act, wd_ref[...])


def _load_ffn_weights(layer, wgu_hbm, wd_hbm, wgu_ref, wd_ref, gu_stage, d_stage, sem):
    gu_cols = 2 * D_FF // FFN_CAST_CHUNKS
    d_rows = D_FF // FFN_CAST_CHUNKS

    def gu_copy(c, slot):
        return pltpu.make_async_copy(wgu_hbm.at[layer, :, pl.ds(c * gu_cols, gu_cols)],
                                     gu_stage.at[slot], sem.at[0, slot])

    def d_copy(c, slot):
        return pltpu.make_async_copy(wd_hbm.at[layer, pl.ds(c * d_rows, d_rows), :],
                                     d_stage.at[slot], sem.at[1, slot])

    gu_copy(0, 0).start()
    d_copy(0, 0).start()
    for c in range(FFN_CAST_CHUNKS):
        slot = c % 2
        if c + 1 < FFN_CAST_CHUNKS:
            gu_copy(c + 1, 1 - slot).start()
            d_copy(c + 1, 1 - slot).start()
        gu_copy(c, slot).wait()
        wgu_ref[:, c * gu_cols:(c + 1) * gu_cols] = gu_stage[slot].astype(BF16)
        d_copy(c, slot).wait()
        wd_ref[c * d_rows:(c + 1) * d_rows, :] = d_stage[slot].astype(BF16)


def _ffn_kernel(x_ref, g_ref, wgu_hbm, wd_hbm, o_ref, wgu_ref, wd_ref, gu_stage, d_stage, sem, *, layer):
    @pl.when(pl.program_id(0) == 0)
    def _():
        _load_ffn_weights(layer, wgu_hbm, wd_hbm, wgu_ref, wd_ref, gu_stage, d_stage, sem)

    _ffn_body(x_ref[...], g_ref, wgu_ref, wd_ref, o_ref)


def _mix_ffn_kernel(x_ref, a_ref, b_ref, wa_ref, wb_ref, g_ref, wgu_hbm, wd_hbm, o_ref,
                    wgu_ref, wd_ref, gu_stage, d_stage, sem, *, layer):
    @pl.when(pl.program_id(0) == 0)
    def _():
        _load_ffn_weights(layer, wgu_hbm, wd_hbm, wgu_ref, wd_ref, gu_stage, d_stage, sem)

    x = x_ref[...] + _dot(a_ref[...], wa_ref[...]) + _dot(b_ref[...], wb_ref[...])
    _ffn_body(x, g_ref, wgu_ref, wd_ref, o_ref)


def _layer_spec(shape, layer):
    nd = len(shape) - 1
    return pl.BlockSpec((None,) + tuple(shape[1:]), lambda *_: (layer,) + (0,) * nd,
                        pipeline_mode=pl.Buffered(1))


def _ffn_call(x, layer, gains, w_gu, w_down, mix=None):
    t = x.shape[0]
    tm = FFN_TM
    row = lambda i: (i, 0)
    x_spec = pl.BlockSpec((tm, D_MODEL), row)
    hbm = pl.BlockSpec(memory_space=pl.ANY)
    w_args = [gains, w_gu, w_down]
    w_specs = [_layer_spec(gains.shape, layer), hbm, hbm]
    if mix is None:
        kern, in_specs, args = _ffn_kernel, [x_spec] + w_specs, [x] + w_args
    else:
        a, b, wa, wb = mix
        kern = _mix_ffn_kernel
        in_specs = [x_spec, pl.BlockSpec((tm, a.shape[1]), row), pl.BlockSpec((tm, b.shape[1]), row),
                    _const_spec(wa.shape), _const_spec(wb.shape)] + w_specs
        args = [x, a, b, wa, wb] + w_args
    return pl.pallas_call(
        functools.partial(kern, layer=layer), grid=(t // tm,), in_specs=in_specs,
        out_specs=pl.BlockSpec((tm, D_MODEL), row),
        out_shape=jax.ShapeDtypeStruct((t, D_MODEL), F32),
        scratch_shapes=[pltpu.VMEM((D_MODEL, 2 * D_FF), BF16), pltpu.VMEM((D_FF, D_MODEL), BF16),
                        pltpu.VMEM((2, D_MODEL, 2 * D_FF // FFN_CAST_CHUNKS), F32),
                        pltpu.VMEM((2, D_FF // FFN_CAST_CHUNKS, D_MODEL), F32),
                        pltpu.SemaphoreType.DMA((2, 2))],
        compiler_params=_params(("arbitrary",)),
        name="ffn" if mix is None else "mix_ffn",
    )(*args)


def _gelu_tanh(x):
    c = math.sqrt(2.0 / math.pi)
    return 0.5 * x * (1.0 + jnp.tanh(c * (x + 0.044715 * (x * x * x))))


def _even_proj_kernel(x_ref, g_ref, win_ref, sgu_ref, wpair_ref, bias_ref, qg_ref, kg_ref, perm_ref,
                      cos_ref, sin_ref, oa_ref, q_ref, k_ref, v_ref):
    xn = _rms_rows(x_ref[...], g_ref[...]).astype(BF16)
    proj = _dot(xn, win_ref[...])
    w = GMLP_WIDTH
    nblk = w // LANES
    tm = proj.shape[0]
    for j in range(nblk):
        u = _gelu_tanh(proj[:, j * LANES:(j + 1) * LANES])
        v = _gelu_tanh(proj[:, w + j * LANES:w + (j + 1) * LANES])
        vn = v * _inv_rms_half_blocks(v) * sgu_ref[:, j * LANES:(j + 1) * LANES]
        low = _lane_is_low(vn.shape)
        vn_lo = jnp.where(low, vn, 0.0).astype(BF16)
        vn_hi = jnp.where(low, 0.0, vn).astype(BF16)
        wp = wpair_ref[j]
        bias = bias_ref[:, j * LANES:(j + 1) * LANES]
        for c in range(tm // GMLP_CHUNK):
            rows = slice(c * GMLP_CHUNK, (c + 1) * GMLP_CHUNK)
            stacked = jnp.concatenate([vn_lo[rows], vn_hi[rows]], axis=0)
            mixed = _dot(wp, stacked) + bias
            oa_ref[rows, j * LANES:(j + 1) * LANES] = (u[rows] * mixed).astype(BF16)
    cos, sin = cos_ref[...], sin_ref[...]
    q_cos, q_sin = qg_ref[0:1] * cos, qg_ref[1:2] * sin
    k_cos, k_sin = kg_ref[0:1] * cos, kg_ref[1:2] * sin
    pw = perm_ref.shape[0]
    partner = jnp.concatenate(
        [_dot(proj[:, 2 * w + c * pw:2 * w + (c + 1) * pw].astype(BF16), perm_ref[...])
         for c in range(2 * w // pw)], axis=1)
    for j in range(nblk):
        blk = slice(j * LANES, (j + 1) * LANES)
        qb = proj[:, 2 * w + j * LANES:2 * w + (j + 1) * LANES]
        kb = proj[:, 3 * w + j * LANES:3 * w + (j + 1) * LANES]
        qp = partner[:, j * LANES:(j + 1) * LANES]
        kp = partner[:, w + j * LANES:w + (j + 1) * LANES]
        qr = _inv_rms_half_blocks(qb) * (LOG2E * DIFF_D ** -0.5)
        q_ref[:, blk] = ((qb * q_cos + qp * q_sin) * qr).astype(BF16)
        k_ref[:, blk] = ((kb * k_cos + kp * k_sin) * _inv_rms_half_blocks(kb)).astype(BF16)
    v_ref[...] = proj[:, 4 * w:].astype(BF16)


def _gain_pair(g, group):
    return jnp.stack([jnp.tile(g, 2), jnp.tile(_swap_pairs(g, group), 2)])


def _even_proj_call(x, seq, gain, w_in, sgu_norm, w_s, b_s, q_norm, k_norm, tables):
    t = x.shape[0]
    tm = EVEN_PROJ_TM
    w = GMLP_WIDTH
    row = lambda i: (i, 0)
    pos = lambda i: (i % (seq // tm), 0)
    wpair = jnp.concatenate([w_s[0::2], w_s[1::2]], axis=2).astype(BF16)
    bias = jnp.repeat(b_s.T, GMLP_GROUP_DIM, axis=1)
    perm = _swap_pairs(jnp.eye(2 * LANES, dtype=F32), DIFF_D).astype(BF16)
    args = [x, gain.reshape(1, D_MODEL), w_in.astype(BF16), sgu_norm.reshape(1, w), wpair, bias,
            _gain_pair(q_norm, DIFF_D), _gain_pair(k_norm, DIFF_D), perm, *tables]
    in_specs = [pl.BlockSpec((tm, D_MODEL), row)] + [_const_spec(a.shape) for a in args[1:9]]
    in_specs += [pl.BlockSpec((tm, LANES), pos)] * 2
    out = jax.ShapeDtypeStruct((t, w), BF16)
    return pl.pallas_call(
        _even_proj_kernel, grid=(t // tm,), in_specs=in_specs,
        out_specs=[pl.BlockSpec((tm, w), row)] * 4, out_shape=[out] * 4,
        compiler_params=_params(("parallel",)), name="even_proj",
    )(*args)


def _odd_proj_kernel(x_ref, g_ref, win_ref, cqg_ref, ckvg_ref, wuq_ref, wuk_ref, wuv_ref,
                     mqg_ref, mkg_ref, gqg_ref, gkg_ref, mcos_ref, msin_ref, acos_ref, asin_ref,
                     qc_ref, kc_ref, vc_ref, qd_ref, kd_ref, vd_ref):
    xn = _rms_rows(x_ref[...], g_ref[...]).astype(BF16)
    proj = _dot(xn, win_ref[...])
    nq = GQA_Q_HEADS * GQA_DIM
    o1 = MLA_Q_RANK
    o2 = o1 + MLA_KV_RANK
    o3 = o2 + 2 * LANES
    o4 = o3 + 2 * nq
    o5 = o4 + 2 * LANES
    mcos, msin = mcos_ref[...], msin_ref[...]
    q_cos, q_sin = mqg_ref[0:1] * mcos, mqg_ref[1:2] * msin
    k_cos, k_sin = mkg_ref[0:1] * mcos, mkg_ref[1:2] * msin
    cq = _rms_rows(proj[:, :o1], cqg_ref[...]).astype(BF16)
    q_all = _dot(cq, wuq_ref[...])
    ckv = _rms_rows(proj[:, o1:o2], ckvg_ref[...]).astype(BF16)
    kn_all = _dot(ckv, wuk_ref[...])
    vc_ref[...] = _dot(ckv, wuv_ref[...]).astype(BF16)
    kpe = proj[:, o2:o2 + LANES]
    kpe_sin = proj[:, o2 + LANES:o3] * k_sin
    nh = MLA_HEADS * LANES
    for h in range(MLA_HEADS):
        blk = slice(h * LANES, (h + 1) * LANES)
        qb = q_all[:, blk]
        qp = q_all[:, nh + h * LANES:nh + (h + 1) * LANES]
        qr = _inv_rms_padded_block(qb, MLA_QK) * (LOG2E * MLA_QK ** -0.5)
        qc_ref[:, blk] = ((qb * q_cos + qp * q_sin) * qr).astype(BF16)
        kb = kn_all[:, blk] + kpe
        kc_ref[:, blk] = ((kb * k_cos + kpe_sin) * _inv_rms_padded_block(kb, MLA_QK)).astype(BF16)
    acos, asin = acos_ref[...], asin_ref[...]
    q_cos, q_sin = gqg_ref[0:1] * acos, gqg_ref[1:2] * asin
    for j in range(nq // LANES):
        blk = slice(j * LANES, (j + 1) * LANES)
        qb = proj[:, o3 + j * LANES:o3 + (j + 1) * LANES]
        qp = proj[:, o3 + nq + j * LANES:o3 + nq + (j + 1) * LANES]
        qr = _inv_rms_half_blocks(qb) * (LOG2E * GQA_DIM ** -0.5)
        qd_ref[:, blk] = ((qb * q_cos + qp * q_sin) * qr).astype(BF16)
    kb = proj[:, o4:o4 + LANES]
    kp = proj[:, o4 + LANES:o5]
    kd_ref[...] = ((kb * (gkg_ref[0:1] * acos) + kp * (gkg_ref[1:2] * asin))
                   * _inv_rms_half_blocks(kb)).astype(BF16)
    vd_ref[...] = proj[:, o5:].astype(BF16)


def _pad_heads(w, heads, width):
    r = w.shape[0]
    w = w.reshape(r, heads, width)
    return jnp.pad(w, ((0, 0), (0, 0), (0, LANES - width))).reshape(r, heads * LANES)


def _rope_partner_cols(w, heads):
    r = w.shape[0]
    w = w.reshape(r, heads, MLA_QK)
    rope = _swap_pairs(w[:, :, MLA_NOPE:], MLA_ROPE)
    return jnp.concatenate([jnp.zeros_like(w[:, :, :MLA_NOPE]), rope], axis=2).reshape(r, heads * MLA_QK)


def _odd_proj_call(x, seq, gain, w_in, cq_norm, ckv_norm, w_uq, w_ukv, mq_norm, mk_norm,
                   gq_norm, gk_norm, mla_tables, axial_tables):
    t = x.shape[0]
    tm = ODD_PROJ_TM
    row = lambda i: (i, 0)
    pos = lambda i: (i % (seq // tm), 0)
    o1 = MLA_Q_RANK
    o2 = o1 + MLA_KV_RANK
    o3 = o2 + MLA_ROPE
    o4 = o3 + GQA_Q_HEADS * GQA_DIM
    o5 = o4 + GQA_KV_HEADS * GQA_DIM
    place_rope = lambda c: jnp.pad(c, ((0, 0), (MLA_NOPE, LANES - MLA_QK)))
    kpe_cols = w_in[:, o2:o3]
    gq_cols = w_in[:, o3:o4].reshape(D_MODEL, GQA_KV_HEADS, GQA_GROUP, GQA_DIM)
    gq_cols = gq_cols.transpose(0, 2, 1, 3).reshape(D_MODEL, GQA_Q_HEADS * GQA_DIM)
    gk_cols = w_in[:, o4:o5]
    half = GQA_DIM // 2
    win = jnp.concatenate(
        [w_in[:, :o2], place_rope(kpe_cols), place_rope(_swap_pairs(kpe_cols, MLA_ROPE)),
         gq_cols, _swap_pairs(gq_cols, half), gk_cols, _swap_pairs(gk_cols, half), w_in[:, o5:]],
        axis=1).astype(BF16)
    wuq = jnp.concatenate([_pad_heads(w_uq, MLA_HEADS, MLA_QK),
                           _pad_heads(_rope_partner_cols(w_uq, MLA_HEADS), MLA_HEADS, MLA_QK)],
                          axis=1).astype(BF16)
    w_ukv = w_ukv.reshape(MLA_KV_RANK, MLA_HEADS, MLA_NOPE + MLA_V)
    wuk = _pad_heads(w_ukv[:, :, :MLA_NOPE].reshape(MLA_KV_RANK, -1), MLA_HEADS, MLA_NOPE).astype(BF16)
    wuv = w_ukv[:, :, MLA_NOPE:].reshape(MLA_KV_RANK, MLA_HEADS * MLA_V).astype(BF16)

    def mla_gain_pair(g):
        partner = jnp.concatenate([g[:MLA_NOPE], _swap_pairs(g[MLA_NOPE:], MLA_ROPE)])
        return jnp.pad(jnp.stack([g, partner]), ((0, 0), (0, LANES - MLA_QK)))

    args = [x, gain.reshape(1, D_MODEL), win, cq_norm.reshape(1, -1), ckv_norm.reshape(1, -1),
            wuq, wuk, wuv, mla_gain_pair(mq_norm), mla_gain_pair(mk_norm),
            _gain_pair(gq_norm, half), _gain_pair(gk_norm, half), *mla_tables, *axial_tables]
    in_specs = ([pl.BlockSpec((tm, D_MODEL), row)]
                + [_const_spec(a.shape) for a in args[1:12]]
                + [pl.BlockSpec((tm, LANES), pos)] * 4)
    widths = [MLA_HEADS * LANES, MLA_HEADS * LANES, MLA_HEADS * MLA_V,
              GQA_Q_HEADS * GQA_DIM, LANES, LANES]
    return pl.pallas_call(
        _odd_proj_kernel, grid=(t // tm,), in_specs=in_specs,
        out_specs=[pl.BlockSpec((tm, n), row) for n in widths],
        out_shape=[jax.ShapeDtypeStruct((t, n), BF16) for n in widths],
        compiler_params=_params(("parallel",)), name="odd_proj",
    )(*args)


def _attention_kernel(*refs, mode, lam_init, stabilised):
    if mode == "diff":
        q_ref, k_ref, v_ref, lam_ref, sub_ref, o_ref = refs[:6]
    else:
        q_ref, k_ref, v_ref, o_ref = refs[:4]
    qt_ref, tile_buf, stat_ref, acc_ref = refs[-4:]
    tq = q_ref.shape[0]
    seq = k_ref.shape[0]
    tk = ATT_TK
    cw = ATT_CW
    n = seq // tk
    q = q_ref[...].astype(F32)
    if mode == "mla":
        qt_ref[:, :tq] = q[:, :LANES].T.astype(BF16)
        qt_ref[:, tq:] = q[:, LANES:].T.astype(BF16)
    else:
        qt = q.T
        first = lax.broadcasted_iota(jnp.int32, qt.shape, 0) < (LANES // 2)
        qt_ref[:, :tq] = jnp.where(first, qt, 0.0).astype(BF16)
        qt_ref[:, tq:] = jnp.where(first, 0.0, qt).astype(BF16)

    chunks = [slice(c * cw, (c + 1) * cw) for c in range(2 * tq // cw)]

    def tile_rows(t):
        return pl.ds(pl.multiple_of(t * tk, tk), tk)

    def ahead(kt, slot, cols):
        kc = kt
        if mode == "mla":
            kc = kt[:, :LANES] if cols.start < tq else kt[:, LANES:]
        s = _dot(kc, qt_ref[:, cols])
        if stabilised:
            tile_buf[slot, :, cols] = s
        else:
            p = jnp.exp2(s)
            stat_ref[:, cols] += jnp.sum(p.reshape(tk // 8, 8, p.shape[1]), axis=0)
            tile_buf[slot, :, cols] = p.astype(BF16)

    def consume(vt, slot, cols):
        if stabilised:
            s = tile_buf[slot, :, cols]
            m_old = stat_ref[:, cols]
            m_new = jnp.maximum(m_old, jnp.max(s, axis=0, keepdims=True))
            alpha = jnp.exp2(m_old - m_new)
            p = jnp.exp2(s - m_new).astype(BF16)
            stat_ref[:, cols] = m_new
            acc_ref[:, cols] = alpha * acc_ref[:, cols] + _dot(vt, p)
        else:
            acc_ref[:, cols] += _dot(vt, tile_buf[slot, :, cols])

    def value_rows(t):
        vt = v_ref[tile_rows(t), :].T
        parts = (vt, vt) if mode == "diff" else (vt[:LANES // 2], vt[LANES // 2:])
        if not stabilised:
            return parts
        ones = jnp.ones((SUM_ROWS, tk), BF16)
        return tuple(jnp.concatenate([part, ones], axis=0) for part in parts)

    def half_step(t, t_next, cur, nxt):
        kt = k_ref[tile_rows(t_next), :]
        vt = value_rows(t)
        for cols in chunks:
            ahead(kt, nxt, cols)
            consume(vt[0] if cols.start < tq else vt[1], cur, cols)

    stat_ref[...] = jnp.full(stat_ref.shape, -jnp.inf if stabilised else 0.0, F32)
    acc_ref[...] = jnp.zeros(acc_ref.shape, F32)
    kt0 = k_ref[tile_rows(0), :]
    for cols in chunks:
        ahead(kt0, 0, cols)

    def body(i, carry):
        for u in range(ATT_PAIRS):
            t = 2 * (i * ATT_PAIRS + u)
            half_step(t, t + 1, 0, 1)
            half_step(t + 1, t + 2, 1, 0)
        return carry

    lax.fori_loop(0, (n - 2) // (2 * ATT_PAIRS), body, 0)
    half_step(n - 2, n - 1, 0, 1)
    vt_last = value_rows(n - 1)
    for cols in chunks:
        consume(vt_last[0] if cols.start < tq else vt_last[1], 1, cols)

    if stabilised:
        nv = acc_ref.shape[0] - SUM_ROWS
        o = acc_ref[:nv, :] / acc_ref[nv:nv + 1, :]
    else:
        o = acc_ref[...] / jnp.sum(stat_ref[...], axis=0, keepdims=True)
    if mode == "diff":
        lv = lam_ref[...]
        lam = (jnp.exp(jnp.sum(lv[0:1] * lv[1:2], axis=-1, keepdims=True))
               - jnp.exp(jnp.sum(lv[2:3] * lv[3:4], axis=-1, keepdims=True)) + lam_init)
        ot = o[:, :tq] - lam * o[:, tq:]
        ms = jnp.mean(ot * ot, axis=0, keepdims=True)
        ot = ot * lax.rsqrt(ms + EPS) * (sub_ref[...] * (1.0 - lam_init))
        o_ref[...] = ot.T.astype(o_ref.dtype)
    else:
        ot = jnp.concatenate([o[:, :tq], o[:, tq:]], axis=0).T
        o_ref[...] = ot.astype(o_ref.dtype)


def _attention_call(q, k, v, mode, score_bound, extra=(), lam_init=0.0):
    b, seq, _ = q.shape
    qw = 2 * LANES if mode == "mla" else LANES
    nblk = q.shape[2] // qw
    tq, tk = ATT_TQ, ATT_TK
    assert seq % tk == 0 and (seq // tk - 2) % (2 * ATT_PAIRS) == 0 and seq % tq == 0 and tq % ATT_CW == 0
    shared = mode == "gqa"
    value_rows = LANES if mode == "diff" else LANES // 2
    kv_idx = (lambda bi, j, i: (bi, 0, 0)) if shared else (lambda bi, j, i: (bi, 0, j))
    in_specs = [pl.BlockSpec((None, tq, qw), lambda bi, j, i: (bi, i, j)),
                pl.BlockSpec((None, seq, qw), kv_idx),
                pl.BlockSpec((None, seq, LANES), kv_idx)]
    in_specs += [_const_spec(e.shape) for e in extra]

    def call(stabilised):
        if stabilised:
            scratch = [pltpu.VMEM((2, tk, 2 * tq), F32), pltpu.VMEM((1, 2 * tq), F32),
                       pltpu.VMEM((value_rows + SUM_ROWS, 2 * tq), F32)]
        else:
            scratch = [pltpu.VMEM((2, tk, 2 * tq), BF16), pltpu.VMEM((8, 2 * tq), F32),
                       pltpu.VMEM((value_rows, 2 * tq), F32)]
        return pl.pallas_call(
            functools.partial(_attention_kernel, mode=mode, lam_init=lam_init, stabilised=stabilised),
            grid=(b, nblk, seq // tq), in_specs=in_specs,
            out_specs=pl.BlockSpec((None, tq, LANES), lambda bi, j, i: (bi, i, j)),
            out_shape=jax.ShapeDtypeStruct((b, seq, nblk * LANES), BF16),
            scratch_shapes=[pltpu.VMEM((LANES, 2 * tq), BF16)] + scratch,
            compiler_params=_params(("parallel", "parallel", "arbitrary")),
            name="attn_" + mode + ("_stab" if stabilised else ""),
        )

    return lax.cond(score_bound <= SAFE_SCORE_RANGE, call(False), call(True), q, k, v, *extra)


def _score_bound(q_gain, k_gain, dim):
    return (jnp.max(jnp.abs(q_gain)) * jnp.max(jnp.abs(k_gain))
            * (dim * dim ** -0.5 * LOG2E * ROUNDING_SLACK))


def _rope_tables(seq):
    assert MLA_ROPE == GQA_DIM // 2 and seq % GRID_W == 0
    pos = jnp.arange(seq, dtype=jnp.int32)
    zeros = lambda n: jnp.zeros((seq, n), F32)

    def angles(dim):
        inv = ROPE_THETA ** (-jnp.arange(0, dim, 2, dtype=F32) / dim)
        ang = pos.astype(F32)[:, None] * inv[None, :]
        return lax.optimization_barrier((jnp.cos(ang), jnp.sin(ang)))

    c, s = angles(DIFF_D)
    full = (jnp.tile(jnp.concatenate([c, c], 1), (1, 2)), jnp.tile(jnp.concatenate([-s, s], 1), (1, 2)))
    c, s = angles(MLA_ROPE)
    tail = zeros(LANES - MLA_QK)
    mla = (jnp.concatenate([jnp.ones((seq, MLA_NOPE), F32), c, c, tail], 1),
           jnp.concatenate([zeros(MLA_NOPE), -s, s, tail], 1))
    rows = seq // GRID_W
    cr, sr = jnp.repeat(c[:rows], GRID_W, axis=0), jnp.repeat(s[:rows], GRID_W, axis=0)
    cc, sc = jnp.tile(c[:GRID_W], (rows, 1)), jnp.tile(s[:GRID_W], (rows, 1))
    axial = (jnp.tile(jnp.concatenate([cr, cr, cc, cc], 1), (1, 2)),
             jnp.tile(jnp.concatenate([-sr, sr, -sc, sc], 1), (1, 2)))
    return full, mla, axial


def kernel(x, ffn1_norm, ffn1_w_gu, ffn1_w_down, ffn2_norm, ffn2_w_gu, ffn2_w_down, ev_norm, ev_w_in, ev_sgu_norm, ev_w_s, ev_b_s, ev_q_norm, ev_k_norm, ev_lam_q1, ev_lam_k1, ev_lam_q2, ev_lam_k2, ev_sub_norm, ev_w_out, od_norm, od_w_in, od_cq_norm, od_ckv_norm, od_w_uq, od_w_ukv, od_mla_q_norm, od_mla_k_norm, od_gqa_q_norm, od_gqa_k_norm, od_w_out):
    b, seq, d = x.shape
    t = b * seq
    full_tab, mla_tab, axial_tab = _rope_tables(seq)
    x2 = x.reshape(t, d)
    ffn1 = (ffn1_norm[:, None, :], ffn1_w_gu, ffn1_w_down)
    ffn2 = (ffn2_norm[:, None, :], ffn2_w_gu, ffn2_w_down)

    x2 = _ffn_call(x2, 0, *ffn1)
    out_a, q, k, v = _even_proj_call(x2, seq, ev_norm[0], ev_w_in[0], ev_sgu_norm[0], ev_w_s[0],
                                     ev_b_s[0], ev_q_norm[0], ev_k_norm[0], full_tab)
    lam_init = 0.8 - 0.6 * float(np.exp(-0.3 * 0))
    lam_vecs = jnp.stack([ev_lam_q1[0], ev_lam_k1[0], ev_lam_q2[0], ev_lam_k2[0]])
    shp = (b, seq, GMLP_WIDTH)
    out_b = _attention_call(q.reshape(shp), k.reshape(shp), v.reshape(shp), "diff",
                            _score_bound(ev_q_norm[0], ev_k_norm[0], DIFF_D), extra=(lam_vecs, ev_sub_norm[0].reshape(DIFF_V, 1)), lam_init=lam_init)
    w_out = ev_w_out[0].astype(BF16)
    x2 = _ffn_call(x2, 0, *ffn2,
                   mix=(out_a, out_b.reshape(t, -1), w_out[:GMLP_WIDTH], w_out[GMLP_WIDTH:]))

    x2 = _ffn_call(x2, 1, *ffn1)
    qc, kc, vc, qd, kd, vd = _odd_proj_call(
        x2, seq, od_norm[0], od_w_in[0], od_cq_norm[0], od_ckv_norm[0], od_w_uq[0], od_w_ukv[0],
        od_mla_q_norm[0], od_mla_k_norm[0], od_gqa_q_norm[0], od_gqa_k_norm[0], mla_tab, axial_tab)
    r3 = lambda a: a.reshape(b, seq, a.shape[1])
    out_c = _attention_call(r3(qc), r3(kc), r3(vc), "mla",
                            _score_bound(od_mla_q_norm[0], od_mla_k_norm[0], MLA_QK))
    out_d = _attention_call(r3(qd), r3(kd), r3(vd), "gqa",
                            _score_bound(od_gqa_q_norm[0], od_gqa_k_norm[0], GQA_DIM))
    w_out = od_w_out[0].astype(BF16)
    n_c = MLA_HEADS * MLA_V
    w_d = w_out[n_c:].reshape(GQA_KV_HEADS, GQA_GROUP, GQA_DIM, d).transpose(1, 0, 2, 3)
    x2 = _ffn_call(x2, 1, *ffn2,
                   mix=(out_c.reshape(t, -1), out_d.reshape(t, -1), w_out[:n_c],
                        w_d.reshape(GQA_Q_HEADS * GQA_DIM, d)))
    return x2.reshape(b, seq, d)
```

```python
import functools
import math

import numpy as np
import jax
import jax.numpy as jnp
from jax import lax
from jax.experimental import pallas as pl
from jax.experimental.pallas import tpu as pltpu

D_MODEL = 1024
D_FF = 2816
ROPE_THETA = 10000.0
GRID_W = 64
EPS = 1e-6
GMLP_GROUPS = 8
GMLP_GROUP_DIM = 64
GMLP_CHUNK = 128
GMLP_WIDTH = GMLP_GROUPS * GMLP_GROUP_DIM
DIFF_D = 64
DIFF_V = 128
MLA_HEADS = 8
MLA_Q_RANK = 256
MLA_KV_RANK = 128
MLA_NOPE = 64
MLA_ROPE = 32
MLA_V = 64
MLA_QK = MLA_NOPE + MLA_ROPE
GQA_Q_HEADS = 8
GQA_KV_HEADS = 2
GQA_GROUP = GQA_Q_HEADS // GQA_KV_HEADS
GQA_DIM = 64

LANES = 128
VMEM_LIMIT_BYTES = 56 * 1024 * 1024

FFN_TM = 1024
FFN_SUB = 256
FFN_CAST_CHUNKS = 11
EVEN_PROJ_TM = 512
ODD_PROJ_TM = 1024
ATT_TQ = 2048
ATT_TK = 256
ATT_CW = 512
ATT_PAIRS = 5
SUM_ROWS = 16
LOG2E = math.log2(math.e)
SAFE_SCORE_RANGE = 60.0
ROUNDING_SLACK = 1.0 + 2.0 ** -6

BF16 = jnp.bfloat16
F32 = jnp.float32


def _params(semantics):
    return pltpu.CompilerParams(dimension_semantics=semantics,
                                vmem_limit_bytes=VMEM_LIMIT_BYTES)


def _const_spec(shape):
    nd = len(shape)
    return pl.BlockSpec(shape, lambda *_: (0,) * nd, pipeline_mode=pl.Buffered(1))


def _rms_rows(x, gain):
    ms = jnp.mean(x * x, axis=-1, keepdims=True)
    return x * lax.rsqrt(ms + EPS) * gain


def _dot(a, b):
    return jnp.dot(a, b, preferred_element_type=F32)


def _lane_is_low(shape):
    return lax.broadcasted_iota(jnp.int32, shape, len(shape) - 1) < (LANES // 2)


def _inv_rms_half_blocks(x):
    low = _lane_is_low(x.shape)
    x2 = x * x
    x2_lo = jnp.where(low, x2, 0.0)
    x2_hi = x2 - x2_lo
    ms_lo = jnp.sum(x2_lo, axis=-1, keepdims=True) * (2.0 / LANES)
    ms_hi = jnp.sum(x2_hi, axis=-1, keepdims=True) * (2.0 / LANES)
    return jnp.where(low, lax.rsqrt(ms_lo + EPS), lax.rsqrt(ms_hi + EPS))


def _inv_rms_padded_block(x, width):
    ms = jnp.sum(x * x, axis=-1, keepdims=True) * (1.0 / width)
    return lax.rsqrt(ms + EPS)


def _swap_pairs(a, group):
    shape = a.shape
    a = a.reshape(shape[:-1] + (shape[-1] // group, 2, group // 2))
    return jnp.flip(a, axis=-2).reshape(shape)


def _ffn_body(x, g_ref, wgu_ref, wd_ref, o_ref):
    for r in range(x.shape[0] // FFN_SUB):
        rows = slice(r * FFN_SUB, (r + 1) * FFN_SUB)
        xr = x[rows]
        xn = _rms_rows(xr, g_ref[...]).astype(BF16)
        h = _dot(xn, wgu_ref[...])
        gate = h[:, :D_FF]
        up = h[:, D_FF:]
        act = (gate / (1.0 + jnp.exp(-gate)) * up).astype(BF16)
        o_ref[rows, :] = xr + 0.5 * _dot(act, wd_ref[...])


def _load_ffn_weights(layer, wgu_hbm, wd_hbm, wgu_ref, wd_ref, gu_stage, d_stage, sem):
    gu_cols = 2 * D_FF // FFN_CAST_CHUNKS
    d_rows = D_FF // FFN_CAST_CHUNKS

    def gu_copy(c, slot):
        return pltpu.make_async_copy(wgu_hbm.at[layer, :, pl.ds(c * gu_cols, gu_cols)],
                                     gu_stage.at[slot], sem.at[0, slot])

    def d_copy(c, slot):
        return pltpu.make_async_copy(wd_hbm.at[layer, pl.ds(c * d_rows, d_rows), :],
                                     d_stage.at[slot], sem.at[1, slot])

    gu_copy(0, 0).start()
    d_copy(0, 0).start()
    for c in range(FFN_CAST_CHUNKS):
        slot = c % 2
        if c + 1 < FFN_CAST_CHUNKS:
            gu_copy(c + 1, 1 - slot).start()
            d_copy(c + 1, 1 - slot).start()
        gu_copy(c, slot).wait()
        wgu_ref[:, c * gu_cols:(c + 1) * gu_cols] = gu_stage[slot].astype(BF16)
        d_copy(c, slot).wait()
        wd_ref[c * d_rows:(c + 1) * d_rows, :] = d_stage[slot].astype(BF16)


def _ffn_kernel(x_ref, g_ref, wgu_hbm, wd_hbm, o_ref, wgu_ref, wd_ref, gu_stage, d_stage, sem, *, layer):
    @pl.when(pl.program_id(0) == 0)
    def _():
        _load_ffn_weights(layer, wgu_hbm, wd_hbm, wgu_ref, wd_ref, gu_stage, d_stage, sem)

    _ffn_body(x_ref[...], g_ref, wgu_ref, wd_ref, o_ref)


def _mix_ffn_kernel(x_ref, a_ref, b_ref, wa_ref, wb_ref, g_ref, wgu_hbm, wd_hbm, o_ref,
                    wgu_ref, wd_ref, gu_stage, d_stage, sem, *, layer):
    @pl.when(pl.program_id(0) == 0)
    def _():
        _load_ffn_weights(layer, wgu_hbm, wd_hbm, wgu_ref, wd_ref, gu_stage, d_stage, sem)

    x = x_ref[...] + _dot(a_ref[...], wa_ref[...]) + _dot(b_ref[...], wb_ref[...])
    _ffn_body(x, g_ref, wgu_ref, wd_ref, o_ref)


def _layer_spec(shape, layer):
    nd = len(shape) - 1
    return pl.BlockSpec((None,) + tuple(shape[1:]), lambda *_: (layer,) + (0,) * nd,
                        pipeline_mode=pl.Buffered(1))


def _ffn_call(x, layer, gains, w_gu, w_down, mix=None):
    t = x.shape[0]
    tm = FFN_TM
    row = lambda i: (i, 0)
    x_spec = pl.BlockSpec((tm, D_MODEL), row)
    hbm = pl.BlockSpec(memory_space=pl.ANY)
    w_args = [gains, w_gu, w_down]
    w_specs = [_layer_spec(gains.shape, layer), hbm, hbm]
    if mix is None:
        kern, in_specs, args = _ffn_kernel, [x_spec] + w_specs, [x] + w_args
    else:
        a, b, (wa, ia), (wb, ib) = mix

        def half_spec(rows, blk):
            return pl.BlockSpec((rows, D_MODEL), lambda i: (blk, 0), pipeline_mode=pl.Buffered(1))

        kern = _mix_ffn_kernel
        in_specs = [x_spec, pl.BlockSpec((tm, a.shape[1]), row), pl.BlockSpec((tm, b.shape[1]), row),
                    half_spec(a.shape[1], ia), half_spec(b.shape[1], ib)] + w_specs
        args = [x, a, b, wa, wb] + w_args
    return pl.pallas_call(
        functools.partial(kern, layer=layer), grid=(t // tm,), in_specs=in_specs,
        out_specs=pl.BlockSpec((tm, D_MODEL), row),
        out_shape=jax.ShapeDtypeStruct((t, D_MODEL), F32),
        scratch_shapes=[pltpu.VMEM((D_MODEL, 2 * D_FF), BF16), pltpu.VMEM((D_FF, D_MODEL), BF16),
                        pltpu.VMEM((2, D_MODEL, 2 * D_FF // FFN_CAST_CHUNKS), F32),
                        pltpu.VMEM((2, D_FF // FFN_CAST_CHUNKS, D_MODEL), F32),
                        pltpu.SemaphoreType.DMA((2, 2))],
        compiler_params=_params(("arbitrary",)),
        name="ffn" if mix is None else "mix_ffn",
    )(*args)


def _gelu_tanh(x):
    c = math.sqrt(2.0 / math.pi)
    return 0.5 * x * (1.0 + jnp.tanh(c * (x + 0.044715 * (x * x * x))))


def _even_proj_kernel(x_ref, g_ref, win_ref, sgu_ref, wpair_ref, bias_ref, qg_ref, kg_ref, perm_ref,
                      cos_ref, sin_ref, oa_ref, q_ref, k_ref, v_ref):
    xn = _rms_rows(x_ref[...], g_ref[...]).astype(BF16)
    proj = _dot(xn, win_ref[...])
    w = GMLP_WIDTH
    nblk = w // LANES
    tm = proj.shape[0]
    for j in range(nblk):
        u = _gelu_tanh(proj[:, j * LANES:(j + 1) * LANES])
        v = _gelu_tanh(proj[:, w + j * LANES:w + (j + 1) * LANES])
        vn = v * _inv_rms_half_blocks(v) * sgu_ref[:, j * LANES:(j + 1) * LANES]
        low = _lane_is_low(vn.shape)
        vn_lo = jnp.where(low, vn, 0.0).astype(BF16)
        vn_hi = jnp.where(low, 0.0, vn).astype(BF16)
        wp = wpair_ref[j]
        bias = bias_ref[:, j * LANES:(j + 1) * LANES]
        for c in range(tm // GMLP_CHUNK):
            rows = slice(c * GMLP_CHUNK, (c + 1) * GMLP_CHUNK)
            stacked = jnp.concatenate([vn_lo[rows], vn_hi[rows]], axis=0)
            mixed = _dot(wp, stacked) + bias
            oa_ref[rows, j * LANES:(j + 1) * LANES] = (u[rows] * mixed).astype(BF16)
    cos, sin = cos_ref[...], sin_ref[...]
    q_cos, q_sin = qg_ref[0:1] * cos, qg_ref[1:2] * sin
    k_cos, k_sin = kg_ref[0:1] * cos, kg_ref[1:2] * sin
    pw = perm_ref.shape[0]
    partner = jnp.concatenate(
        [_dot(proj[:, 2 * w + c * pw:2 * w + (c + 1) * pw].astype(BF16), perm_ref[...])
         for c in range(2 * w // pw)], axis=1)
    for j in range(nblk):
        blk = slice(j * LANES, (j + 1) * LANES)
        qb = proj[:, 2 * w + j * LANES:2 * w + (j + 1) * LANES]
        kb = proj[:, 3 * w + j * LANES:3 * w + (j + 1) * LANES]
        qp = partner[:, j * LANES:(j + 1) * LANES]
        kp = partner[:, w + j * LANES:w + (j + 1) * LANES]
        qr = _inv_rms_half_blocks(qb) * (LOG2E * DIFF_D ** -0.5)
        q_ref[:, blk] = ((qb * q_cos + qp * q_sin) * qr).astype(BF16)
        k_ref[:, blk] = ((kb * k_cos + kp * k_sin) * _inv_rms_half_blocks(kb)).astype(BF16)
    v_ref[...] = proj[:, 4 * w:].astype(BF16)


def _gain_pair(g, group):
    return jnp.stack([jnp.tile(g, 2), jnp.tile(_swap_pairs(g, group), 2)])


def _even_proj_call(x, seq, gain, w_in, sgu_norm, w_s, b_s, q_norm, k_norm, tables):
    t = x.shape[0]
    tm = EVEN_PROJ_TM
    w = GMLP_WIDTH
    row = lambda i: (i, 0)
    pos = lambda i: (i % (seq // tm), 0)
    wpair = jnp.concatenate([w_s[0::2], w_s[1::2]], axis=2).astype(BF16)
    bias = jnp.repeat(b_s.T, GMLP_GROUP_DIM, axis=1)
    perm = _swap_pairs(jnp.eye(2 * LANES, dtype=F32), DIFF_D).astype(BF16)
    args = [x, gain.reshape(1, D_MODEL), w_in.astype(BF16), sgu_norm.reshape(1, w), wpair, bias,
            _gain_pair(q_norm, DIFF_D), _gain_pair(k_norm, DIFF_D), perm, *tables]
    in_specs = [pl.BlockSpec((tm, D_MODEL), row)] + [_const_spec(a.shape) for a in args[1:9]]
    in_specs += [pl.BlockSpec((tm, LANES), pos)] * 2
    out = jax.ShapeDtypeStruct((t, w), BF16)
    return pl.pallas_call(
        _even_proj_kernel, grid=(t // tm,), in_specs=in_specs,
        out_specs=[pl.BlockSpec((tm, w), row)] * 4, out_shape=[out] * 4,
        compiler_params=_params(("parallel",)), name="even_proj",
    )(*args)


def _odd_proj_kernel(x_ref, g_ref, win_ref, cqg_ref, ckvg_ref, wuq_ref, wuk_ref, wuv_ref,
                     mqg_ref, mkg_ref, gqg_ref, gkg_ref, mcos_ref, msin_ref, acos_ref, asin_ref,
                     qc_ref, kc_ref, vc_ref, qd_ref, kd_ref, vd_ref):
    xn = _rms_rows(x_ref[...], g_ref[...]).astype(BF16)
    proj = _dot(xn, win_ref[...])
    nq = GQA_Q_HEADS * GQA_DIM
    o1 = MLA_Q_RANK
    o2 = o1 + MLA_KV_RANK
    o3 = o2 + 2 * LANES
    o4 = o3 + 2 * nq
    o5 = o4 + 2 * LANES
    mcos, msin = mcos_ref[...], msin_ref[...]
    q_cos, q_sin = mqg_ref[0:1] * mcos, mqg_ref[1:2] * msin
    k_cos, k_sin = mkg_ref[0:1] * mcos, mkg_ref[1:2] * msin
    cq = _rms_rows(proj[:, :o1], cqg_ref[...]).astype(BF16)
    q_all = _dot(cq, wuq_ref[...])
    ckv = _rms_rows(proj[:, o1:o2], ckvg_ref[...]).astype(BF16)
    kn_all = _dot(ckv, wuk_ref[...])
    vc_ref[...] = _dot(ckv, wuv_ref[...]).astype(BF16)
    kpe = proj[:, o2:o2 + LANES]
    kpe_sin = proj[:, o2 + LANES:o3] * k_sin
    nh = MLA_HEADS * LANES
    for h in range(MLA_HEADS):
        blk = slice(h * LANES, (h + 1) * LANES)
        qb = q_all[:, blk]
        qp = q_all[:, nh + h * LANES:nh + (h + 1) * LANES]
        qr = _inv_rms_padded_block(qb, MLA_QK) * (LOG2E * MLA_QK ** -0.5)
        qc_ref[:, blk] = ((qb * q_cos + qp * q_sin) * qr).astype(BF16)
        kb = kn_all[:, blk] + kpe
        kc_ref[:, blk] = ((kb * k_cos + kpe_sin) * _inv_rms_padded_block(kb, MLA_QK)).astype(BF16)
    acos, asin = acos_ref[...], asin_ref[...]
    q_cos, q_sin = gqg_ref[0:1] * acos, gqg_ref[1:2] * asin
    for j in range(nq // LANES):
        blk = slice(j * LANES, (j + 1) * LANES)
        qb = proj[:, o3 + j * LANES:o3 + (j + 1) * LANES]
        qp = proj[:, o3 + nq + j * LANES:o3 + nq + (j + 1) * LANES]
        qr = _inv_rms_half_blocks(qb) * (LOG2E * GQA_DIM ** -0.5)
        qd_ref[:, blk] = ((qb * q_cos + qp * q_sin) * qr).astype(BF16)
    kb = proj[:, o4:o4 + LANES]
    kp = proj[:, o4 + LANES:o5]
    kd_ref[...] = ((kb * (gkg_ref[0:1] * acos) + kp * (gkg_ref[1:2] * asin))
                   * _inv_rms_half_blocks(kb)).astype(BF16)
    vd_ref[...] = proj[:, o5:].astype(BF16)


def _pad_heads(w, heads, width):
    r = w.shape[0]
    w = w.reshape(r, heads, width)
    return jnp.pad(w, ((0, 0), (0, 0), (0, LANES - width))).reshape(r, heads * LANES)


def _rope_partner_cols(w, heads):
    r = w.shape[0]
    w = w.reshape(r, heads, MLA_QK)
    rope = _swap_pairs(w[:, :, MLA_NOPE:], MLA_ROPE)
    return jnp.concatenate([jnp.zeros_like(w[:, :, :MLA_NOPE]), rope], axis=2).reshape(r, heads * MLA_QK)


def _odd_proj_call(x, seq, gain, w_in, cq_norm, ckv_norm, w_uq, w_ukv, mq_norm, mk_norm,
                   gq_norm, gk_norm, mla_tables, axial_tables):
    t = x.shape[0]
    tm = ODD_PROJ_TM
    row = lambda i: (i, 0)
    pos = lambda i: (i % (seq // tm), 0)
    o1 = MLA_Q_RANK
    o2 = o1 + MLA_KV_RANK
    o3 = o2 + MLA_ROPE
    o4 = o3 + GQA_Q_HEADS * GQA_DIM
    o5 = o4 + GQA_KV_HEADS * GQA_DIM
    place_rope = lambda c: jnp.pad(c, ((0, 0), (MLA_NOPE, LANES - MLA_QK)))
    kpe_cols = w_in[:, o2:o3]
    gq_cols = w_in[:, o3:o4].reshape(D_MODEL, GQA_KV_HEADS, GQA_GROUP, GQA_DIM)
    gq_cols = gq_cols.transpose(0, 2, 1, 3).reshape(D_MODEL, GQA_Q_HEADS * GQA_DIM)
    gk_cols = w_in[:, o4:o5]
    half = GQA_DIM // 2
    win = jnp.concatenate(
        [w_in[:, :o2], place_rope(kpe_cols), place_rope(_swap_pairs(kpe_cols, MLA_ROPE)),
         gq_cols, _swap_pairs(gq_cols, half), gk_cols, _swap_pairs(gk_cols, half), w_in[:, o5:]],
        axis=1).astype(BF16)
    wuq = jnp.concatenate([_pad_heads(w_uq, MLA_HEADS, MLA_QK),
                           _pad_heads(_rope_partner_cols(w_uq, MLA_HEADS), MLA_HEADS, MLA_QK)],
                          axis=1).astype(BF16)
    w_ukv = w_ukv.reshape(MLA_KV_RANK, MLA_HEADS, MLA_NOPE + MLA_V)
    wuk = _pad_heads(w_ukv[:, :, :MLA_NOPE].reshape(MLA_KV_RANK, -1), MLA_HEADS, MLA_NOPE).astype(BF16)
    wuv = w_ukv[:, :, MLA_NOPE:].reshape(MLA_KV_RANK, MLA_HEADS * MLA_V).astype(BF16)

    def mla_gain_pair(g):
        partner = jnp.concatenate([g[:MLA_NOPE], _swap_pairs(g[MLA_NOPE:], MLA_ROPE)])
        return jnp.pad(jnp.stack([g, partner]), ((0, 0), (0, LANES - MLA_QK)))

    args = [x, gain.reshape(1, D_MODEL), win, cq_norm.reshape(1, -1), ckv_norm.reshape(1, -1),
            wuq, wuk, wuv, mla_gain_pair(mq_norm), mla_gain_pair(mk_norm),
            _gain_pair(gq_norm, half), _gain_pair(gk_norm, half), *mla_tables, *axial_tables]
    in_specs = ([pl.BlockSpec((tm, D_MODEL), row)]
                + [_const_spec(a.shape) for a in args[1:12]]
                + [pl.BlockSpec((tm, LANES), pos)] * 4)
    widths = [MLA_HEADS * LANES, MLA_HEADS * LANES, MLA_HEADS * MLA_V,
              GQA_Q_HEADS * GQA_DIM, LANES, LANES]
    return pl.pallas_call(
        _odd_proj_kernel, grid=(t // tm,), in_specs=in_specs,
        out_specs=[pl.BlockSpec((tm, n), row) for n in widths],
        out_shape=[jax.ShapeDtypeStruct((t, n), BF16) for n in widths],
        compiler_params=_params(("parallel",)), name="odd_proj",
    )(*args)


def _attention_kernel(*refs, mode, lam_init, stabilised):
    if mode == "diff":
        q_ref, k_ref, v_ref, lam_ref, sub_ref, o_ref = refs[:6]
    else:
        q_ref, k_ref, v_ref, o_ref = refs[:4]
    qt_ref, tile_buf, stat_ref, acc_ref = refs[-4:]
    tq = q_ref.shape[0]
    seq = k_ref.shape[0]
    tk = ATT_TK
    cw = ATT_CW
    n = seq // tk
    q = q_ref[...].astype(F32)
    if mode == "mla":
        qt_ref[:, :tq] = q[:, :LANES].T.astype(BF16)
        qt_ref[:, tq:] = q[:, LANES:].T.astype(BF16)
    else:
        qt = q.T
        first = lax.broadcasted_iota(jnp.int32, qt.shape, 0) < (LANES // 2)
        qt_ref[:, :tq] = jnp.where(first, qt, 0.0).astype(BF16)
        qt_ref[:, tq:] = jnp.where(first, 0.0, qt).astype(BF16)

    chunks = [slice(c * cw, (c + 1) * cw) for c in range(2 * tq // cw)]

    def tile_rows(t):
        return pl.ds(pl.multiple_of(t * tk, tk), tk)

    def ahead(kt, slot, cols):
        kc = kt
        if mode == "mla":
            kc = kt[:, :LANES] if cols.start < tq else kt[:, LANES:]
        s = _dot(kc, qt_ref[:, cols])
        if stabilised:
            tile_buf[slot, :, cols] = s
        else:
            p = jnp.exp2(s)
            stat_ref[:, cols] += jnp.sum(p.reshape(tk // 8, 8, p.shape[1]), axis=0)
            tile_buf[slot, :, cols] = p.astype(BF16)

    def consume(vt, slot, cols):
        if stabilised:
            s = tile_buf[slot, :, cols]
            m_old = stat_ref[:, cols]
            m_new = jnp.maximum(m_old, jnp.max(s, axis=0, keepdims=True))
            alpha = jnp.exp2(m_old - m_new)
            p = jnp.exp2(s - m_new).astype(BF16)
            stat_ref[:, cols] = m_new
            acc_ref[:, cols] = alpha * acc_ref[:, cols] + _dot(vt, p)
        else:
            acc_ref[:, cols] += _dot(vt, tile_buf[slot, :, cols])

    def value_rows(t):
        vt = v_ref[tile_rows(t), :].T
        parts = (vt, vt) if mode == "diff" else (vt[:LANES // 2], vt[LANES // 2:])
        if not stabilised:
            return parts
        ones = jnp.ones((SUM_ROWS, tk), BF16)
        return tuple(jnp.concatenate([part, ones], axis=0) for part in parts)

    def half_step(t, t_next, cur, nxt):
        kt = k_ref[tile_rows(t_next), :]
        vt = value_rows(t)
        for cols in chunks:
            ahead(kt, nxt, cols)
            consume(vt[0] if cols.start < tq else vt[1], cur, cols)

    stat_ref[...] = jnp.full(stat_ref.shape, -jnp.inf if stabilised else 0.0, F32)
    acc_ref[...] = jnp.zeros(acc_ref.shape, F32)
    kt0 = k_ref[tile_rows(0), :]
    for cols in chunks:
        ahead(kt0, 0, cols)

    def body(i, carry):
        for u in range(ATT_PAIRS):
            t = 2 * (i * ATT_PAIRS + u)
            half_step(t, t + 1, 0, 1)
            half_step(t + 1, t + 2, 1, 0)
        return carry

    lax.fori_loop(0, (n - 2) // (2 * ATT_PAIRS), body, 0)
    half_step(n - 2, n - 1, 0, 1)
    vt_last = value_rows(n - 1)
    for cols in chunks:
        consume(vt_last[0] if cols.start < tq else vt_last[1], 1, cols)

    if stabilised:
        nv = acc_ref.shape[0] - SUM_ROWS
        o = acc_ref[:nv, :] / acc_ref[nv:nv + 1, :]
    else:
        o = acc_ref[...] / jnp.sum(stat_ref[...], axis=0, keepdims=True)
    if mode == "diff":
        lv = lam_ref[...]
        lam = (jnp.exp(jnp.sum(lv[0:1] * lv[1:2], axis=-1, keepdims=True))
               - jnp.exp(jnp.sum(lv[2:3] * lv[3:4], axis=-1, keepdims=True)) + lam_init)
        ot = o[:, :tq] - lam * o[:, tq:]
        ms = jnp.mean(ot * ot, axis=0, keepdims=True)
        ot = ot * lax.rsqrt(ms + EPS) * (sub_ref[...] * (1.0 - lam_init))
        o_ref[...] = ot.T.astype(o_ref.dtype)
    else:
        ot = jnp.concatenate([o[:, :tq], o[:, tq:]], axis=0).T
        o_ref[...] = ot.astype(o_ref.dtype)


def _attention_call(q, k, v, mode, score_bound, extra=(), lam_init=0.0):
    b, seq, _ = q.shape
    qw = 2 * LANES if mode == "mla" else LANES
    nblk = q.shape[2] // qw
    tq, tk = ATT_TQ, ATT_TK
    assert seq % tk == 0 and (seq // tk - 2) % (2 * ATT_PAIRS) == 0 and seq % tq == 0 and tq % ATT_CW == 0
    shared = mode == "gqa"
    value_rows = LANES if mode == "diff" else LANES // 2
    kv_idx = (lambda bi, j, i: (bi, 0, 0)) if shared else (lambda bi, j, i: (bi, 0, j))
    in_specs = [pl.BlockSpec((None, tq, qw), lambda bi, j, i: (bi, i, j)),
                pl.BlockSpec((None, seq, qw), kv_idx),
                pl.BlockSpec((None, seq, LANES), kv_idx)]
    in_specs += [_const_spec(e.shape) for e in extra]

    def call(stabilised):
        if stabilised:
            scratch = [pltpu.VMEM((2, tk, 2 * tq), F32), pltpu.VMEM((1, 2 * tq), F32),
                       pltpu.VMEM((value_rows + SUM_ROWS, 2 * tq), F32)]
        else:
            scratch = [pltpu.VMEM((2, tk, 2 * tq), BF16), pltpu.VMEM((8, 2 * tq), F32),
                       pltpu.VMEM((value_rows, 2 * tq), F32)]
        return pl.pallas_call(
            functools.partial(_attention_kernel, mode=mode, lam_init=lam_init, stabilised=stabilised),
            grid=(b, nblk, seq // tq), in_specs=in_specs,
            out_specs=pl.BlockSpec((None, tq, LANES), lambda bi, j, i: (bi, i, j)),
            out_shape=jax.ShapeDtypeStruct((b, seq, nblk * LANES), BF16),
            scratch_shapes=[pltpu.VMEM((LANES, 2 * tq), BF16)] + scratch,
            compiler_params=_params(("parallel", "parallel", "arbitrary")),
            name="attn_" + mode + ("_stab" if stabilised else ""),
        )

    return lax.cond(score_bound <= SAFE_SCORE_RANGE, call(False), call(True), q, k, v, *extra)


def _score_bound(q_gain, k_gain, dim):
    return (jnp.max(jnp.abs(q_gain)) * jnp.max(jnp.abs(k_gain))
            * (dim * dim ** -0.5 * LOG2E * ROUNDING_SLACK))


def _rope_tables(seq):
    assert MLA_ROPE == GQA_DIM // 2 and seq % GRID_W == 0
    pos = jnp.arange(seq, dtype=jnp.int32)
    zeros = lambda n: jnp.zeros((seq, n), F32)

    def angles(dim):
        inv = ROPE_THETA ** (-jnp.arange(0, dim, 2, dtype=F32) / dim)
        ang = pos.astype(F32)[:, None] * inv[None, :]
        return lax.optimization_barrier((jnp.cos(ang), jnp.sin(ang)))

    c, s = angles(DIFF_D)
    full = (jnp.tile(jnp.concatenate([c, c], 1), (1, 2)), jnp.tile(jnp.concatenate([-s, s], 1), (1, 2)))
    c, s = angles(MLA_ROPE)
    tail = zeros(LANES - MLA_QK)
    mla = (jnp.concatenate([jnp.ones((seq, MLA_NOPE), F32), c, c, tail], 1),
           jnp.concatenate([zeros(MLA_NOPE), -s, s, tail], 1))
    rows = seq // GRID_W
    cr, sr = jnp.repeat(c[:rows], GRID_W, axis=0), jnp.repeat(s[:rows], GRID_W, axis=0)
    cc, sc = jnp.tile(c[:GRID_W], (rows, 1)), jnp.tile(s[:GRID_W], (rows, 1))
    axial = (jnp.tile(jnp.concatenate([cr, cr, cc, cc], 1), (1, 2)),
             jnp.tile(jnp.concatenate([-sr, sr, -sc, sc], 1), (1, 2)))
    return full, mla, axial


def kernel(x, ffn1_norm, ffn1_w_gu, ffn1_w_down, ffn2_norm, ffn2_w_gu, ffn2_w_down, ev_norm, ev_w_in, ev_sgu_norm, ev_w_s, ev_b_s, ev_q_norm, ev_k_norm, ev_lam_q1, ev_lam_k1, ev_lam_q2, ev_lam_k2, ev_sub_norm, ev_w_out, od_norm, od_w_in, od_cq_norm, od_ckv_norm, od_w_uq, od_w_ukv, od_mla_q_norm, od_mla_k_norm, od_gqa_q_norm, od_gqa_k_norm, od_w_out):
    b, seq, d = x.shape
    t = b * seq
    full_tab, mla_tab, axial_tab = _rope_tables(seq)
    x2 = x.reshape(t, d)
    ffn1 = (ffn1_norm[:, None, :], ffn1_w_gu, ffn1_w_down)
    ffn2 = (ffn2_norm[:, None, :], ffn2_w_gu, ffn2_w_down)

    x2 = _ffn_call(x2, 0, *ffn1)
    out_a, q, k, v = _even_proj_call(x2, seq, ev_norm[0], ev_w_in[0], ev_sgu_norm[0], ev_w_s[0],
                                     ev_b_s[0], ev_q_norm[0], ev_k_norm[0], full_tab)
    lam_init = 0.8 - 0.6 * float(np.exp(-0.3 * 0))
    lam_vecs = jnp.stack([ev_lam_q1[0], ev_lam_k1[0], ev_lam_q2[0], ev_lam_k2[0]])
    shp = (b, seq, GMLP_WIDTH)
    out_b = _attention_call(q.reshape(shp), k.reshape(shp), v.reshape(shp), "diff",
                            _score_bound(ev_q_norm[0], ev_k_norm[0], DIFF_D), extra=(lam_vecs, ev_sub_norm[0].reshape(DIFF_V, 1)), lam_init=lam_init)
    w_out = ev_w_out[0].astype(BF16)
    x2 = _ffn_call(x2, 0, *ffn2,
                   mix=(out_a, out_b.reshape(t, -1), (w_out, 0), (w_out, 1)))

    x2 = _ffn_call(x2, 1, *ffn1)
    qc, kc, vc, qd, kd, vd = _odd_proj_call(
        x2, seq, od_norm[0], od_w_in[0], od_cq_norm[0], od_ckv_norm[0], od_w_uq[0], od_w_ukv[0],
        od_mla_q_norm[0], od_mla_k_norm[0], od_gqa_q_norm[0], od_gqa_k_norm[0], mla_tab, axial_tab)
    r3 = lambda a: a.reshape(b, seq, a.shape[1])
    out_c = _attention_call(r3(qc), r3(kc), r3(vc), "mla",
                            _score_bound(od_mla_q_norm[0], od_mla_k_norm[0], MLA_QK))
    out_d = _attention_call(r3(qd), r3(kd), r3(vd), "gqa",
                            _score_bound(od_gqa_q_norm[0], od_gqa_k_norm[0], GQA_DIM))
    w_out = od_w_out[0].astype(BF16)
    n_c = MLA_HEADS * MLA_V
    w_d = w_out[n_c:].reshape(GQA_KV_HEADS, GQA_GROUP, GQA_DIM, d).transpose(1, 0, 2, 3)
    x2 = _ffn_call(x2, 1, *ffn2,
                   mix=(out_c.reshape(t, -1), out_d.reshape(t, -1), (w_out, 0),
                        (w_d.reshape(GQA_Q_HEADS * GQA_DIM, d), 0)))
    return x2.reshape(b, seq, d)
```
